```python
import math
import jax, jax.numpy as jnp
from jax import lax
import numpy as np

D_MODEL = 2048
BATCH = 1
SEQ = 16384
DEPTH = 2

D_MIX = D_MODEL
SGU_WIDTH = D_MIX // 4
SGU_HEADS = 4
SGU_HEAD_DIM = SGU_WIDTH // SGU_HEADS
SGU_CHUNK = 128
CONV_CH = D_MIX // 4
CONV_K = 31
ATTN_WIDTH = D_MIX // 2
ATTN_HEADS = 8
ATTN_HEAD_DIM = ATTN_WIDTH // ATTN_HEADS
DILATED_CONFIGS = ((128, 1), (512, 4), (2048, 16))
QBLK = 128
D_IN = 2 * SGU_WIDTH + 2 * CONV_CH + 3 * ATTN_WIDTH
IN_SPLITS = (SGU_WIDTH, 2 * SGU_WIDTH, 2 * SGU_WIDTH + CONV_CH, 2 * SGU_WIDTH + 2 * CONV_CH,
             2 * SGU_WIDTH + 2 * CONV_CH + ATTN_WIDTH, 2 * SGU_WIDTH + 2 * CONV_CH + 2 * ATTN_WIDTH)
N_GROUPS = 4
EXPERTS_PER_GROUP = 8
N_EXPERTS = N_GROUPS * EXPERTS_PER_GROUP
TOP_K = 2
D_EXPERT = D_MODEL // 4
MOE_BLK = 128
EPS = 1e-6

kernel_name = "hymba_gmlp_conformer_dilated_attn_hmoe"


def rms_norm(x, g):
    xf = x.astype(jnp.float32)
    y = xf * lax.rsqrt(jnp.mean(xf * xf, axis=-1, keepdims=True) + EPS)
    return (y * g.astype(jnp.float32)).astype(x.dtype)


def layer_norm(x, g, b):
    xf = x.astype(jnp.float32)
    mu = jnp.mean(xf, axis=-1, keepdims=True)
    xc = xf - mu
    y = xc * lax.rsqrt(jnp.mean(xc * xc, axis=-1, keepdims=True) + EPS)
    return (y * g.astype(jnp.float32) + b.astype(jnp.float32)).astype(x.dtype)


def spatial_gating(u, v, g, w_s, b_s):
    B, S, _ = u.shape
    u = jax.nn.gelu(u)
    v = rms_norm(jax.nn.gelu(v), g)
    vc = v.reshape(B, S // SGU_CHUNK, SGU_CHUNK, SGU_HEADS, SGU_HEAD_DIM)
    causal = jnp.tril(jnp.ones((SGU_CHUNK, SGU_CHUNK), dtype=bool))
    w = jnp.where(causal[None], w_s, jnp.zeros((), w_s.dtype))
    z = jnp.einsum('hij,bnjhd->bnihd', w, vc) + b_s.T[None, None, :, :, None]
    return u * z.reshape(B, S, SGU_WIDTH)


def conformer_conv(a, g, w, b, ln_g, ln_b):
    z = a * jax.nn.sigmoid(g)
    z = lax.conv_general_dilated(z, w[:, None, :], window_strides=(1,),
                                 padding=[(CONV_K - 1, 0)],
                                 dimension_numbers=('NWC', 'WIO', 'NWC'),
                                 feature_group_count=CONV_CH) + b
    return jax.nn.silu(layer_norm(z, ln_g, ln_b))


def dilated_branch(q, k, v, window, dil):
    B, H, S, Dh = q.shape
    span = window // dil
    step = dil * QBLK
    P = -(-S // step) * step
    M = P // dil
    nb = M // QBLK

    def phase(t):
        t = jnp.pad(t, ((0, 0), (0, 0), (0, P - S), (0, 0)))
        t = t.reshape(B, H, M, dil, Dh).transpose(0, 1, 3, 2, 4)
        return t.reshape(B, H, dil, nb, QBLK, Dh)

    def with_prev(t):
        prev = jnp.pad(t[:, :, :, :-1], ((0, 0), (0, 0), (0, 0), (1, 0), (0, 0), (0, 0)))
        return jnp.concatenate([prev, t], axis=4)

    qb = phase(q).astype(jnp.float32)
    kb = with_prev(phase(k)).astype(jnp.float32)
    vb = with_prev(phase(v)).astype(jnp.float32)
    s = jnp.einsum('bhrnid,bhrnjd->bhrnij', qb, kb) * (Dh ** -0.5)
    i = jnp.arange(QBLK)[:, None]
    j = jnp.arange(2 * QBLK)[None, :]
    dist = QBLK + i - j
    blk = jnp.arange(nb)[:, None, None]
    valid = (dist >= 0) & (dist <= span) & ((blk > 0) | (j >= QBLK))[None] if False else \
        ((dist >= 0) & (dist <= span))[None] & ((blk > 0) | (j >= QBLK))
    s = jnp.where(valid, s, -jnp.inf)
    lse = jax.nn.logsumexp(s, axis=-1)
    p = jnp.exp(s - lse[..., None])
    o = jnp.einsum('bhrnij,bhrnjd->bhrnid', p, vb)
    o = o.reshape(B, H, dil, M, Dh).transpose(0, 1, 3, 2, 4).reshape(B, H, P, Dh)[:, :, :S]
    lse = lse.reshape(B, H, dil, M).transpose(0, 1, 3, 2).reshape(B, H, P)[:, :, :S]
    return o, lse


def dilated_attention(q, k, v, q_g, k_g):
    B, S, _ = q.shape

    def heads(t):
        return t.reshape(B, S, ATTN_HEADS, ATTN_HEAD_DIM).transpose(0, 2, 1, 3)

    qh = rms_norm(heads(q), q_g)
    kh = rms_norm(heads(k), k_g)
    vh = heads(v)
    outs, lses = [], []
    for window, dil in DILATED_CONFIGS:
        o, lse = dilated_branch(qh, kh, vh, window, dil)
        outs.append(o)
        lses.append(lse)
    wts = jax.nn.softmax(jnp.stack(lses), axis=0)
    o = jnp.einsum('gbhs,gbhsd->bhsd', wts, jnp.stack(outs))
    return o.astype(v.dtype).transpose(0, 2, 1, 3).reshape(B, S, ATTN_WIDTH)


def hier_moe(h, w_rg, b_rg, w_re, b_re, w_g, w_u, w_d):
    B, S, D = h.shape
    T = B * S
    xt = h.reshape(T, D)
    logit_g = (xt @ w_rg).astype(jnp.float32) + b_rg.astype(jnp.float32)
    p_g = jax.nn.softmax(logit_g, axis=-1)
    grp = jnp.argmax(logit_g, axis=-1).astype(jnp.int32)
    gate_g = jnp.take_along_axis(p_g, grp[:, None], axis=1)[:, 0]
    logit_e = ((xt @ w_re).astype(jnp.float32) + b_re.astype(jnp.float32)).reshape(T, N_GROUPS, EXPERTS_PER_GROUP)
    logit_e = jnp.take_along_axis(logit_e, grp[:, None, None], axis=1)[:, 0]
    top_v, top_i = lax.top_k(logit_e, TOP_K)
    gates = gate_g[:, None] * jax.nn.softmax(top_v, axis=-1)
    eid = grp[:, None] * EXPERTS_PER_GROUP + top_i.astype(jnp.int32)

    n_assign = T * TOP_K
    flat_e = eid.reshape(-1)
    flat_tok = jnp.repeat(jnp.arange(T, dtype=jnp.int32), TOP_K)
    flat_gate = gates.reshape(-1)
    order = jnp.argsort(flat_e)
    se = flat_e[order]
    counts = jnp.bincount(flat_e, length=N_EXPERTS).astype(jnp.int32)
    starts = jnp.cumsum(counts) - counts
    pcounts = (counts + MOE_BLK - 1) // MOE_BLK * MOE_BLK
    pends = jnp.cumsum(pcounts)
    pstarts = pends - pcounts
    dest = pstarts[se] + (jnp.arange(n_assign, dtype=jnp.int32) - starts[se])
    cap = n_assign + N_EXPERTS * MOE_BLK
    nblk = cap // MOE_BLK
    buf_tok = jnp.full((cap,), T, jnp.int32).at[dest].set(flat_tok[order])
    buf_gate = jnp.zeros((cap,), jnp.float32).at[dest].set(flat_gate[order])
    blk_e = jnp.minimum(jnp.searchsorted(pends, jnp.arange(nblk, dtype=jnp.int32) * MOE_BLK, side='right'),
                        N_EXPERTS - 1).astype(jnp.int32)
    xpad = jnp.concatenate([xt, jnp.zeros((1, D), xt.dtype)], axis=0)

    def expert_block(args):
        tok, e = args
        xb = xpad[tok]
        hb = jax.nn.silu(xb @ w_g[e]) * (xb @ w_u[e])
        return hb @ w_d[e]

    yb = lax.map(expert_block, (buf_tok.reshape(nblk, MOE_BLK), blk_e)).reshape(cap, D)
    y = jnp.zeros((T + 1, D), xt.dtype).at[buf_tok].add(yb * buf_gate[:, None].astype(yb.dtype))
    return y[:T].reshape(B, S, D)


def setup_inputs(seed: int = 0) -> dict:
    key = jax.random.key(seed)
    ks = jax.random.split(key, 24)
    f32 = jnp.float32

    def nrm(k, shape, scale):
        return jax.random.normal(k, shape, f32) * scale

    return {
        "x": nrm(ks[0], (BATCH, SEQ, D_MODEL), 1.0),
        "norm1": 1.0 + nrm(ks[1], (DEPTH, D_MODEL), 0.02),
        "w_in": nrm(ks[2], (DEPTH, D_MODEL, D_IN), D_MODEL ** -0.5),
        "sgu_norm": 1.0 + nrm(ks[3], (DEPTH, SGU_WIDTH), 0.02),
        "sgu_w": nrm(ks[4], (DEPTH, SGU_HEADS, SGU_CHUNK, SGU_CHUNK), SGU_CHUNK ** -0.5),
        "sgu_b": 1.0 + nrm(ks[5], (DEPTH, SGU_HEADS, SGU_CHUNK), 0.02),
        "conv_w": nrm(ks[6], (DEPTH, CONV_K, CONV_CH), CONV_K ** -0.5),
        "conv_b": nrm(ks[7], (DEPTH, CONV_CH), 0.02),
        "conv_ln_g": 1.0 + nrm(ks[8], (DEPTH, CONV_CH), 0.02),
        "conv_ln_b": nrm(ks[9], (DEPTH, CONV_CH), 0.02),
        "q_norm": 1.0 + nrm(ks[10], (DEPTH, ATTN_HEAD_DIM), 0.02),
        "k_norm": 1.0 + nrm(ks[11], (DEPTH, ATTN_HEAD_DIM), 0.02),
        "out_norm": 1.0 + nrm(ks[12], (DEPTH, D_MIX), 0.02),
        "w_out": nrm(ks[13], (DEPTH, D_MIX, D_MODEL), D_MIX ** -0.5),
        "norm2": 1.0 + nrm(ks[14], (DEPTH, D_MODEL), 0.02),
        "w_router_group": nrm(ks[15], (DEPTH, D_MODEL, N_GROUPS), D_MODEL ** -0.5),
        "b_router_group": nrm(ks[16], (DEPTH, N_GROUPS), 0.01),
        "w_router_expert": nrm(ks[17], (DEPTH, D_MODEL, N_EXPERTS), D_MODEL ** -0.5),
        "b_router_expert": nrm(ks[18], (DEPTH, N_EXPERTS), 0.01),
        "w_expert_gate": nrm(ks[19], (DEPTH, N_EXPERTS, D_MODEL, D_EXPERT), D_MODEL ** -0.5),
        "w_expert_up": nrm(ks[20], (DEPTH, N_EXPERTS, D_MODEL, D_EXPERT), D_MODEL ** -0.5),
        "w_expert_down": nrm(ks[21], (DEPTH, N_EXPERTS, D_EXPERT, D_MODEL), D_EXPERT ** -0.5),
    }


def reference(x, norm1, w_in, sgu_norm, sgu_w, sgu_b, conv_w, conv_b, conv_ln_g, conv_ln_b,
              q_norm, k_norm, out_norm, w_out, norm2, w_router_group, b_router_group,
              w_router_expert, b_router_expert, w_expert_gate, w_expert_up, w_expert_down):
    h = x
    a_end = SGU_WIDTH
    b_end = SGU_WIDTH + CONV_CH
    for l in range(DEPTH):
        hn = rms_norm(h, norm1[l])
        proj = hn @ w_in[l]
        u_a, v_a, a_b, g_b, q, k, v = jnp.split(proj, list(IN_SPLITS), axis=-1)
        y_a = spatial_gating(u_a, v_a, sgu_norm[l], sgu_w[l], sgu_b[l])
        y_b = conformer_conv(a_b, g_b, conv_w[l], conv_b[l], conv_ln_g[l], conv_ln_b[l])
        y_c = dilated_attention(q, k, v, q_norm[l], k_norm[l])
        g = out_norm[l]
        mix = jnp.concatenate([rms_norm(y_a, g[:a_end]),
                               rms_norm(y_b, g[a_end:b_end]),
                               rms_norm(y_c, g[b_end:])], axis=-1)
        h = h + mix @ w_out[l]
        h = h + hier_moe(rms_norm(h, norm2[l]), w_router_group[l], b_router_group[l],
                         w_router_expert[l], b_router_expert[l],
                         w_expert_gate[l], w_expert_up[l], w_expert_down[l])
    return h
```

```python
import functools

import jax
import jax.numpy as jnp
from jax import lax
from jax.experimental import pallas as pl
from jax.experimental.pallas import tpu as pltpu

F32 = jnp.float32
BF16 = jnp.bfloat16

D_MODEL = 2048
SGU_WIDTH = 512
SGU_HEADS = 4
SGU_CHUNK = 128
CONV_CH = 512
CONV_K = 31
ATTN_WIDTH = 1024
ATTN_HEADS = 8
HEAD_DIM = 128
DILATIONS = (1, 4, 16)
QBLK = 128
D_IN = 2 * SGU_WIDTH + 2 * CONV_CH + 3 * ATTN_WIDTH
N_GROUPS = 4
EXPERTS_PER_GROUP = 8
N_EXPERTS = 32
TOP_K = 2
D_EXPERT = 512
EPS = 1e-6

LANES = 128
VMEM_LIMIT = 56 * 1024 * 1024
NEG = -1e30

ATTN_TILE = QBLK * max(DILATIONS)
CONV_HALO = 32
MOE_BLK = 256
ROUTE_COLS = N_GROUPS + N_EXPERTS


def _rms(x, g):
    return x * lax.rsqrt(jnp.mean(x * x, axis=-1, keepdims=True) + EPS) * g


def _in_proj_kernel(x_ref, g_ref, w_ref, o_ref, xn_ref):
    @pl.when(pl.program_id(1) == 0)
    def _():
        xn_ref[...] = _rms(x_ref[...], g_ref[...]).astype(BF16)

    o_ref[...] = jnp.dot(xn_ref[...], w_ref[...], preferred_element_type=F32)


def _in_proj(h, g, w_bf16, tm=1024, tn=1024):
    s, d = h.shape
    n = w_bf16.shape[1]
    return pl.pallas_call(
        _in_proj_kernel,
        grid=(s // tm, n // tn),
        in_specs=[
            pl.BlockSpec((tm, d), lambda i, j: (i, 0)),
            pl.BlockSpec((1, d), lambda i, j: (0, 0)),
            pl.BlockSpec((d, tn), lambda i, j: (0, j)),
        ],
        out_specs=pl.BlockSpec((tm, tn), lambda i, j: (i, j)),
        out_shape=jax.ShapeDtypeStruct((s, n), F32),
        scratch_shapes=[pltpu.VMEM((tm, d), BF16)],
        compiler_params=pltpu.CompilerParams(
            dimension_semantics=("parallel", "arbitrary"), vmem_limit_bytes=VMEM_LIMIT),
        name="in_proj",
    )(h, g.reshape(1, d), w_bf16)


def _mix_ab_kernel(u_ref, v_ref, a_ref, gt_ref, ah_ref, gh_ref,
                   sgn_ref, sw_ref, sb_ref, cw_ref, cb_ref, lng_ref, lnb_ref,
                   ona_ref, onb_ref, o_ref, zext_ref, *, tq):
    u = jax.nn.gelu(u_ref[...])
    v = _rms(jax.nn.gelu(v_ref[...]), sgn_ref[...]).astype(BF16)
    row = lax.broadcasted_iota(jnp.int32, (SGU_CHUNK, SGU_CHUNK), 0)
    col = lax.broadcasted_iota(jnp.int32, (SGU_CHUNK, SGU_CHUNK), 1)
    causal = col <= row
    wm = [jnp.where(causal, sw_ref[hh], 0.0).astype(BF16) for hh in range(SGU_HEADS)]
    for c in range(tq // SGU_CHUNK):
        rows = slice(c * SGU_CHUNK, (c + 1) * SGU_CHUNK)
        zs = []
        for hh in range(SGU_HEADS):
            cols = slice(hh * HEAD_DIM, (hh + 1) * HEAD_DIM)
            zs.append(jnp.dot(wm[hh], v[rows, cols], preferred_element_type=F32) + sb_ref[hh])
        ya = u[rows, :] * jnp.concatenate(zs, axis=1)
        o_ref[rows, 0:SGU_WIDTH] = _rms(ya, ona_ref[...]).astype(o_ref.dtype)

    first = pl.program_id(0) == 0
    zh = ah_ref[...] * jax.nn.sigmoid(gh_ref[...])
    zext_ref[0:CONV_HALO, :] = jnp.where(first, 0.0, zh)
    zext_ref[CONV_HALO:, :] = a_ref[...] * jax.nn.sigmoid(gt_ref[...])
    rc = 32
    off0 = CONV_HALO - (CONV_K - 1)
    for c in range(tq // rc):
        acc = jnp.zeros((rc, CONV_CH), F32) + cb_ref[...]
        for k in range(CONV_K):
            acc = acc + zext_ref[pl.ds(c * rc + off0 + k, rc), :] * cw_ref[k:k + 1, :]
        mu = jnp.mean(acc, axis=-1, keepdims=True)
        xc = acc - mu
        y = xc * lax.rsqrt(jnp.mean(xc * xc, axis=-1, keepdims=True) + EPS)
        y = jax.nn.silu(y * lng_ref[...] + lnb_ref[...])
        o_ref[c * rc:(c + 1) * rc, SGU_WIDTH:] = _rms(y, onb_ref[...]).astype(o_ref.dtype)


def _mix_ab(proj, sgu_norm, sgu_w, sgu_b, conv_w, conv_b, ln_g, ln_b, on_a, on_b, tq=512):
    s = proj.shape[0]
    w = SGU_WIDTH
    hb = tq // CONV_HALO
    sb = jnp.broadcast_to(sgu_b[:, :, None], (SGU_HEADS, SGU_CHUNK, HEAD_DIM))
    cw = jnp.pad(conv_w, ((0, 32 - CONV_K), (0, 0)))
    vec = lambda a: a.reshape(1, -1)
    const2 = lambda i: (0, 0)
    const3 = lambda i: (0, 0, 0)
    return pl.pallas_call(
        functools.partial(_mix_ab_kernel, tq=tq),
        grid=(s // tq,),
        in_specs=[
            pl.BlockSpec((tq, w), lambda i: (i, 0)),
            pl.BlockSpec((tq, w), lambda i: (i, 1)),
            pl.BlockSpec((tq, w), lambda i: (i, 2)),
            pl.BlockSpec((tq, w), lambda i: (i, 3)),
            pl.BlockSpec((CONV_HALO, w), lambda i: (jnp.maximum(i * hb - 1, 0), 2)),
            pl.BlockSpec((CONV_HALO, w), lambda i: (jnp.maximum(i * hb - 1, 0), 3)),
            pl.BlockSpec((1, w), const2),
            pl.BlockSpec((SGU_HEADS, SGU_CHUNK, SGU_CHUNK), const3),
            pl.BlockSpec((SGU_HEADS, SGU_CHUNK, HEAD_DIM), const3),
            pl.BlockSpec((32, w), const2),
            pl.BlockSpec((1, w), const2),
            pl.BlockSpec((1, w), const2),
            pl.BlockSpec((1, w), const2),
            pl.BlockSpec((1, w), const2),
            pl.BlockSpec((1, w), const2),
        ],
        out_specs=pl.BlockSpec((tq, 2 * w), lambda i: (i, 0)),
        out_shape=jax.ShapeDtypeStruct((s, 2 * w), BF16),
        scratch_shapes=[pltpu.VMEM((tq + CONV_HALO, w), F32)],
        compiler_params=pltpu.CompilerParams(
            dimension_semantics=("parallel",), vmem_limit_bytes=VMEM_LIMIT),
        name="mix_ab",
    )(proj, proj, proj, proj, proj, proj, vec(sgu_norm), sgu_w, sb, cw, vec(conv_b),
      vec(ln_g), vec(ln_b), vec(on_a), vec(on_b))


def _attn_kernel(q_ref, k_ref, v_ref, qg_ref, kg_ref, o_ref,
                 qn_ref, kn_ref, vv_ref, ob_ref, lb_ref):
    t = pl.program_id(1)
    tl = ATTN_TILE

    @pl.when(t == 0)
    def _():
        kn_ref[0:tl, :] = jnp.zeros((tl, HEAD_DIM), F32)
        vv_ref[0:tl, :] = jnp.zeros((tl, HEAD_DIM), F32)

    @pl.when(t > 0)
    def _():
        kn_ref[0:tl, :] = kn_ref[tl:, :]
        vv_ref[0:tl, :] = vv_ref[tl:, :]

    kn_ref[tl:, :] = _rms(k_ref[...], kg_ref[...])
    vv_ref[tl:, :] = v_ref[...]
    qn_ref[...] = _rms(q_ref[...], qg_ref[...]) * (HEAD_DIM ** -0.5)

    qi = lax.broadcasted_iota(jnp.int32, (QBLK, 2 * QBLK), 0)
    kj = lax.broadcasted_iota(jnp.int32, (QBLK, 2 * QBLK), 1)
    band = (kj >= qi) & (kj <= qi + QBLK)
    band_first = band & (kj >= jnp.where(t > 0, 0, QBLK))

    for g, dil in enumerate(DILATIONS):
        for r in range(dil):
            for b in range(tl // (QBLK * dil)):
                base = b * QBLK * dil + r
                qb = qn_ref[pl.ds(base, QBLK, stride=dil), :].astype(BF16)
                kb = kn_ref[pl.ds(tl + base - QBLK * dil, 2 * QBLK, stride=dil), :].astype(BF16)
                vb = vv_ref[pl.ds(tl + base - QBLK * dil, 2 * QBLK, stride=dil), :].astype(BF16)
                sc = lax.dot_general(qb, kb, (((1,), (1,)), ((), ())),
                                     preferred_element_type=F32)
                sc = jnp.where(band_first if b == 0 else band, sc, NEG)
                m = jnp.max(sc, axis=-1, keepdims=True)
                p = jnp.exp(sc - m)
                l = jnp.sum(p, axis=-1, keepdims=True)
                ob = jnp.dot(p.astype(BF16), vb, preferred_element_type=F32) / l
                ob_ref[g, pl.ds(base, QBLK, stride=dil), :] = ob
                lb_ref[g, pl.ds(base, QBLK, stride=dil), :] = jnp.broadcast_to(
                    m + jnp.log(l), (QBLK, HEAD_DIM))

    l0, l1, l2 = lb_ref[0], lb_ref[1], lb_ref[2]
    mx = jnp.maximum(jnp.maximum(l0, l1), l2)
    w0, w1, w2 = jnp.exp(l0 - mx), jnp.exp(l1 - mx), jnp.exp(l2 - mx)
    y = (w0 * ob_ref[0] + w1 * ob_ref[1] + w2 * ob_ref[2]) / (w0 + w1 + w2)
    o_ref[...] = y.astype(o_ref.dtype)


def _attention(proj, q_g, k_g):
    s = proj.shape[0]
    tl = ATTN_TILE
    qc = (2 * SGU_WIDTH + 2 * CONV_CH) // HEAD_DIM
    kc = qc + ATTN_HEADS
    vc = kc + ATTN_HEADS
    return pl.pallas_call(
        _attn_kernel,
        grid=(ATTN_HEADS, s // tl),
        in_specs=[
            pl.BlockSpec((tl, HEAD_DIM), lambda h, t: (t, qc + h)),
            pl.BlockSpec((tl, HEAD_DIM), lambda h, t: (t, kc + h)),
            pl.BlockSpec((tl, HEAD_DIM), lambda h, t: (t, vc + h)),
            pl.BlockSpec((1, HEAD_DIM), lambda h, t: (0, 0)),
            pl.BlockSpec((1, HEAD_DIM), lambda h, t: (0, 0)),
        ],
        out_specs=pl.BlockSpec((tl, HEAD_DIM), lambda h, t: (t, h)),
        out_shape=jax.ShapeDtypeStruct((s, ATTN_WIDTH), BF16),
        scratch_shapes=[
            pltpu.VMEM((tl, HEAD_DIM), F32),
            pltpu.VMEM((2 * tl, HEAD_DIM), F32),
            pltpu.VMEM((2 * tl, HEAD_DIM), F32),
            pltpu.VMEM((len(DILATIONS), tl, HEAD_DIM), F32),
            pltpu.VMEM((len(DILATIONS), tl, HEAD_DIM), F32),
        ],
        compiler_params=pltpu.CompilerParams(
            dimension_semantics=("parallel", "arbitrary"), vmem_limit_bytes=VMEM_LIMIT),
        name="attention",
    )(proj, proj, proj, q_g.reshape(1, HEAD_DIM), k_g.reshape(1, HEAD_DIM))


def _split_bf16(x):
    hi = x.astype(BF16)
    lo = (x - hi.astype(F32)).astype(BF16)
    return hi, lo


def _out_proj_kernel(yab_ref, yc_ref, onc_ref, h_ref, w_ref, n2_ref, rwh_ref, rwl_ref, rb_ref,
                     h1_ref, xn_ref, route_ref):
    half = yab_ref.shape[1]
    ycn = _rms(yc_ref[...].astype(F32), onc_ref[...]).astype(BF16)
    acc = jnp.dot(yab_ref[...], w_ref[0:half, :], preferred_element_type=F32)
    acc = acc + jnp.dot(ycn, w_ref[half:, :], preferred_element_type=F32)
    h1 = h_ref[...] + acc
    h1_ref[...] = h1
    xn = _rms(h1, n2_ref[...])
    xn_ref[...] = xn

    xh, xl = _split_bf16(xn)
    lg = (jnp.dot(xh, rwh_ref[...], preferred_element_type=F32)
          + jnp.dot(xl, rwh_ref[...], preferred_element_type=F32)
          + jnp.dot(xh, rwl_ref[...], preferred_element_type=F32)) + rb_ref[...]

    lane = lax.broadcasted_iota(jnp.int32, lg.shape, 1)
    big = jnp.int32(LANES)
    gl = jnp.where(lane < N_GROUPS, lg, NEG)
    gmax = jnp.max(gl, axis=-1, keepdims=True)
    grp = jnp.min(jnp.where(gl == gmax, lane, big), axis=-1, keepdims=True)
    gate_g = 1.0 / jnp.sum(jnp.exp(gl - gmax), axis=-1, keepdims=True)
    lo = N_GROUPS + grp * EXPERTS_PER_GROUP
    el = jnp.where((lane >= lo) & (lane < lo + EXPERTS_PER_GROUP), lg, NEG)
    v1 = jnp.max(el, axis=-1, keepdims=True)
    i1 = jnp.min(jnp.where(el == v1, lane, big), axis=-1, keepdims=True)
    el2 = jnp.where(lane == i1, NEG, el)
    v2 = jnp.max(el2, axis=-1, keepdims=True)
    i2 = jnp.min(jnp.where(el2 == v2, lane, big), axis=-1, keepdims=True)
    e2 = jnp.exp(v2 - v1)
    g0 = gate_g / (1.0 + e2)
    g1 = gate_g * e2 / (1.0 + e2)
    route = jnp.where(lane == 0, (i1 - N_GROUPS).astype(F32),
                      jnp.where(lane == 1, (i2 - N_GROUPS).astype(F32),
                                jnp.where(lane == 2, g0, jnp.where(lane == 3, g1, 0.0))))
    route_ref[...] = route


def _out_proj(y_ab, y_c, on_c, h, w_bf16, norm2, rw, rb, tm=512):
    s, d = h.shape
    half = y_ab.shape[1]
    rw_pad = jnp.pad(rw, ((0, 0), (0, LANES - ROUTE_COLS)))
    rwh, rwl = _split_bf16(rw_pad)
    rb_pad = jnp.pad(rb, (0, LANES - ROUTE_COLS)).reshape(1, LANES)
    const = lambda i: (0, 0)
    return pl.pallas_call(
        _out_proj_kernel,
        grid=(s // tm,),
        in_specs=[
            pl.BlockSpec((tm, half), lambda i: (i, 0)),
            pl.BlockSpec((tm, half), lambda i: (i, 0)),
            pl.BlockSpec((1, half), const),
            pl.BlockSpec((tm, d), lambda i: (i, 0)),
            pl.BlockSpec((d, d), const, pipeline_mode=pl.Buffered(1)),
            pl.BlockSpec((1, d), const),
            pl.BlockSpec((d, LANES), const),
            pl.BlockSpec((d, LANES), const),
            pl.BlockSpec((1, LANES), const),
        ],
        out_specs=[
            pl.BlockSpec((tm, d), lambda i: (i, 0)),
            pl.BlockSpec((tm, d), lambda i: (i, 0)),
            pl.BlockSpec((tm, LANES), lambda i: (i, 0)),
        ],
        out_shape=[
            jax.ShapeDtypeStruct((s, d), F32),
            jax.ShapeDtypeStruct((s, d), F32),
            jax.ShapeDtypeStruct((s, LANES), F32),
        ],
        compiler_params=pltpu.CompilerParams(
            dimension_semantics=("parallel",), vmem_limit_bytes=VMEM_LIMIT),
        name="out_proj",
    )(y_ab, y_c, on_c.reshape(1, half), h, w_bf16, norm2.reshape(1, d), rwh, rwl, rb_pad)


def _route_plan(route, t):
    n_assign = t * TOP_K
    cap = n_assign + N_EXPERTS * MOE_BLK
    nblk = cap // MOE_BLK
    flat_e = route[:, 0:TOP_K].astype(jnp.int32).reshape(-1)
    onehot = (flat_e[:, None] == jnp.arange(N_EXPERTS, dtype=jnp.int32)[None, :]).astype(jnp.int32)
    csum = jnp.cumsum(onehot, axis=0)
    pos = jnp.sum((csum - 1) * onehot, axis=1)
    counts = csum[-1]
    pcounts = (counts + MOE_BLK - 1) // MOE_BLK * MOE_BLK
    pends = jnp.cumsum(pcounts)
    pstarts = pends - pcounts
    dest = jnp.sum(onehot * pstarts[None, :], axis=1) + pos
    slot_a = jnp.full((cap,), -1, jnp.int32).at[dest].set(jnp.arange(n_assign, dtype=jnp.int32))
    blk_n = jnp.sum((slot_a >= 0).reshape(nblk, MOE_BLK), axis=1).astype(jnp.int32)
    blk_e = jnp.minimum(
        jnp.searchsorted(pends, jnp.arange(nblk, dtype=jnp.int32) * MOE_BLK, side='right'),
        N_EXPERTS - 1).astype(jnp.int32)
    n_act = (pends[-1] // MOE_BLK).astype(jnp.int32).reshape(1)
    return blk_e, blk_n, n_act, slot_a, nblk


def _experts_kernel(blk_e_ref, blk_n_ref, n_act_ref, slot_a_ref, x_hbm, wg_ref, wu_ref, wd_ref,
                    y_hbm, xbuf, ybuf, gsem, ssem):
    i = pl.program_id(0)
    n_act = n_act_ref[0]

    def gather_row(blk, rr):
        tok = slot_a_ref[blk * MOE_BLK + rr] >> 1
        return pltpu.make_async_copy(x_hbm.at[pl.ds(tok, 1)], xbuf.at[blk % 2, pl.ds(rr, 1)],
                                     gsem.at[blk % 2])

    def scatter_row(blk, rr):
        dst = slot_a_ref[blk * MOE_BLK + rr]
        return pltpu.make_async_copy(ybuf.at[blk % 2, pl.ds(rr, 1)], y_hbm.at[pl.ds(dst, 1)],
                                     ssem.at[blk % 2])

    def for_valid_rows(blk, full_block, one_row):
        nv = blk_n_ref[blk]

        @pl.when(nv == MOE_BLK)
        def _():
            full_block()

        @pl.when(nv < MOE_BLK)
        def _():
            def body(rr, carry):
                one_row(rr)
                return carry
            lax.fori_loop(0, nv, body, 0)

    def start_rows(row_copy, blk):
        def full_block():
            for rr in range(MOE_BLK):
                row_copy(blk, rr).start()
        for_valid_rows(blk, full_block, lambda rr: row_copy(blk, rr).start())

    def wait_gather(blk):
        def full_block():
            pltpu.make_async_copy(x_hbm.at[pl.ds(0, MOE_BLK)], xbuf.at[blk % 2],
                                  gsem.at[blk % 2]).wait()
        for_valid_rows(blk, full_block, lambda rr: gather_row(blk, rr).wait())

    def wait_scatter(blk):
        def full_block():
            pltpu.make_async_copy(ybuf.at[blk % 2], y_hbm.at[pl.ds(0, MOE_BLK)],
                                  ssem.at[blk % 2]).wait()
        for_valid_rows(blk, full_block, lambda rr: scatter_row(blk, rr).wait())

    @pl.when(i == 0)
    def _():
        xbuf[...] = jnp.zeros(xbuf.shape, xbuf.dtype)

    @pl.when(i < n_act)
    def _():
        start_rows(gather_row, i)

    @pl.when((i >= 1) & (i <= n_act))
    def _():
        j = i - 1
        wait_gather(j)
        x = xbuf[j % 2].astype(BF16)
        hg = jnp.dot(x, wg_ref[0], preferred_element_type=F32)
        hu = jnp.dot(x, wu_ref[0], preferred_element_type=F32)
        hb = (jax.nn.silu(hg) * hu).astype(BF16)
        y = jnp.dot(hb, wd_ref[0], preferred_element_type=F32)

        @pl.when(j >= 2)
        def _():
            wait_scatter(j - 2)

        ybuf[j % 2] = y
        start_rows(scatter_row, j)

    @pl.when(i == n_act)
    def _():
        wait_scatter(i - 1)

        @pl.when(i >= 2)
        def _():
            wait_scatter(i - 2)


def _experts(xn, blk_e, blk_n, n_act, slot_a, nblk, wg, wu, wd):
    t, d = xn.shape
    de = wg.shape[2]
    wmap = lambda i, be, bn, na, sa: (be[jnp.maximum(i - 1, 0)], 0, 0)
    return pl.pallas_call(
        _experts_kernel,
        grid_spec=pltpu.PrefetchScalarGridSpec(
            num_scalar_prefetch=4,
            grid=(nblk + 1,),
            in_specs=[
                pl.BlockSpec(memory_space=pl.ANY),
                pl.BlockSpec((1, d, de), wmap),
                pl.BlockSpec((1, d, de), wmap),
                pl.BlockSpec((1, de, d), wmap),
            ],
            out_specs=pl.BlockSpec(memory_space=pl.ANY),
            scratch_shapes=[
                pltpu.VMEM((2, MOE_BLK, d), F32),
                pltpu.VMEM((2, MOE_BLK, d), F32),
                pltpu.SemaphoreType.DMA((2,)),
                pltpu.SemaphoreType.DMA((2,)),
            ],
        ),
        out_shape=jax.ShapeDtypeStruct((t * TOP_K, d), F32),
        compiler_params=pltpu.CompilerParams(
            dimension_semantics=("arbitrary",), vmem_limit_bytes=VMEM_LIMIT),
        name="experts",
    )(blk_e, blk_n, n_act, slot_a, xn, wg, wu, wd)


def _combine_kernel(h_ref, ya_ref, yb_ref, route_ref, o_ref):
    r = route_ref[...]
    lane = lax.broadcasted_iota(jnp.int32, r.shape, 1)
    g0 = jnp.sum(jnp.where(lane == 2, r, 0.0), axis=-1, keepdims=True)
    g1 = jnp.sum(jnp.where(lane == 3, r, 0.0), axis=-1, keepdims=True)
    o_ref[...] = h_ref[...] + (ya_ref[...] * g0 + yb_ref[...] * g1)


def _combine(h1, y2, route, tm=512):
    t, d = h1.shape
    y2v = y2.reshape(y2.shape[0] // TOP_K, TOP_K * d)
    return pl.pallas_call(
        _combine_kernel,
        grid=(t // tm,),
        in_specs=[
            pl.BlockSpec((tm, d), lambda i: (i, 0)),
            pl.BlockSpec((tm, d), lambda i: (i, 0)),
            pl.BlockSpec((tm, d), lambda i: (i, 1)),
            pl.BlockSpec((tm, LANES), lambda i: (i, 0)),
        ],
        out_specs=pl.BlockSpec((tm, d), lambda i: (i, 0)),
        out_shape=jax.ShapeDtypeStruct((t, d), F32),
        compiler_params=pltpu.CompilerParams(
            dimension_semantics=("parallel",), vmem_limit_bytes=VMEM_LIMIT),
        name="combine",
    )(h1, y2v, y2v, route)


def kernel(x, norm1, w_in, sgu_norm, sgu_w, sgu_b, conv_w, conv_b, conv_ln_g, conv_ln_b,
           q_norm, k_norm, out_norm, w_out, norm2, w_router_group, b_router_group,
           w_router_expert, b_router_expert, w_expert_gate, w_expert_up, w_expert_down):
    b, s, d = x.shape
    depth = norm1.shape[0]
    t = b * s
    h = x.reshape(t, d)
    a_end, b_end = SGU_WIDTH, SGU_WIDTH + CONV_CH
    for l in range(depth):
        proj = _in_proj(h, norm1[l], w_in[l].astype(BF16))
        y_ab = _mix_ab(proj, sgu_norm[l], sgu_w[l], sgu_b[l], conv_w[l], conv_b[l],
                       conv_ln_g[l], conv_ln_b[l], out_norm[l, :a_end], out_norm[l, a_end:b_end])
        y_c = _attention(proj, q_norm[l], k_norm[l])
        rw = jnp.concatenate([w_router_group[l], w_router_expert[l]], axis=1)
        rb = jnp.concatenate([b_router_group[l], b_router_expert[l]], axis=0)
        h1, xn, route = _out_proj(y_ab, y_c, out_norm[l, b_end:], h, w_out[l].astype(BF16),
                                  norm2[l], rw, rb)
        blk_e, blk_n, n_act, slot_a, nblk = _route_plan(route, t)
        y2 = _experts(xn, blk_e, blk_n, n_act, slot_a, nblk, w_expert_gate[l].astype(BF16),
                      w_expert_up[l].astype(BF16), w_expert_down[l].astype(BF16))
        h = _combine(h1, y2, route)
    return h.reshape(b, s, d)
```

```python
import functools

import jax
import jax.numpy as jnp
from jax import lax
from jax.experimental import pallas as pl
from jax.experimental.pallas import tpu as pltpu

F32 = jnp.float32
BF16 = jnp.bfloat16
U32 = jnp.uint32

D_MODEL = 2048
SGU_WIDTH = 512
SGU_HEADS = 4
SGU_CHUNK = 128
CONV_CH = 512
CONV_K = 31
ATTN_WIDTH = 1024
ATTN_HEADS = 8
HEAD_DIM = 128
DILATIONS = (1, 4, 16)
QBLK = 128
D_IN = 2 * SGU_WIDTH + 2 * CONV_CH + 3 * ATTN_WIDTH
N_GROUPS = 4
EXPERTS_PER_GROUP = 8
N_EXPERTS = 32
TOP_K = 2
D_EXPERT = 512
EPS = 1e-6

LANES = 128
VMEM_LIMIT = 56 * 1024 * 1024
NEG = -1e30

ATTN_TILE = QBLK * max(DILATIONS)
CONV_HALO = 32
MOE_BLK = 256
ROUTE_COLS = N_GROUPS + N_EXPERTS
ROUTE_LANES = 6


def _rms(x, g):
    return x * lax.rsqrt(jnp.mean(x * x, axis=-1, keepdims=True) + EPS) * g


def _in_proj_kernel(x_ref, g_ref, w_ref, o_ref, xn_ref):
    @pl.when(pl.program_id(1) == 0)
    def _():
        xn_ref[...] = _rms(x_ref[...], g_ref[...]).astype(BF16)

    o_ref[...] = jnp.dot(xn_ref[...], w_ref[...], preferred_element_type=F32)


def _in_proj(h, g, w_bf16, tm=1024, tn=1024):
    s, d = h.shape
    n = w_bf16.shape[1]
    return pl.pallas_call(
        _in_proj_kernel,
        grid=(s // tm, n // tn),
        in_specs=[
            pl.BlockSpec((tm, d), lambda i, j: (i, 0)),
            pl.BlockSpec((1, d), lambda i, j: (0, 0)),
            pl.BlockSpec((d, tn), lambda i, j: (0, j)),
        ],
        out_specs=pl.BlockSpec((tm, tn), lambda i, j: (i, j)),
        out_shape=jax.ShapeDtypeStruct((s, n), F32),
        scratch_shapes=[pltpu.VMEM((tm, d), BF16)],
        compiler_params=pltpu.CompilerParams(
            dimension_semantics=("parallel", "arbitrary"), vmem_limit_bytes=VMEM_LIMIT),
        name="in_proj",
    )(h, g.reshape(1, d), w_bf16)


def _mix_ab_kernel(u_ref, v_ref, a_ref, gt_ref, ah_ref, gh_ref,
                   sgn_ref, sw_ref, sb_ref, cw_ref, cb_ref, lng_ref, lnb_ref,
                   ona_ref, onb_ref, o_ref, zext_ref, *, tq):
    u = jax.nn.gelu(u_ref[...])
    v = _rms(jax.nn.gelu(v_ref[...]), sgn_ref[...]).astype(BF16)
    row = lax.broadcasted_iota(jnp.int32, (SGU_CHUNK, SGU_CHUNK), 0)
    col = lax.broadcasted_iota(jnp.int32, (SGU_CHUNK, SGU_CHUNK), 1)
    causal = col <= row
    wm = [jnp.where(causal, sw_ref[hh], 0.0).astype(BF16) for hh in range(SGU_HEADS)]
    for c in range(tq // SGU_CHUNK):
        rows = slice(c * SGU_CHUNK, (c + 1) * SGU_CHUNK)
        zs = []
        for hh in range(SGU_HEADS):
            cols = slice(hh * HEAD_DIM, (hh + 1) * HEAD_DIM)
            zs.append(jnp.dot(wm[hh], v[rows, cols], preferred_element_type=F32) + sb_ref[hh])
        ya = u[rows, :] * jnp.concatenate(zs, axis=1)
        o_ref[rows, 0:SGU_WIDTH] = _rms(ya, ona_ref[...]).astype(o_ref.dtype)

    first = pl.program_id(0) == 0
    zh = ah_ref[...] * jax.nn.sigmoid(gh_ref[...])
    zext_ref[0:CONV_HALO, :] = jnp.where(first, 0.0, zh)
    zext_ref[CONV_HALO:, :] = a_ref[...] * jax.nn.sigmoid(gt_ref[...])
    rc = 32
    off0 = CONV_HALO - (CONV_K - 1)
    for c in range(tq // rc):
        acc = jnp.zeros((rc, CONV_CH), F32) + cb_ref[...]
        for k in range(CONV_K):
            acc = acc + zext_ref[pl.ds(c * rc + off0 + k, rc), :] * cw_ref[k:k + 1, :]
        mu = jnp.mean(acc, axis=-1, keepdims=True)
        xc = acc - mu
        y = xc * lax.rsqrt(jnp.mean(xc * xc, axis=-1, keepdims=True) + EPS)
        y = jax.nn.silu(y * lng_ref[...] + lnb_ref[...])
        o_ref[c * rc:(c + 1) * rc, SGU_WIDTH:] = _rms(y, onb_ref[...]).astype(o_ref.dtype)


def _mix_ab(proj, sgu_norm, sgu_w, sgu_b, conv_w, conv_b, ln_g, ln_b, on_a, on_b, tq=512):
    s = proj.shape[0]
    w = SGU_WIDTH
    hb = tq // CONV_HALO
    sb = jnp.broadcast_to(sgu_b[:, :, None], (SGU_HEADS, SGU_CHUNK, HEAD_DIM))
    cw = jnp.pad(conv_w, ((0, 32 - CONV_K), (0, 0)))
    vec = lambda a: a.reshape(1, -1)
    const2 = lambda i: (0, 0)
    const3 = lambda i: (0, 0, 0)
    return pl.pallas_call(
        functools.partial(_mix_ab_kernel, tq=tq),
        grid=(s // tq,),
        in_specs=[
            pl.BlockSpec((tq, w), lambda i: (i, 0)),
            pl.BlockSpec((tq, w), lambda i: (i, 1)),
            pl.BlockSpec((tq, w), lambda i: (i, 2)),
            pl.BlockSpec((tq, w), lambda i: (i, 3)),
            pl.BlockSpec((CONV_HALO, w), lambda i: (jnp.maximum(i * hb - 1, 0), 2)),
            pl.BlockSpec((CONV_HALO, w), lambda i: (jnp.maximum(i * hb - 1, 0), 3)),
            pl.BlockSpec((1, w), const2),
            pl.BlockSpec((SGU_HEADS, SGU_CHUNK, SGU_CHUNK), const3),
            pl.BlockSpec((SGU_HEADS, SGU_CHUNK, HEAD_DIM), const3),
            pl.BlockSpec((32, w), const2),
            pl.BlockSpec((1, w), const2),
            pl.BlockSpec((1, w), const2),
            pl.BlockSpec((1, w), const2),
            pl.BlockSpec((1, w), const2),
            pl.BlockSpec((1, w), const2),
        ],
        out_specs=pl.BlockSpec((tq, 2 * w), lambda i: (i, 0)),
        out_shape=jax.ShapeDtypeStruct((s, 2 * w), BF16),
        scratch_shapes=[pltpu.VMEM((tq + CONV_HALO, w), F32)],
        compiler_params=pltpu.CompilerParams(
            dimension_semantics=("parallel",), vmem_limit_bytes=VMEM_LIMIT),
        name="mix_ab",
    )(proj, proj, proj, proj, proj, proj, vec(sgu_norm), sgu_w, sb, cw, vec(conv_b),
      vec(ln_g), vec(ln_b), vec(on_a), vec(on_b))


def _attn_kernel(q_ref, k_ref, v_ref, qg_ref, kg_ref, o_ref,
                 qn_ref, kn_ref, vv_ref, ob_ref, lb_ref):
    t = pl.program_id(1)
    tl = ATTN_TILE

    @pl.when(t == 0)
    def _():
        kn_ref[0:tl, :] = jnp.zeros((tl, HEAD_DIM), F32)
        vv_ref[0:tl, :] = jnp.zeros((tl, HEAD_DIM), F32)

    @pl.when(t > 0)
    def _():
        kn_ref[0:tl, :] = kn_ref[tl:, :]
        vv_ref[0:tl, :] = vv_ref[tl:, :]

    kn_ref[tl:, :] = _rms(k_ref[...], kg_ref[...])
    vv_ref[tl:, :] = v_ref[...]
    qn_ref[...] = _rms(q_ref[...], qg_ref[...]) * (HEAD_DIM ** -0.5)

    qi = lax.broadcasted_iota(jnp.int32, (QBLK, 2 * QBLK), 0)
    kj = lax.broadcasted_iota(jnp.int32, (QBLK, 2 * QBLK), 1)
    band = (kj >= qi) & (kj <= qi + QBLK)
    band_first = band & (kj >= jnp.where(t > 0, 0, QBLK))

    for g, dil in enumerate(DILATIONS):
        for r in range(dil):
            for b in range(tl // (QBLK * dil)):
                base = b * QBLK * dil + r
                qb = qn_ref[pl.ds(base, QBLK, stride=dil), :].astype(BF16)
                kb = kn_ref[pl.ds(tl + base - QBLK * dil, 2 * QBLK, stride=dil), :].astype(BF16)
                vb = vv_ref[pl.ds(tl + base - QBLK * dil, 2 * QBLK, stride=dil), :].astype(BF16)
                sc = lax.dot_general(qb, kb, (((1,), (1,)), ((), ())),
                                     preferred_element_type=F32)
                sc = jnp.where(band_first if b == 0 else band, sc, NEG)
                m = jnp.max(sc, axis=-1, keepdims=True)
                p = jnp.exp(sc - m)
                l = jnp.sum(p, axis=-1, keepdims=True)
                ob = jnp.dot(p.astype(BF16), vb, preferred_element_type=F32) / l
                ob_ref[g, pl.ds(base, QBLK, stride=dil), :] = ob
                lb_ref[g, pl.ds(base, QBLK, stride=dil), :] = jnp.broadcast_to(
                    m + jnp.log(l), (QBLK, HEAD_DIM))

    l0, l1, l2 = lb_ref[0], lb_ref[1], lb_ref[2]
    mx = jnp.maximum(jnp.maximum(l0, l1), l2)
    w0, w1, w2 = jnp.exp(l0 - mx), jnp.exp(l1 - mx), jnp.exp(l2 - mx)
    y = (w0 * ob_ref[0] + w1 * ob_ref[1] + w2 * ob_ref[2]) / (w0 + w1 + w2)
    o_ref[...] = y.astype(o_ref.dtype)


def _attention(proj, q_g, k_g):
    s = proj.shape[0]
    tl = ATTN_TILE
    qc = (2 * SGU_WIDTH + 2 * CONV_CH) // HEAD_DIM
    kc = qc + ATTN_HEADS
    vc = kc + ATTN_HEADS
    return pl.pallas_call(
        _attn_kernel,
        grid=(ATTN_HEADS, s // tl),
        in_specs=[
            pl.BlockSpec((tl, HEAD_DIM), lambda h, t: (t, qc + h)),
            pl.BlockSpec((tl, HEAD_DIM), lambda h, t: (t, kc + h)),
            pl.BlockSpec((tl, HEAD_DIM), lambda h, t: (t, vc + h)),
            pl.BlockSpec((1, HEAD_DIM), lambda h, t: (0, 0)),
            pl.BlockSpec((1, HEAD_DIM), lambda h, t: (0, 0)),
        ],
        out_specs=pl.BlockSpec((tl, HEAD_DIM), lambda h, t: (t, h)),
        out_shape=jax.ShapeDtypeStruct((s, ATTN_WIDTH), BF16),
        scratch_shapes=[
            pltpu.VMEM((tl, HEAD_DIM), F32),
            pltpu.VMEM((2 * tl, HEAD_DIM), F32),
            pltpu.VMEM((2 * tl, HEAD_DIM), F32),
            pltpu.VMEM((len(DILATIONS), tl, HEAD_DIM), F32),
            pltpu.VMEM((len(DILATIONS), tl, HEAD_DIM), F32),
        ],
        compiler_params=pltpu.CompilerParams(
            dimension_semantics=("parallel", "arbitrary"), vmem_limit_bytes=VMEM_LIMIT),
        name="attention",
    )(proj, proj, proj, q_g.reshape(1, HEAD_DIM), k_g.reshape(1, HEAD_DIM))


def _split_bf16(x):
    hi = x.astype(BF16)
    lo = (x - hi.astype(F32)).astype(BF16)
    return hi, lo


def _pack_halves(x):
    c = x.shape[1] // 2
    lo = lax.bitcast_convert_type(x[:, :c].astype(BF16).astype(F32), U32)
    hi = lax.bitcast_convert_type(x[:, c:].astype(BF16).astype(F32), U32)
    return (lo >> 16) | hi


def _unpack_halves(u):
    lo = lax.bitcast_convert_type(u << 16, F32)
    hi = lax.bitcast_convert_type(u & jnp.uint32(0xFFFF0000), F32)
    return lo, hi


def _out_proj_kernel(yab_ref, yc_ref, onc_ref, h_ref, w_ref, n2_ref, rwh_ref, rwl_ref, rb_ref,
                     h1_ref, xp_ref, route_ref, cnt_ref, tri_ref, run_ref):
    half = yab_ref.shape[1]
    tm = h_ref.shape[0]

    @pl.when(pl.program_id(0) == 0)
    def _():
        r = lax.broadcasted_iota(jnp.int32, (tm, tm), 0)
        c = lax.broadcasted_iota(jnp.int32, (tm, tm), 1)
        tri_ref[...] = jnp.where(c < r, 1.0, 0.0).astype(BF16)
        run_ref[...] = jnp.zeros(run_ref.shape, F32)

    ycn = _rms(yc_ref[...].astype(F32), onc_ref[...]).astype(BF16)
    acc = jnp.dot(yab_ref[...], w_ref[0:half, :], preferred_element_type=F32)
    acc = acc + jnp.dot(ycn, w_ref[half:, :], preferred_element_type=F32)
    h1 = h_ref[...] + acc
    h1_ref[...] = h1
    xn = _rms(h1, n2_ref[...])
    xp_ref[...] = _pack_halves(xn)

    xh, xl = _split_bf16(xn)
    lg = (jnp.dot(xh, rwh_ref[...], preferred_element_type=F32)
          + jnp.dot(xl, rwh_ref[...], preferred_element_type=F32)
          + jnp.dot(xh, rwl_ref[...], preferred_element_type=F32)) + rb_ref[...]

    lane = lax.broadcasted_iota(jnp.int32, lg.shape, 1)
    big = jnp.int32(LANES)
    gl = jnp.where(lane < N_GROUPS, lg, NEG)
    gmax = jnp.max(gl, axis=-1, keepdims=True)
    grp = jnp.min(jnp.where(gl == gmax, lane, big), axis=-1, keepdims=True)
    gate_g = 1.0 / jnp.sum(jnp.exp(gl - gmax), axis=-1, keepdims=True)
    lo = N_GROUPS + grp * EXPERTS_PER_GROUP
    el = jnp.where((lane >= lo) & (lane < lo + EXPERTS_PER_GROUP), lg, NEG)
    v1 = jnp.max(el, axis=-1, keepdims=True)
    i1 = jnp.min(jnp.where(el == v1, lane, big), axis=-1, keepdims=True)
    el2 = jnp.where(lane == i1, NEG, el)
    v2 = jnp.max(el2, axis=-1, keepdims=True)
    i2 = jnp.min(jnp.where(el2 == v2, lane, big), axis=-1, keepdims=True)
    e2 = jnp.exp(v2 - v1)
    g0 = gate_g / (1.0 + e2)
    g1 = gate_g * e2 / (1.0 + e2)
    oh0 = lane == i1 - N_GROUPS
    oh1 = lane == i2 - N_GROUPS
    cnt = jnp.where(oh0 | oh1, 1.0, 0.0)
    before = jnp.dot(tri_ref[...], cnt.astype(BF16), preferred_element_type=F32) + run_ref[...]
    r0 = jnp.sum(jnp.where(oh0, before, 0.0), axis=-1, keepdims=True)
    r1 = jnp.sum(jnp.where(oh1, before, 0.0), axis=-1, keepdims=True)
    run = run_ref[...] + jnp.sum(cnt, axis=0, keepdims=True)
    run_ref[...] = run
    cnt_ref[...] = run

    route = (i1 - N_GROUPS).astype(F32)
    for k, val in enumerate(((i2 - N_GROUPS).astype(F32), g0, g1, r0, r1), start=1):
        route = jnp.where(lane == k, val, route)
    route_ref[...] = jnp.where(lane < ROUTE_LANES, route, 0.0)


def _out_proj(y_ab, y_c, on_c, h, w_bf16, norm2, rw, rb, tm=512):
    s, d = h.shape
    half = y_ab.shape[1]
    rw_pad = jnp.pad(rw, ((0, 0), (0, LANES - ROUTE_COLS)))
    rwh, rwl = _split_bf16(rw_pad)
    rb_pad = jnp.pad(rb, (0, LANES - ROUTE_COLS)).reshape(1, LANES)
    const = lambda i: (0, 0)
    return pl.pallas_call(
        _out_proj_kernel,
        grid=(s // tm,),
        in_specs=[
            pl.BlockSpec((tm, half), lambda i: (i, 0)),
            pl.BlockSpec((tm, half), lambda i: (i, 0)),
            pl.BlockSpec((1, half), const),
            pl.BlockSpec((tm, d), lambda i: (i, 0)),
            pl.BlockSpec((d, d), const, pipeline_mode=pl.Buffered(1)),
            pl.BlockSpec((1, d), const),
            pl.BlockSpec((d, LANES), const),
            pl.BlockSpec((d, LANES), const),
            pl.BlockSpec((1, LANES), const),
        ],
        out_specs=[
            pl.BlockSpec((tm, d), lambda i: (i, 0)),
            pl.BlockSpec((tm, d // 2), lambda i: (i, 0)),
            pl.BlockSpec((tm, LANES), lambda i: (i, 0)),
            pl.BlockSpec((1, LANES), const),
        ],
        out_shape=[
            jax.ShapeDtypeStruct((s, d), F32),
            jax.ShapeDtypeStruct((s, d // 2), U32),
            jax.ShapeDtypeStruct((s, LANES), F32),
            jax.ShapeDtypeStruct((1, LANES), F32),
        ],
        scratch_shapes=[pltpu.VMEM((tm, tm), BF16), pltpu.VMEM((1, LANES), F32)],
        compiler_params=pltpu.CompilerParams(
            dimension_semantics=("arbitrary",), vmem_limit_bytes=VMEM_LIMIT),
        name="out_proj",
    )(y_ab, y_c, on_c.reshape(1, half), h, w_bf16, norm2.reshape(1, d), rwh, rwl, rb_pad)


def _route_plan(route, cnt, t):
    nblk = t * TOP_K // MOE_BLK + N_EXPERTS
    e = route[:, 0:TOP_K].astype(jnp.int32)
    rank = route[:, 4:4 + TOP_K].astype(jnp.int32)
    counts = cnt[0, :N_EXPERTS].astype(jnp.int32)
    pcounts = (counts + MOE_BLK - 1) // MOE_BLK * MOE_BLK
    pends = jnp.cumsum(pcounts)
    pstarts = pends - pcounts
    dest = (pstarts[e] + rank).reshape(-1)
    blk = jnp.arange(nblk, dtype=jnp.int32)
    blk_e = jnp.minimum(jnp.searchsorted(pends, blk * MOE_BLK, side='right'),
                        N_EXPERTS - 1).astype(jnp.int32)
    n_act = pends[-1] // MOE_BLK
    first = (blk == 0) | (blk_e != jnp.roll(blk_e, 1))
    seg_par = (jnp.cumsum(first.astype(jnp.int32)) - 1) % 2
    nxt_blk = pends[blk_e] // MOE_BLK
    nxt_e = jnp.where(nxt_blk < n_act, blk_e[jnp.minimum(nxt_blk, nblk - 1)], -1)
    return (dest, blk_e, first.astype(jnp.int32), seg_par.astype(jnp.int32),
            nxt_e.astype(jnp.int32), n_act.astype(jnp.int32).reshape(1), nblk)


def _dispatch_kernel(dest_ref, xp_ref, xs_in_ref, xs_ref, buf, sem, *, tm, nsteps):
    del xs_in_ref
    i = pl.program_id(0)
    slot = i % 2

    def wait_slot(sl):
        for _ in range(TOP_K):
            pltpu.make_async_copy(buf.at[sl], xs_ref.at[pl.ds(0, tm)], sem.at[sl]).wait()

    @pl.when(i >= 2)
    def _():
        wait_slot(slot)

    buf[slot] = xp_ref[...]
    for rr in range(tm):
        for k in range(TOP_K):
            dst = dest_ref[(i * tm + rr) * TOP_K + k]
            pltpu.make_async_copy(buf.at[slot, pl.ds(rr, 1)], xs_ref.at[pl.ds(dst, 1)],
                                  sem.at[slot]).start()

    @pl.when(i == nsteps - 1)
    def _():
        wait_slot(slot)
        if nsteps > 1:
            wait_slot(1 - slot)


def _dispatch(dest, xp, n_slots, tm=256):
    t, c = xp.shape
    nsteps = t // tm
    xs0 = jnp.zeros((n_slots, c), U32)
    return pl.pallas_call(
        functools.partial(_dispatch_kernel, tm=tm, nsteps=nsteps),
        grid_spec=pltpu.PrefetchScalarGridSpec(
            num_scalar_prefetch=1,
            grid=(nsteps,),
            in_specs=[
                pl.BlockSpec((tm, c), lambda i, dst: (i, 0)),
                pl.BlockSpec(memory_space=pl.ANY),
            ],
            out_specs=pl.BlockSpec(memory_space=pl.ANY),
            scratch_shapes=[pltpu.VMEM((2, tm, c), U32), pltpu.SemaphoreType.DMA((2,))],
        ),
        out_shape=jax.ShapeDtypeStruct((n_slots, c), U32),
        input_output_aliases={2: 0},
        compiler_params=pltpu.CompilerParams(
            dimension_semantics=("arbitrary",), vmem_limit_bytes=VMEM_LIMIT),
        name="dispatch",
    )(dest, xp, xs0)


def _experts_kernel(blk_e_ref, first_ref, par_ref, nxt_ref, n_act_ref,
                    xs_ref, wg_hbm, wu_hbm, wd_hbm, ys_ref,
                    wg_f, wu_f, wd_f, wg_b, wu_b, wd_b, wsem, *, layer):
    i = pl.program_id(0)
    c = xs_ref.shape[1]

    def weight_copies(e, s):
        return (pltpu.make_async_copy(wg_hbm.at[layer, e], wg_f.at[s], wsem.at[s]),
                pltpu.make_async_copy(wu_hbm.at[layer, e], wu_f.at[s], wsem.at[s]),
                pltpu.make_async_copy(wd_hbm.at[layer, e], wd_f.at[s], wsem.at[s]))

    @pl.when(i >= n_act_ref[0])
    def _():
        ys_ref[...] = jnp.zeros(ys_ref.shape, ys_ref.dtype)

    @pl.when(i < n_act_ref[0])
    def _():
        s = par_ref[i]

        @pl.when(first_ref[i] == 1)
        def _():
            @pl.when(i == 0)
            def _():
                for cp in weight_copies(blk_e_ref[0], 0):
                    cp.start()

            for cp in weight_copies(0, s):
                cp.wait()
            wg_b[...] = wg_f[s].astype(BF16)
            wu_b[...] = wu_f[s].astype(BF16)
            wd_b[...] = wd_f[s].astype(BF16)

            @pl.when(nxt_ref[i] >= 0)
            def _():
                for cp in weight_copies(nxt_ref[i], 1 - s):
                    cp.start()

        x_lo, x_hi = _unpack_halves(xs_ref[...])
        x_lo = x_lo.astype(BF16)
        x_hi = x_hi.astype(BF16)
        hg = (jnp.dot(x_lo, wg_b[0:c, :], preferred_element_type=F32)
              + jnp.dot(x_hi, wg_b[c:, :], preferred_element_type=F32))
        hu = (jnp.dot(x_lo, wu_b[0:c, :], preferred_element_type=F32)
              + jnp.dot(x_hi, wu_b[c:, :], preferred_element_type=F32))
        hb = (jax.nn.silu(hg) * hu).astype(BF16)
        ys_ref[...] = _pack_halves(jnp.dot(hb, wd_b[...], preferred_element_type=F32))


def _experts(xs, blk_e, first, seg_par, nxt_e, n_act, nblk, wg, wu, wd, layer):
    n_slots, c = xs.shape
    _, _, d, de = wg.shape
    active_rows = lambda i, be, fi, pa, nx, na: (jnp.minimum(i, na[0] - 1), 0)
    return pl.pallas_call(
        functools.partial(_experts_kernel, layer=layer),
        grid_spec=pltpu.PrefetchScalarGridSpec(
            num_scalar_prefetch=5,
            grid=(nblk,),
            in_specs=[
                pl.BlockSpec((MOE_BLK, c), active_rows),
                pl.BlockSpec(memory_space=pl.ANY),
                pl.BlockSpec(memory_space=pl.ANY),
                pl.BlockSpec(memory_space=pl.ANY),
            ],
            out_specs=pl.BlockSpec((MOE_BLK, c), lambda i, be, fi, pa, nx, na: (i, 0)),
            scratch_shapes=[
                pltpu.VMEM((2, d, de), F32),
                pltpu.VMEM((2, d, de), F32),
                pltpu.VMEM((2, de, d), F32),
                pltpu.VMEM((d, de), BF16),
                pltpu.VMEM((d, de), BF16),
                pltpu.VMEM((de, d), BF16),
                pltpu.SemaphoreType.DMA((2,)),
            ],
        ),
        out_shape=jax.ShapeDtypeStruct((n_slots, c), U32),
        compiler_params=pltpu.CompilerParams(
            dimension_semantics=("arbitrary",), vmem_limit_bytes=VMEM_LIMIT),
        name="experts",
    )(blk_e, first, seg_par, nxt_e, n_act, xs, wg, wu, wd)


def _combine_kernel(dest_ref, h_ref, route_ref, ys_hbm, o_ref, gbuf, sem, *, tm, nsteps):
    i = pl.program_id(0)

    @pl.when(i < nsteps)
    def _():
        for rr in range(tm):
            for k in range(TOP_K):
                src = dest_ref[(i * tm + rr) * TOP_K + k]
                pltpu.make_async_copy(ys_hbm.at[pl.ds(src, 1)],
                                      gbuf.at[i % 2, pl.ds(k * tm + rr, 1)],
                                      sem.at[i % 2]).start()

    @pl.when(i >= 1)
    def _():
        sl = (i - 1) % 2
        for k in range(TOP_K):
            pltpu.make_async_copy(ys_hbm.at[pl.ds(0, tm)], gbuf.at[sl, pl.ds(k * tm, tm)],
                                  sem.at[sl]).wait()
        r = route_ref[...]
        lane = lax.broadcasted_iota(jnp.int32, r.shape, 1)
        g0 = jnp.sum(jnp.where(lane == 2, r, 0.0), axis=-1, keepdims=True)
        g1 = jnp.sum(jnp.where(lane == 3, r, 0.0), axis=-1, keepdims=True)
        a_lo, a_hi = _unpack_halves(gbuf[sl, 0:tm])
        b_lo, b_hi = _unpack_halves(gbuf[sl, tm:2 * tm])
        c = a_lo.shape[1]
        o_ref[:, 0:c] = h_ref[:, 0:c] + (a_lo * g0 + b_lo * g1)
        o_ref[:, c:] = h_ref[:, c:] + (a_hi * g0 + b_hi * g1)


def _combine(dest, h1, route, ys, tm=256):
    t, d = h1.shape
    c = ys.shape[1]
    nsteps = t // tm
    prev = lambda i, dst: (jnp.maximum(i - 1, 0), 0)
    return pl.pallas_call(
        functools.partial(_combine_kernel, tm=tm, nsteps=nsteps),
        grid_spec=pltpu.PrefetchScalarGridSpec(
            num_scalar_prefetch=1,
            grid=(nsteps + 1,),
            in_specs=[
                pl.BlockSpec((tm, d), prev),
                pl.BlockSpec((tm, LANES), prev),
                pl.BlockSpec(memory_space=pl.ANY),
            ],
            out_specs=pl.BlockSpec((tm, d), prev),
            scratch_shapes=[pltpu.VMEM((2, TOP_K * tm, c), U32), pltpu.SemaphoreType.DMA((2,))],
        ),
        out_shape=jax.ShapeDtypeStruct((t, d), F32),
        compiler_params=pltpu.CompilerParams(
            dimension_semantics=("arbitrary",), vmem_limit_bytes=VMEM_LIMIT),
        name="combine",
    )(dest, h1, route, ys)


def kernel(x, norm1, w_in, sgu_norm, sgu_w, sgu_b, conv_w, conv_b, conv_ln_g, conv_ln_b,
           q_norm, k_norm, out_norm, w_out, norm2, w_router_group, b_router_group,
           w_router_expert, b_router_expert, w_expert_gate, w_expert_up, w_expert_down):
    b, s, d = x.shape
    depth = norm1.shape[0]
    t = b * s
    h = x.reshape(t, d)
    a_end, b_end = SGU_WIDTH, SGU_WIDTH + CONV_CH
    for l in range(depth):
        proj = _in_proj(h, norm1[l], w_in[l].astype(BF16))
        y_ab = _mix_ab(proj, sgu_norm[l], sgu_w[l], sgu_b[l], conv_w[l], conv_b[l],
                       conv_ln_g[l], conv_ln_b[l], out_norm[l, :a_end], out_norm[l, a_end:b_end])
        y_c = _attention(proj, q_norm[l], k_norm[l])
        rw = jnp.concatenate([w_router_group[l], w_router_expert[l]], axis=1)
        rb = jnp.concatenate([b_router_group[l], b_router_expert[l]], axis=0)
        h1, xp, route, cnt = _out_proj(y_ab, y_c, out_norm[l, b_end:], h, w_out[l].astype(BF16),
                                       norm2[l], rw, rb)
        dest, blk_e, first, seg_par, nxt_e, n_act, nblk = _route_plan(route, cnt, t)
        xs = _dispatch(dest, xp, nblk * MOE_BLK)
        ys = _experts(xs, blk_e, first, seg_par, nxt_e, n_act, nblk,
                      w_expert_gate, w_expert_up, w_expert_down, l)
        h = _combine(dest, h1, route, ys)
    return h.reshape(b, s, d)
```

```python
import functools

import jax
import jax.numpy as jnp
from jax import lax
from jax.experimental import pallas as pl
from jax.experimental.pallas import tpu as pltpu

F32 = jnp.float32
BF16 = jnp.bfloat16
U32 = jnp.uint32

D_MODEL = 2048
SGU_WIDTH = 512
SGU_HEADS = 4
SGU_CHUNK = 128
CONV_CH = 512
CONV_K = 31
ATTN_WIDTH = 1024
ATTN_HEADS = 8
HEAD_DIM = 128
DILATIONS = (1, 4, 16)
QBLK = 128
D_IN = 2 * SGU_WIDTH + 2 * CONV_CH + 3 * ATTN_WIDTH
N_GROUPS = 4
EXPERTS_PER_GROUP = 8
N_EXPERTS = 32
TOP_K = 2
D_EXPERT = 512
EPS = 1e-6

LANES = 128
VMEM_LIMIT = 56 * 1024 * 1024
NEG = -1e30

ATTN_TILE = QBLK * max(DILATIONS)
CONV_HALO = 32
MOE_BLK = 256
ROUTE_COLS = N_GROUPS + N_EXPERTS
ROUTE_LANES = 6


def _rms(x, g):
    return x * lax.rsqrt(jnp.mean(x * x, axis=-1, keepdims=True) + EPS) * g


def _in_proj_norm_kernel(x_ref, g_ref, w_ref, o_ref, xn_ref):
    @pl.when(pl.program_id(1) == 0)
    def _():
        xn_ref[...] = _rms(x_ref[...], g_ref[...]).astype(BF16)

    o_ref[...] = jnp.dot(xn_ref[...], w_ref[...], preferred_element_type=F32).astype(o_ref.dtype)


def _in_proj_kernel(xn_ref, w_ref, o_ref):
    o_ref[...] = jnp.dot(xn_ref[...], w_ref[...], preferred_element_type=F32).astype(o_ref.dtype)


def _in_proj(x, g, w_bf16, tn=1024):
    s, d = x.shape
    n = w_bf16.shape[1]
    prenormed = x.dtype == BF16
    tm = 2048 if prenormed else 1024
    x_spec = pl.BlockSpec((tm, d), lambda i, j: (i, 0))
    w_spec = pl.BlockSpec((d, tn), lambda i, j: (0, j))
    common = dict(
        grid=(s // tm, n // tn),
        out_specs=pl.BlockSpec((tm, tn), lambda i, j: (i, j)),
        out_shape=jax.ShapeDtypeStruct((s, n), BF16),
        compiler_params=pltpu.CompilerParams(
            dimension_semantics=("parallel", "arbitrary"), vmem_limit_bytes=VMEM_LIMIT),
        name="in_proj",
    )
    if prenormed:
        return pl.pallas_call(_in_proj_kernel, in_specs=[x_spec, w_spec], **common)(x, w_bf16)
    return pl.pallas_call(
        _in_proj_norm_kernel,
        in_specs=[x_spec, pl.BlockSpec((1, d), lambda i, j: (0, 0)), w_spec],
        scratch_shapes=[pltpu.VMEM((tm, d), BF16)],
        **common,
    )(x, g.reshape(1, d), w_bf16)


def _mix_ab_kernel(u_ref, v_ref, a_ref, gt_ref, ah_ref, gh_ref,
                   sgn_ref, sw_ref, sb_ref, cw_ref, cb_ref, lng_ref, lnb_ref,
                   ona_ref, onb_ref, o_ref, zext_ref, *, tq):
    u = jax.nn.gelu(u_ref[...].astype(F32))
    v = _rms(jax.nn.gelu(v_ref[...].astype(F32)), sgn_ref[...]).astype(BF16)
    row = lax.broadcasted_iota(jnp.int32, (SGU_CHUNK, SGU_CHUNK), 0)
    col = lax.broadcasted_iota(jnp.int32, (SGU_CHUNK, SGU_CHUNK), 1)
    causal = col <= row
    wm = [jnp.where(causal, sw_ref[hh], 0.0).astype(BF16) for hh in range(SGU_HEADS)]
    for c in range(tq // SGU_CHUNK):
        rows = slice(c * SGU_CHUNK, (c + 1) * SGU_CHUNK)
        zs = []
        for hh in range(SGU_HEADS):
            cols = slice(hh * HEAD_DIM, (hh + 1) * HEAD_DIM)
            zs.append(jnp.dot(wm[hh], v[rows, cols], preferred_element_type=F32) + sb_ref[hh])
        ya = u[rows, :] * jnp.concatenate(zs, axis=1)
        o_ref[rows, 0:SGU_WIDTH] = _rms(ya, ona_ref[...]).astype(o_ref.dtype)

    first = pl.program_id(0) == 0
    zh = ah_ref[...].astype(F32) * jax.nn.sigmoid(gh_ref[...].astype(F32))
    zext_ref[0:CONV_HALO, :] = jnp.where(first, 0.0, zh)
    zext_ref[CONV_HALO:, :] = a_ref[...].astype(F32) * jax.nn.sigmoid(gt_ref[...].astype(F32))
    rc = 32
    off0 = CONV_HALO - (CONV_K - 1)
    for c in range(tq // rc):
        acc = jnp.zeros((rc, CONV_CH), F32) + cb_ref[...]
        for k in range(CONV_K):
            acc = acc + zext_ref[pl.ds(c * rc + off0 + k, rc), :] * cw_ref[k:k + 1, :]
        mu = jnp.mean(acc, axis=-1, keepdims=True)
        xc = acc - mu
        y = xc * lax.rsqrt(jnp.mean(xc * xc, axis=-1, keepdims=True) + EPS)
        y = jax.nn.silu(y * lng_ref[...] + lnb_ref[...])
        o_ref[c * rc:(c + 1) * rc, SGU_WIDTH:] = _rms(y, onb_ref[...]).astype(o_ref.dtype)


def _mix_ab(proj, sgu_norm, sgu_w, sgu_b, conv_w, conv_b, ln_g, ln_b, on_a, on_b, tq=512):
    s = proj.shape[0]
    w = SGU_WIDTH
    hb = tq // CONV_HALO
    sb = jnp.broadcast_to(sgu_b[:, :, None], (SGU_HEADS, SGU_CHUNK, HEAD_DIM))
    cw = jnp.pad(conv_w, ((0, 32 - CONV_K), (0, 0)))
    vec = lambda a: a.reshape(1, -1)
    const2 = lambda i: (0, 0)
    const3 = lambda i: (0, 0, 0)
    return pl.pallas_call(
        functools.partial(_mix_ab_kernel, tq=tq),
        grid=(s // tq,),
        in_specs=[
            pl.BlockSpec((tq, w), lambda i: (i, 0)),
            pl.BlockSpec((tq, w), lambda i: (i, 1)),
            pl.BlockSpec((tq, w), lambda i: (i, 2)),
            pl.BlockSpec((tq, w), lambda i: (i, 3)),
            pl.BlockSpec((CONV_HALO, w), lambda i: (jnp.maximum(i * hb - 1, 0), 2)),
            pl.BlockSpec((CONV_HALO, w), lambda i: (jnp.maximum(i * hb - 1, 0), 3)),
            pl.BlockSpec((1, w), const2),
            pl.BlockSpec((SGU_HEADS, SGU_CHUNK, SGU_CHUNK), const3),
            pl.BlockSpec((SGU_HEADS, SGU_CHUNK, HEAD_DIM), const3),
            pl.BlockSpec((32, w), const2),
            pl.BlockSpec((1, w), const2),
            pl.BlockSpec((1, w), const2),
            pl.BlockSpec((1, w), const2),
            pl.BlockSpec((1, w), const2),
            pl.BlockSpec((1, w), const2),
        ],
        out_specs=pl.BlockSpec((tq, 2 * w), lambda i: (i, 0)),
        out_shape=jax.ShapeDtypeStruct((s, 2 * w), BF16),
        scratch_shapes=[pltpu.VMEM((tq + CONV_HALO, w), F32)],
        compiler_params=pltpu.CompilerParams(
            dimension_semantics=("parallel",), vmem_limit_bytes=VMEM_LIMIT),
        name="mix_ab",
    )(proj, proj, proj, proj, proj, proj, vec(sgu_norm), sgu_w, sb, cw, vec(conv_b),
      vec(ln_g), vec(ln_b), vec(on_a), vec(on_b))


def _attn_kernel(q_ref, k_ref, v_ref, qg_ref, kg_ref, o_ref,
                 qn_ref, kn_ref, vv_ref, ob_ref, lb_ref):
    t = pl.program_id(1)
    tl = ATTN_TILE

    @pl.when(t == 0)
    def _():
        kn_ref[0:tl, :] = jnp.zeros((tl, HEAD_DIM), F32)
        vv_ref[0:tl, :] = jnp.zeros((tl, HEAD_DIM), F32)

    @pl.when(t > 0)
    def _():
        kn_ref[0:tl, :] = kn_ref[tl:, :]
        vv_ref[0:tl, :] = vv_ref[tl:, :]

    kn_ref[tl:, :] = _rms(k_ref[...].astype(F32), kg_ref[...])
    vv_ref[tl:, :] = v_ref[...].astype(F32)
    qn_ref[...] = _rms(q_ref[...].astype(F32), qg_ref[...]) * (HEAD_DIM ** -0.5)

    qi = lax.broadcasted_iota(jnp.int32, (QBLK, 2 * QBLK), 0)
    kj = lax.broadcasted_iota(jnp.int32, (QBLK, 2 * QBLK), 1)
    band = (kj >= qi) & (kj <= qi + QBLK)
    band_first = band & (kj >= jnp.where(t > 0, 0, QBLK))

    for g, dil in enumerate(DILATIONS):
        for r in range(dil):
            for b in range(tl // (QBLK * dil)):
                base = b * QBLK * dil + r
                qb = qn_ref[pl.ds(base, QBLK, stride=dil), :].astype(BF16)
                kb = kn_ref[pl.ds(tl + base - QBLK * dil, 2 * QBLK, stride=dil), :].astype(BF16)
                vb = vv_ref[pl.ds(tl + base - QBLK * dil, 2 * QBLK, stride=dil), :].astype(BF16)
                sc = lax.dot_general(qb, kb, (((1,), (1,)), ((), ())),
                                     preferred_element_type=F32)
                sc = jnp.where(band_first if b == 0 else band, sc, NEG)
                m = jnp.max(sc, axis=-1, keepdims=True)
                p = jnp.exp(sc - m)
                l = jnp.sum(p, axis=-1, keepdims=True)
                ob = jnp.dot(p.astype(BF16), vb, preferred_element_type=F32) / l
                ob_ref[g, pl.ds(base, QBLK, stride=dil), :] = ob
                lb_ref[g, pl.ds(base, QBLK, stride=dil), :] = jnp.broadcast_to(
                    m + jnp.log(l), (QBLK, HEAD_DIM))

    l0, l1, l2 = lb_ref[0], lb_ref[1], lb_ref[2]
    mx = jnp.maximum(jnp.maximum(l0, l1), l2)
    w0, w1, w2 = jnp.exp(l0 - mx), jnp.exp(l1 - mx), jnp.exp(l2 - mx)
    y = (w0 * ob_ref[0] + w1 * ob_ref[1] + w2 * ob_ref[2]) / (w0 + w1 + w2)
    o_ref[...] = y.astype(o_ref.dtype)


def _attention(proj, q_g, k_g):
    s = proj.shape[0]
    tl = ATTN_TILE
    qc = (2 * SGU_WIDTH + 2 * CONV_CH) // HEAD_DIM
    kc = qc + ATTN_HEADS
    vc = kc + ATTN_HEADS
    return pl.pallas_call(
        _attn_kernel,
        grid=(ATTN_HEADS, s // tl),
        in_specs=[
            pl.BlockSpec((tl, HEAD_DIM), lambda h, t: (t, qc + h)),
            pl.BlockSpec((tl, HEAD_DIM), lambda h, t: (t, kc + h)),
            pl.BlockSpec((tl, HEAD_DIM), lambda h, t: (t, vc + h)),
            pl.BlockSpec((1, HEAD_DIM), lambda h, t: (0, 0)),
            pl.BlockSpec((1, HEAD_DIM), lambda h, t: (0, 0)),
        ],
        out_specs=pl.BlockSpec((tl, HEAD_DIM), lambda h, t: (t, h)),
        out_shape=jax.ShapeDtypeStruct((s, ATTN_WIDTH), BF16),
        scratch_shapes=[
            pltpu.VMEM((tl, HEAD_DIM), F32),
            pltpu.VMEM((2 * tl, HEAD_DIM), F32),
            pltpu.VMEM((2 * tl, HEAD_DIM), F32),
            pltpu.VMEM((len(DILATIONS), tl, HEAD_DIM), F32),
            pltpu.VMEM((len(DILATIONS), tl, HEAD_DIM), F32),
        ],
        compiler_params=pltpu.CompilerParams(
            dimension_semantics=("parallel", "arbitrary"), vmem_limit_bytes=VMEM_LIMIT),
        name="attention",
    )(proj, proj, proj, q_g.reshape(1, HEAD_DIM), k_g.reshape(1, HEAD_DIM))


def _split_bf16(x):
    hi = x.astype(BF16)
    lo = (x - hi.astype(F32)).astype(BF16)
    return hi, lo


def _pack_halves(x):
    c = x.shape[1] // 2
    lo = lax.bitcast_convert_type(x[:, :c].astype(BF16).astype(F32), U32)
    hi = lax.bitcast_convert_type(x[:, c:].astype(BF16).astype(F32), U32)
    return (lo >> 16) | hi


def _unpack_halves(u):
    lo = lax.bitcast_convert_type(u << 16, F32)
    hi = lax.bitcast_convert_type(u & jnp.uint32(0xFFFF0000), F32)
    return lo, hi


def _out_proj_kernel(yab_ref, yc_ref, onc_ref, h_ref, w_ref, n2_ref, rwh_ref, rwl_ref, rb_ref,
                     h1_ref, xp_ref, route_ref, routet_ref, cnt_ref, tri_ref, run_ref):
    half = yab_ref.shape[1]
    tm = h_ref.shape[0]

    @pl.when(pl.program_id(0) == 0)
    def _():
        r = lax.broadcasted_iota(jnp.int32, (tm, tm), 0)
        c = lax.broadcasted_iota(jnp.int32, (tm, tm), 1)
        tri_ref[...] = jnp.where(c < r, 1.0, 0.0).astype(BF16)
        run_ref[...] = jnp.zeros(run_ref.shape, F32)

    ycn = _rms(yc_ref[...].astype(F32), onc_ref[...]).astype(BF16)
    acc = jnp.dot(yab_ref[...], w_ref[0:half, :], preferred_element_type=F32)
    acc = acc + jnp.dot(ycn, w_ref[half:, :], preferred_element_type=F32)
    h1 = h_ref[...] + acc
    h1_ref[...] = h1
    xn = _rms(h1, n2_ref[...])
    xp_ref[...] = _pack_halves(xn)

    xh, xl = _split_bf16(xn)
    lg = (jnp.dot(xh, rwh_ref[...], preferred_element_type=F32)
          + jnp.dot(xl, rwh_ref[...], preferred_element_type=F32)
          + jnp.dot(xh, rwl_ref[...], preferred_element_type=F32)) + rb_ref[...]

    lane = lax.broadcasted_iota(jnp.int32, lg.shape, 1)
    big = jnp.int32(LANES)
    gl = jnp.where(lane < N_GROUPS, lg, NEG)
    gmax = jnp.max(gl, axis=-1, keepdims=True)
    grp = jnp.min(jnp.where(gl == gmax, lane, big), axis=-1, keepdims=True)
    gate_g = 1.0 / jnp.sum(jnp.exp(gl - gmax), axis=-1, keepdims=True)
    lo = N_GROUPS + grp * EXPERTS_PER_GROUP
    el = jnp.where((lane >= lo) & (lane < lo + EXPERTS_PER_GROUP), lg, NEG)
    v1 = jnp.max(el, axis=-1, keepdims=True)
    i1 = jnp.min(jnp.where(el == v1, lane, big), axis=-1, keepdims=True)
    el2 = jnp.where(lane == i1, NEG, el)
    v2 = jnp.max(el2, axis=-1, keepdims=True)
    i2 = jnp.min(jnp.where(el2 == v2, lane, big), axis=-1, keepdims=True)
    e2 = jnp.exp(v2 - v1)
    g0 = gate_g / (1.0 + e2)
    g1 = gate_g * e2 / (1.0 + e2)
    oh0 = lane == i1 - N_GROUPS
    oh1 = lane == i2 - N_GROUPS
    cnt = jnp.where(oh0 | oh1, 1.0, 0.0)
    before = jnp.dot(tri_ref[...], cnt.astype(BF16), preferred_element_type=F32) + run_ref[...]
    r0 = jnp.sum(jnp.where(oh0, before, 0.0), axis=-1, keepdims=True)
    r1 = jnp.sum(jnp.where(oh1, before, 0.0), axis=-1, keepdims=True)
    run = run_ref[...] + jnp.sum(cnt, axis=0, keepdims=True)
    run_ref[...] = run
    cnt_ref[...] = run

    route = (i1 - N_GROUPS).astype(F32)
    for k, val in enumerate(((i2 - N_GROUPS).astype(F32), g0, g1, r0, r1), start=1):
        route = jnp.where(lane == k, val, route)
    route = jnp.where(lane < ROUTE_LANES, route, 0.0)
    route_ref[...] = route
    routet_ref[...] = route.T[0:8, :]


def _out_proj(y_ab, y_c, on_c, h, w_bf16, norm2, rw, rb, tm=512):
    s, d = h.shape
    half = y_ab.shape[1]
    rw_pad = jnp.pad(rw, ((0, 0), (0, LANES - ROUTE_COLS)))
    rwh, rwl = _split_bf16(rw_pad)
    rb_pad = jnp.pad(rb, (0, LANES - ROUTE_COLS)).reshape(1, LANES)
    const = lambda i: (0, 0)
    return pl.pallas_call(
        _out_proj_kernel,
        grid=(s // tm,),
        in_specs=[
            pl.BlockSpec((tm, half), lambda i: (i, 0)),
            pl.BlockSpec((tm, half), lambda i: (i, 0)),
            pl.BlockSpec((1, half), const),
            pl.BlockSpec((tm, d), lambda i: (i, 0)),
            pl.BlockSpec((d, d), const, pipeline_mode=pl.Buffered(1)),
            pl.BlockSpec((1, d), const),
            pl.BlockSpec((d, LANES), const),
            pl.BlockSpec((d, LANES), const),
            pl.BlockSpec((1, LANES), const),
        ],
        out_specs=[
            pl.BlockSpec((tm, d), lambda i: (i, 0)),
            pl.BlockSpec((tm, d // 2), lambda i: (i, 0)),
            pl.BlockSpec((tm, LANES), lambda i: (i, 0)),
            pl.BlockSpec((8, tm), lambda i: (0, i)),
            pl.BlockSpec((1, LANES), const),
        ],
        out_shape=[
            jax.ShapeDtypeStruct((s, d), F32),
            jax.ShapeDtypeStruct((s, d // 2), U32),
            jax.ShapeDtypeStruct((s, LANES), F32),
            jax.ShapeDtypeStruct((8, s), F32),
            jax.ShapeDtypeStruct((1, LANES), F32),
        ],
        scratch_shapes=[pltpu.VMEM((tm, tm), BF16), pltpu.VMEM((1, LANES), F32)],
        compiler_params=pltpu.CompilerParams(
            dimension_semantics=("arbitrary",), vmem_limit_bytes=VMEM_LIMIT),
        name="out_proj",
    )(y_ab, y_c, on_c.reshape(1, half), h, w_bf16, norm2.reshape(1, d), rwh, rwl, rb_pad)


def _route_plan(route_t, cnt, t):
    nblk = t * TOP_K // MOE_BLK + N_EXPERTS
    ids = jnp.arange(N_EXPERTS, dtype=jnp.int32)
    e = route_t[0:TOP_K].astype(jnp.int32)
    rank = route_t[4:4 + TOP_K].astype(jnp.int32)
    counts = cnt[0, :N_EXPERTS].astype(jnp.int32)
    pcounts = (counts + MOE_BLK - 1) // MOE_BLK * MOE_BLK
    pends = jnp.cumsum(pcounts)
    pstarts = pends - pcounts
    dest = (jnp.sum(jnp.where(e[:, :, None] == ids, pstarts, 0), axis=-1) + rank).reshape(-1)
    blk = jnp.arange(nblk, dtype=jnp.int32)
    blk_e = jnp.minimum(jnp.sum((blk[:, None] * MOE_BLK >= pends[None, :]).astype(jnp.int32), axis=1),
                        N_EXPERTS - 1)
    n_act = pends[-1] // MOE_BLK
    mine = blk_e[:, None] == ids[None, :]
    first = (blk == 0) | (blk_e != jnp.roll(blk_e, 1))
    seg_par = (jnp.cumsum(first.astype(jnp.int32)) - 1) % 2
    nxt_blk = jnp.sum(jnp.where(mine, pends, 0), axis=1) // MOE_BLK
    nxt_e = jnp.where(nxt_blk < n_act,
                      jnp.sum(jnp.where(nxt_blk[:, None] == blk[None, :], blk_e[None, :], 0), axis=1),
                      -1)
    i32 = lambda a: a.astype(jnp.int32)
    return (i32(dest), i32(pstarts + counts), i32(pcounts - counts), i32(blk_e), i32(first),
            i32(seg_par), i32(nxt_e), i32(n_act).reshape(1), nblk)


def _dispatch_kernel(dest_ref, pad_start_ref, pad_n_ref, n_act_ref, xp_ref, xs_ref,
                     buf, zbuf, sem, zsem, *, tm, nsteps, nblk):
    i = pl.program_id(0)
    slot = i % 2

    def wait_slot(sl):
        for _ in range(TOP_K):
            pltpu.make_async_copy(buf.at[sl], xs_ref.at[pl.ds(0, tm)], sem.at[sl]).wait()

    def for_padding_copies(fn):
        sub = 8
        for e in range(N_EXPERTS):
            n = pad_n_ref[e]
            start = pad_start_ref[e]
            head = jnp.minimum((sub - (start & (sub - 1))) & (sub - 1), n)
            for j in range(sub - 1):
                @pl.when(j < head)
                def _(row=start + j):
                    fn(pltpu.make_async_copy(zbuf.at[pl.ds(0, 1)], xs_ref.at[pl.ds(row, 1)], zsem))
            n = n - head
            start = start + head
            p = MOE_BLK // 2
            while p >= sub:
                @pl.when((n & p) != 0)
                def _(start=start, p=p):
                    fn(pltpu.make_async_copy(zbuf.at[pl.ds(0, p)],
                                             xs_ref.at[pl.ds(pl.multiple_of(start, sub), p)], zsem))
                start = start + (n & p)
                p //= 2
        for j in range(nblk - nsteps * tm * TOP_K // MOE_BLK):
            blk = n_act_ref[0] + j

            @pl.when(blk < nblk)
            def _(blk=blk):
                fn(pltpu.make_async_copy(zbuf, xs_ref.at[pl.ds(blk * MOE_BLK, MOE_BLK)], zsem))

    @pl.when(i == 0)
    def _():
        zbuf[...] = jnp.zeros(zbuf.shape, zbuf.dtype)
        for_padding_copies(lambda cp: cp.start())

    @pl.when(i >= 2)
    def _():
        wait_slot(slot)

    buf[slot] = xp_ref[...]
    for rr in range(tm):
        for k in range(TOP_K):
            dst = dest_ref[k * tm * nsteps + i * tm + rr]
            pltpu.make_async_copy(buf.at[slot, pl.ds(rr, 1)], xs_ref.at[pl.ds(dst, 1)],
                                  sem.at[slot]).start()

    @pl.when(i == nsteps - 1)
    def _():
        wait_slot(slot)
        if nsteps > 1:
            wait_slot(1 - slot)
        for_padding_copies(lambda cp: cp.wait())


def _dispatch(dest, pad_start, pad_n, n_act, xp, nblk, tm=256):
    t, c = xp.shape
    nsteps = t // tm
    return pl.pallas_call(
        functools.partial(_dispatch_kernel, tm=tm, nsteps=nsteps, nblk=nblk),
        grid_spec=pltpu.PrefetchScalarGridSpec(
            num_scalar_prefetch=4,
            grid=(nsteps,),
            in_specs=[pl.BlockSpec((tm, c), lambda i, *_: (i, 0))],
            out_specs=pl.BlockSpec(memory_space=pl.ANY),
            scratch_shapes=[pltpu.VMEM((2, tm, c), U32), pltpu.VMEM((MOE_BLK, c), U32),
                            pltpu.SemaphoreType.DMA((2,)), pltpu.SemaphoreType.DMA],
        ),
        out_shape=jax.ShapeDtypeStruct((nblk * MOE_BLK, c), U32),
        compiler_params=pltpu.CompilerParams(
            dimension_semantics=("arbitrary",), vmem_limit_bytes=VMEM_LIMIT),
        name="dispatch",
    )(dest, pad_start, pad_n, n_act, xp)


def _experts_kernel(blk_e_ref, first_ref, par_ref, nxt_ref, n_act_ref,
                    xs_ref, wg_hbm, wu_hbm, wd_hbm, ys_ref,
                    wg_f, wu_f, wd_f, wg_b, wu_b, wd_b, wsem, *, layer):
    i = pl.program_id(0)
    c = xs_ref.shape[1]

    def weight_copies(e, s):
        return (pltpu.make_async_copy(wg_hbm.at[layer, e], wg_f.at[s], wsem.at[s]),
                pltpu.make_async_copy(wu_hbm.at[layer, e], wu_f.at[s], wsem.at[s]),
                pltpu.make_async_copy(wd_hbm.at[layer, e], wd_f.at[s], wsem.at[s]))

    @pl.when(i >= n_act_ref[0])
    def _():
        ys_ref[...] = jnp.zeros(ys_ref.shape, ys_ref.dtype)

    @pl.when(i < n_act_ref[0])
    def _():
        s = par_ref[i]

        @pl.when(first_ref[i] == 1)
        def _():
            @pl.when(i == 0)
            def _():
                for cp in weight_copies(blk_e_ref[0], 0):
                    cp.start()

            for cp in weight_copies(0, s):
                cp.wait()
            wg_b[...] = wg_f[s].astype(BF16)
            wu_b[...] = wu_f[s].astype(BF16)
            wd_b[...] = wd_f[s].astype(BF16)

            @pl.when(nxt_ref[i] >= 0)
            def _():
                for cp in weight_copies(nxt_ref[i], 1 - s):
                    cp.start()

        x_lo, x_hi = _unpack_halves(xs_ref[...])
        x_lo = x_lo.astype(BF16)
        x_hi = x_hi.astype(BF16)
        hg = (jnp.dot(x_lo, wg_b[0:c, :], preferred_element_type=F32)
              + jnp.dot(x_hi, wg_b[c:, :], preferred_element_type=F32))
        hu = (jnp.dot(x_lo, wu_b[0:c, :], preferred_element_type=F32)
              + jnp.dot(x_hi, wu_b[c:, :], preferred_element_type=F32))
        hb = (jax.nn.silu(hg) * hu).astype(BF16)
        ys_ref[...] = _pack_halves(jnp.dot(hb, wd_b[...], preferred_element_type=F32))


def _experts(xs, blk_e, first, seg_par, nxt_e, n_act, nblk, wg, wu, wd, layer):
    n_slots, c = xs.shape
    _, _, d, de = wg.shape
    active_rows = lambda i, be, fi, pa, nx, na: (jnp.minimum(i, na[0] - 1), 0)
    return pl.pallas_call(
        functools.partial(_experts_kernel, layer=layer),
        grid_spec=pltpu.PrefetchScalarGridSpec(
            num_scalar_prefetch=5,
            grid=(nblk,),
            in_specs=[
                pl.BlockSpec((MOE_BLK, c), active_rows),
                pl.BlockSpec(memory_space=pl.ANY),
                pl.BlockSpec(memory_space=pl.ANY),
                pl.BlockSpec(memory_space=pl.ANY),
            ],
            out_specs=pl.BlockSpec((MOE_BLK, c), lambda i, *_: (i, 0)),
            scratch_shapes=[
                pltpu.VMEM((2, d, de), F32),
                pltpu.VMEM((2, d, de), F32),
                pltpu.VMEM((2, de, d), F32),
                pltpu.VMEM((d, de), BF16),
                pltpu.VMEM((d, de), BF16),
                pltpu.VMEM((de, d), BF16),
                pltpu.SemaphoreType.DMA((2,)),
            ],
        ),
        out_shape=jax.ShapeDtypeStruct((n_slots, c), U32),
        compiler_params=pltpu.CompilerParams(
            dimension_semantics=("arbitrary",), vmem_limit_bytes=VMEM_LIMIT),
        name="experts",
    )(blk_e, first, seg_par, nxt_e, n_act, xs, wg, wu, wd)


def _combine_kernel(dest_ref, h_ref, route_ref, ys_hbm, *rest, tm, nsteps, emit_next):
    if emit_next:
        gn_ref, o_ref, xn_ref, gbuf, sem = rest
    else:
        o_ref, gbuf, sem = rest
    i = pl.program_id(0)

    @pl.when(i < nsteps)
    def _():
        for rr in range(tm):
            for k in range(TOP_K):
                src = dest_ref[k * tm * nsteps + i * tm + rr]
                pltpu.make_async_copy(ys_hbm.at[pl.ds(src, 1)],
                                      gbuf.at[i % 2, pl.ds(k * tm + rr, 1)],
                                      sem.at[i % 2]).start()

    @pl.when(i >= 1)
    def _():
        sl = (i - 1) % 2
        for k in range(TOP_K):
            pltpu.make_async_copy(ys_hbm.at[pl.ds(0, tm)], gbuf.at[sl, pl.ds(k * tm, tm)],
                                  sem.at[sl]).wait()
        r = route_ref[...]
        lane = lax.broadcasted_iota(jnp.int32, r.shape, 1)
        g0 = jnp.sum(jnp.where(lane == 2, r, 0.0), axis=-1, keepdims=True)
        g1 = jnp.sum(jnp.where(lane == 3, r, 0.0), axis=-1, keepdims=True)
        a_lo, a_hi = _unpack_halves(gbuf[sl, 0:tm])
        b_lo, b_hi = _unpack_halves(gbuf[sl, tm:2 * tm])
        c = a_lo.shape[1]
        h_lo = h_ref[:, 0:c] + (a_lo * g0 + b_lo * g1)
        h_hi = h_ref[:, c:] + (a_hi * g0 + b_hi * g1)
        o_ref[:, 0:c] = h_lo
        o_ref[:, c:] = h_hi
        if emit_next:
            ms = (jnp.sum(h_lo * h_lo, axis=-1, keepdims=True)
                  + jnp.sum(h_hi * h_hi, axis=-1, keepdims=True)) / (2 * c)
            inv = lax.rsqrt(ms + EPS)
            xn_ref[:, 0:c] = (h_lo * inv * gn_ref[:, 0:c]).astype(BF16)
            xn_ref[:, c:] = (h_hi * inv * gn_ref[:, c:]).astype(BF16)


def _combine(dest, h1, route, ys, next_gain=None, tm=256):
    t, d = h1.shape
    c = ys.shape[1]
    nsteps = t // tm
    emit_next = next_gain is not None
    prev = lambda i, dst: (jnp.maximum(i - 1, 0), 0)
    in_specs = [
        pl.BlockSpec((tm, d), prev),
        pl.BlockSpec((tm, LANES), prev),
        pl.BlockSpec(memory_space=pl.ANY),
    ]
    out_specs = [pl.BlockSpec((tm, d), prev)]
    out_shape = [jax.ShapeDtypeStruct((t, d), F32)]
    args = [dest, h1, route, ys]
    if emit_next:
        in_specs.append(pl.BlockSpec((1, d), lambda i, dst: (0, 0)))
        out_specs.append(pl.BlockSpec((tm, d), prev))
        out_shape.append(jax.ShapeDtypeStruct((t, d), BF16))
        args.append(next_gain.reshape(1, d))
    return pl.pallas_call(
        functools.partial(_combine_kernel, tm=tm, nsteps=nsteps, emit_next=emit_next),
        grid_spec=pltpu.PrefetchScalarGridSpec(
            num_scalar_prefetch=1,
            grid=(nsteps + 1,),
            in_specs=in_specs,
            out_specs=out_specs,
            scratch_shapes=[pltpu.VMEM((2, TOP_K * tm, c), U32), pltpu.SemaphoreType.DMA((2,))],
        ),
        out_shape=out_shape,
        compiler_params=pltpu.CompilerParams(
            dimension_semantics=("arbitrary",), vmem_limit_bytes=VMEM_LIMIT),
        name="combine",
    )(*args)


def kernel(x, norm1, w_in, sgu_norm, sgu_w, sgu_b, conv_w, conv_b, conv_ln_g, conv_ln_b,
           q_norm, k_norm, out_norm, w_out, norm2, w_router_group, b_router_group,
           w_router_expert, b_router_expert, w_expert_gate, w_expert_up, w_expert_down):
    b, s, d = x.shape
    depth = norm1.shape[0]
    t = b * s
    h = x.reshape(t, d)
    a_end, b_end = SGU_WIDTH, SGU_WIDTH + CONV_CH
    x_in = h
    for l in range(depth):
        proj = _in_proj(x_in, norm1[l], w_in[l].astype(BF16))
        y_ab = _mix_ab(proj, sgu_norm[l], sgu_w[l], sgu_b[l], conv_w[l], conv_b[l],
                       conv_ln_g[l], conv_ln_b[l], out_norm[l, :a_end], out_norm[l, a_end:b_end])
        y_c = _attention(proj, q_norm[l], k_norm[l])
        rw = jnp.concatenate([w_router_group[l], w_router_expert[l]], axis=1)
        rb = jnp.concatenate([b_router_group[l], b_router_expert[l]], axis=0)
        h1, xp, route, route_t, cnt = _out_proj(y_ab, y_c, out_norm[l, b_end:], h,
                                                w_out[l].astype(BF16), norm2[l], rw, rb)
        (dest, pad_start, pad_n, blk_e, first, seg_par, nxt_e, n_act,
         nblk) = _route_plan(route_t, cnt, t)
        xs = _dispatch(dest, pad_start, pad_n, n_act, xp, nblk)
        ys = _experts(xs, blk_e, first, seg_par, nxt_e, n_act, nblk,
                      w_expert_gate, w_expert_up, w_expert_down, l)
        if l + 1 < depth:
            h, x_in = _combine(dest, h1, route, ys, next_gain=norm1[l + 1])
        else:
            (h,) = _combine(dest, h1, route, ys)
    return h.reshape(b, s, d)
```

```python
import functools

import jax
import jax.numpy as jnp
from jax import lax
from jax.experimental import pallas as pl
from jax.experimental.pallas import tpu as pltpu

F32 = jnp.float32
BF16 = jnp.bfloat16
U32 = jnp.uint32

D_MODEL = 2048
SGU_WIDTH = 512
SGU_HEADS = 4
SGU_CHUNK = 128
CONV_CH = 512
CONV_K = 31
ATTN_WIDTH = 1024
ATTN_HEADS = 8
HEAD_DIM = 128
DILATIONS = (1, 4, 16)
QBLK = 128
D_IN = 2 * SGU_WIDTH + 2 * CONV_CH + 3 * ATTN_WIDTH
N_GROUPS = 4
EXPERTS_PER_GROUP = 8
N_EXPERTS = 32
TOP_K = 2
D_EXPERT = 512
EPS = 1e-6

LANES = 128
VMEM_LIMIT = 56 * 1024 * 1024
NEG = -1e30

ATTN_TILE = QBLK * max(DILATIONS)
CONV_HALO = 32
MOE_BLK = 256
ROUTE_COLS = N_GROUPS + N_EXPERTS
ROUTE_LANES = 6


def _rms(x, g):
    return x * lax.rsqrt(jnp.mean(x * x, axis=-1, keepdims=True) + EPS) * g


def _in_proj_norm_kernel(x_ref, g_ref, w_ref, o_ref, xn_ref):
    @pl.when(pl.program_id(1) == 0)
    def _():
        xn_ref[...] = _rms(x_ref[...], g_ref[...]).astype(BF16)

    o_ref[...] = jnp.dot(xn_ref[...], w_ref[...].astype(BF16),
                         preferred_element_type=F32).astype(o_ref.dtype)


def _in_proj_kernel(xn_ref, w_ref, o_ref):
    o_ref[...] = jnp.dot(xn_ref[...], w_ref[...].astype(BF16),
                         preferred_element_type=F32).astype(o_ref.dtype)


def _in_proj(x, g, w_in, layer, tn=1024):
    s, d = x.shape
    n = w_in.shape[2]
    prenormed = x.dtype == BF16
    tm = 2048 if prenormed else 1024
    x_spec = pl.BlockSpec((tm, d), lambda i, j: (i, 0))
    w_spec = pl.BlockSpec((None, d, tn), lambda i, j: (layer, 0, j))
    common = dict(
        grid=(s // tm, n // tn),
        out_specs=pl.BlockSpec((tm, tn), lambda i, j: (i, j)),
        out_shape=jax.ShapeDtypeStruct((s, n), BF16),
        compiler_params=pltpu.CompilerParams(
            dimension_semantics=("parallel", "arbitrary"), vmem_limit_bytes=VMEM_LIMIT),
        name="in_proj",
    )
    if prenormed:
        return pl.pallas_call(_in_proj_kernel, in_specs=[x_spec, w_spec], **common)(x, w_in)
    return pl.pallas_call(
        _in_proj_norm_kernel,
        in_specs=[x_spec, pl.BlockSpec((1, d), lambda i, j: (0, 0)), w_spec],
        scratch_shapes=[pltpu.VMEM((tm, d), BF16)],
        **common,
    )(x, g.reshape(1, d), w_in)


def _mix_ab_kernel(u_ref, v_ref, a_ref, gt_ref, ah_ref, gh_ref,
                   sgn_ref, sw_ref, sb_ref, cw_ref, cb_ref, lng_ref, lnb_ref,
                   ona_ref, onb_ref, o_ref, zext_ref, conv_ref, stage_ref, *, tq):
    u = jax.nn.gelu(u_ref[...].astype(F32))
    v = _rms(jax.nn.gelu(v_ref[...].astype(F32)), sgn_ref[...]).astype(BF16)
    row = lax.broadcasted_iota(jnp.int32, (SGU_CHUNK, SGU_CHUNK), 0)
    col = lax.broadcasted_iota(jnp.int32, (SGU_CHUNK, SGU_CHUNK), 1)
    causal = col <= row
    wm = [jnp.where(causal, sw_ref[hh], 0.0).astype(BF16) for hh in range(SGU_HEADS)]
    for c in range(tq // SGU_CHUNK):
        rows = slice(c * SGU_CHUNK, (c + 1) * SGU_CHUNK)
        zs = []
        for hh in range(SGU_HEADS):
            cols = slice(hh * HEAD_DIM, (hh + 1) * HEAD_DIM)
            zs.append(jnp.dot(wm[hh], v[rows, cols], preferred_element_type=F32) + sb_ref[hh])
        ya = u[rows, :] * jnp.concatenate(zs, axis=1)
        o_ref[rows, 0:SGU_WIDTH] = _rms(ya, ona_ref[...]).astype(o_ref.dtype)

    first = pl.program_id(0) == 0
    zh = ah_ref[...].astype(F32) * jax.nn.sigmoid(gh_ref[...].astype(F32))
    zext_ref[0:CONV_HALO, :] = jnp.where(first, 0.0, zh)
    zext_ref[CONV_HALO:, :] = a_ref[...].astype(F32) * jax.nn.sigmoid(gt_ref[...].astype(F32))
    off0 = CONV_HALO - (CONV_K - 1)
    sub = 8
    rc = 128
    for j in range(CONV_CH // LANES):
        cols = slice(j * LANES, (j + 1) * LANES)
        for c in range(tq // rc):
            acc = jnp.zeros((rc, LANES), F32) + cb_ref[:, cols]
            for b in range(sub):
                taps = [(a, sub * a + b - off0) for a in range(CONV_HALO // sub + 1)
                        if 0 <= sub * a + b - off0 < CONV_K]
                rows = rc + sub * taps[-1][0]
                stage_ref[b, 0:rows, :] = zext_ref[pl.ds(c * rc + b, rows), cols]
                for a, k in taps:
                    acc = acc + stage_ref[b, sub * a:sub * a + rc, :] * cw_ref[k:k + 1, cols]
            conv_ref[c * rc:(c + 1) * rc, cols] = acc
    rc = 32
    for c in range(tq // rc):
        acc = conv_ref[c * rc:(c + 1) * rc, :]
        mu = jnp.mean(acc, axis=-1, keepdims=True)
        xc = acc - mu
        y = xc * lax.rsqrt(jnp.mean(xc * xc, axis=-1, keepdims=True) + EPS)
        y = jax.nn.silu(y * lng_ref[...] + lnb_ref[...])
        o_ref[c * rc:(c + 1) * rc, SGU_WIDTH:] = _rms(y, onb_ref[...]).astype(o_ref.dtype)


def _mix_ab(proj, sgu_norm, sgu_w, sgu_b, conv_w, conv_b, ln_g, ln_b, on_a, on_b, tq=512):
    s = proj.shape[0]
    w = SGU_WIDTH
    hb = tq // CONV_HALO
    sb = jnp.broadcast_to(sgu_b[:, :, None], (SGU_HEADS, SGU_CHUNK, HEAD_DIM))
    cw = jnp.pad(conv_w, ((0, 32 - CONV_K), (0, 0)))
    vec = lambda a: a.reshape(1, -1)
    const2 = lambda i: (0, 0)
    const3 = lambda i: (0, 0, 0)
    return pl.pallas_call(
        functools.partial(_mix_ab_kernel, tq=tq),
        grid=(s // tq,),
        in_specs=[
            pl.BlockSpec((tq, w), lambda i: (i, 0)),
            pl.BlockSpec((tq, w), lambda i: (i, 1)),
            pl.BlockSpec((tq, w), lambda i: (i, 2)),
            pl.BlockSpec((tq, w), lambda i: (i, 3)),
            pl.BlockSpec((CONV_HALO, w), lambda i: (jnp.maximum(i * hb - 1, 0), 2)),
            pl.BlockSpec((CONV_HALO, w), lambda i: (jnp.maximum(i * hb - 1, 0), 3)),
            pl.BlockSpec((1, w), const2),
            pl.BlockSpec((SGU_HEADS, SGU_CHUNK, SGU_CHUNK), const3),
            pl.BlockSpec((SGU_HEADS, SGU_CHUNK, HEAD_DIM), const3),
            pl.BlockSpec((32, w), const2),
            pl.BlockSpec((1, w), const2),
            pl.BlockSpec((1, w), const2),
            pl.BlockSpec((1, w), const2),
            pl.BlockSpec((1, w), const2),
            pl.BlockSpec((1, w), const2),
        ],
        out_specs=pl.BlockSpec((tq, 2 * w), lambda i: (i, 0)),
        out_shape=jax.ShapeDtypeStruct((s, 2 * w), BF16),
        scratch_shapes=[pltpu.VMEM((tq + CONV_HALO, w), F32), pltpu.VMEM((tq, w), F32),
                        pltpu.VMEM((8, SGU_CHUNK + CONV_HALO, LANES), F32)],
        compiler_params=pltpu.CompilerParams(
            dimension_semantics=("parallel",), vmem_limit_bytes=VMEM_LIMIT),
        name="mix_ab",
    )(proj, proj, proj, proj, proj, proj, vec(sgu_norm), sgu_w, sb, cw, vec(conv_b),
      vec(ln_g), vec(ln_b), vec(on_a), vec(on_b))


def _attn_kernel(q_ref, k_ref, v_ref, qg_ref, kg_ref, o_ref,
                 qn_ref, kn_ref, vv_ref, ob_ref, lb_ref):
    t = pl.program_id(1)
    tl = ATTN_TILE

    @pl.when(t == 0)
    def _():
        kn_ref[0:tl, :] = jnp.zeros((tl, HEAD_DIM), F32)
        vv_ref[0:tl, :] = jnp.zeros((tl, HEAD_DIM), F32)

    @pl.when(t > 0)
    def _():
        kn_ref[0:tl, :] = kn_ref[tl:, :]
        vv_ref[0:tl, :] = vv_ref[tl:, :]

    kn_ref[tl:, :] = _rms(k_ref[...].astype(F32), kg_ref[...])
    vv_ref[tl:, :] = v_ref[...].astype(F32)
    qn_ref[...] = _rms(q_ref[...].astype(F32), qg_ref[...]) * (HEAD_DIM ** -0.5)

    qi = lax.broadcasted_iota(jnp.int32, (QBLK, 2 * QBLK), 0)
    kj = lax.broadcasted_iota(jnp.int32, (QBLK, 2 * QBLK), 1)
    band = (kj >= qi) & (kj <= qi + QBLK)
    band_first = band & (kj >= jnp.where(t > 0, 0, QBLK))

    for g, dil in enumerate(DILATIONS):
        for r in range(dil):
            for b in range(tl // (QBLK * dil)):
                base = b * QBLK * dil + r
                qb = qn_ref[pl.ds(base, QBLK, stride=dil), :].astype(BF16)
                kb = kn_ref[pl.ds(tl + base - QBLK * dil, 2 * QBLK, stride=dil), :].astype(BF16)
                vb = vv_ref[pl.ds(tl + base - QBLK * dil, 2 * QBLK, stride=dil), :].astype(BF16)
                sc = lax.dot_general(qb, kb, (((1,), (1,)), ((), ())),
                                     preferred_element_type=F32)
                sc = jnp.where(band_first if b == 0 else band, sc, NEG)
                m = jnp.max(sc, axis=-1, keepdims=True)
                p = jnp.exp(sc - m)
                l = jnp.sum(p, axis=-1, keepdims=True)
                ob = jnp.dot(p.astype(BF16), vb, preferred_element_type=F32) / l
                ob_ref[g, pl.ds(base, QBLK, stride=dil), :] = ob
                lb_ref[g, pl.ds(base, QBLK, stride=dil), :] = jnp.broadcast_to(
                    m + jnp.log(l), (QBLK, HEAD_DIM))

    l0, l1, l2 = lb_ref[0], lb_ref[1], lb_ref[2]
    mx = jnp.maximum(jnp.maximum(l0, l1), l2)
    w0, w1, w2 = jnp.exp(l0 - mx), jnp.exp(l1 - mx), jnp.exp(l2 - mx)
    y = (w0 * ob_ref[0] + w1 * ob_ref[1] + w2 * ob_ref[2]) / (w0 + w1 + w2)
    o_ref[...] = y.astype(o_ref.dtype)


def _attention(proj, q_g, k_g):
    s = proj.shape[0]
    tl = ATTN_TILE
    qc = (2 * SGU_WIDTH + 2 * CONV_CH) // HEAD_DIM
    kc = qc + ATTN_HEADS
    vc = kc + ATTN_HEADS
    return pl.pallas_call(
        _attn_kernel,
        grid=(ATTN_HEADS, s // tl),
        in_specs=[
            pl.BlockSpec((tl, HEAD_DIM), lambda h, t: (t, qc + h)),
            pl.BlockSpec((tl, HEAD_DIM), lambda h, t: (t, kc + h)),
            pl.BlockSpec((tl, HEAD_DIM), lambda h, t: (t, vc + h)),
            pl.BlockSpec((1, HEAD_DIM), lambda h, t: (0, 0)),
            pl.BlockSpec((1, HEAD_DIM), lambda h, t: (0, 0)),
        ],
        out_specs=pl.BlockSpec((tl, HEAD_DIM), lambda h, t: (t, h)),
        out_shape=jax.ShapeDtypeStruct((s, ATTN_WIDTH), BF16),
        scratch_shapes=[
            pltpu.VMEM((tl, HEAD_DIM), F32),
            pltpu.VMEM((2 * tl, HEAD_DIM), F32),
            pltpu.VMEM((2 * tl, HEAD_DIM), F32),
            pltpu.VMEM((len(DILATIONS), tl, HEAD_DIM), F32),
            pltpu.VMEM((len(DILATIONS), tl, HEAD_DIM), F32),
        ],
        compiler_params=pltpu.CompilerParams(
            dimension_semantics=("parallel", "arbitrary"), vmem_limit_bytes=VMEM_LIMIT),
        name="attention",
    )(proj, proj, proj, q_g.reshape(1, HEAD_DIM), k_g.reshape(1, HEAD_DIM))


def _split_bf16(x):
    hi = x.astype(BF16)
    lo = (x - hi.astype(F32)).astype(BF16)
    return hi, lo


def _pack_halves(x):
    c = x.shape[1] // 2
    lo = lax.bitcast_convert_type(x[:, :c].astype(BF16).astype(F32), U32)
    hi = lax.bitcast_convert_type(x[:, c:].astype(BF16).astype(F32), U32)
    return (lo >> 16) | hi


def _unpack_halves(u):
    lo = lax.bitcast_convert_type(u << 16, F32)
    hi = lax.bitcast_convert_type(u & jnp.uint32(0xFFFF0000), F32)
    return lo, hi


def _out_proj_kernel(yab_ref, yc_ref, onc_ref, h_ref, w_ref, n2_ref, rwh_ref, rwl_ref, rb_ref,
                     h1_ref, xp_ref, route_ref, routet_ref, cnt_ref, tri_ref, run_ref):
    tm = h_ref.shape[0]
    ts = tri_ref.shape[0]

    @pl.when(pl.program_id(0) == 0)
    def _():
        r = lax.broadcasted_iota(jnp.int32, (ts, ts), 0)
        c = lax.broadcasted_iota(jnp.int32, (ts, ts), 1)
        tri_ref[...] = jnp.where(c < r, 1.0, 0.0).astype(BF16)
        run_ref[...] = jnp.zeros(run_ref.shape, F32)

    run = run_ref[...]
    for p in range(tm // ts):
        run = _out_proj_rows(slice(p * ts, (p + 1) * ts), run, yab_ref, yc_ref, onc_ref, h_ref,
                             w_ref, n2_ref, rwh_ref, rwl_ref, rb_ref, h1_ref, xp_ref, route_ref,
                             routet_ref, tri_ref)
    run_ref[...] = run
    cnt_ref[...] = run


def _out_proj_rows(rows, run, yab_ref, yc_ref, onc_ref, h_ref, w_ref, n2_ref, rwh_ref, rwl_ref,
                   rb_ref, h1_ref, xp_ref, route_ref, routet_ref, tri_ref):
    half = yab_ref.shape[1]
    ycn = _rms(yc_ref[rows, :].astype(F32), onc_ref[...]).astype(BF16)
    acc = jnp.dot(yab_ref[rows, :], w_ref[0:half, :], preferred_element_type=F32)
    acc = acc + jnp.dot(ycn, w_ref[half:, :], preferred_element_type=F32)
    h1 = h_ref[rows, :] + acc
    h1_ref[rows, :] = h1
    xn = _rms(h1, n2_ref[...])
    xp_ref[rows, :] = _pack_halves(xn)

    xh, xl = _split_bf16(xn)
    lg = (jnp.dot(xh, rwh_ref[...], preferred_element_type=F32)
          + jnp.dot(xl, rwh_ref[...], preferred_element_type=F32)
          + jnp.dot(xh, rwl_ref[...], preferred_element_type=F32)) + rb_ref[...]

    lane = lax.broadcasted_iota(jnp.int32, lg.shape, 1)
    big = jnp.int32(LANES)
    gl = jnp.where(lane < N_GROUPS, lg, NEG)
    gmax = jnp.max(gl, axis=-1, keepdims=True)
    grp = jnp.min(jnp.where(gl == gmax, lane, big), axis=-1, keepdims=True)
    gate_g = 1.0 / jnp.sum(jnp.exp(gl - gmax), axis=-1, keepdims=True)
    lo = N_GROUPS + grp * EXPERTS_PER_GROUP
    el = jnp.where((lane >= lo) & (lane < lo + EXPERTS_PER_GROUP), lg, NEG)
    v1 = jnp.max(el, axis=-1, keepdims=True)
    i1 = jnp.min(jnp.where(el == v1, lane, big), axis=-1, keepdims=True)
    el2 = jnp.where(lane == i1, NEG, el)
    v2 = jnp.max(el2, axis=-1, keepdims=True)
    i2 = jnp.min(jnp.where(el2 == v2, lane, big), axis=-1, keepdims=True)
    e2 = jnp.exp(v2 - v1)
    g0 = gate_g / (1.0 + e2)
    g1 = gate_g * e2 / (1.0 + e2)
    oh0 = lane == i1 - N_GROUPS
    oh1 = lane == i2 - N_GROUPS
    cnt = jnp.where(oh0 | oh1, 1.0, 0.0)
    before = jnp.dot(tri_ref[...], cnt.astype(BF16), preferred_element_type=F32) + run
    r0 = jnp.sum(jnp.where(oh0, before, 0.0), axis=-1, keepdims=True)
    r1 = jnp.sum(jnp.where(oh1, before, 0.0), axis=-1, keepdims=True)

    route = (i1 - N_GROUPS).astype(F32)
    for k, val in enumerate(((i2 - N_GROUPS).astype(F32), g0, g1, r0, r1), start=1):
        route = jnp.where(lane == k, val, route)
    route = jnp.where(lane < ROUTE_LANES, route, 0.0)
    route_ref[rows, :] = route
    routet_ref[:, rows] = route.T[0:8, :]
    return run + jnp.sum(cnt, axis=0, keepdims=True)


def _out_proj(y_ab, y_c, on_c, h, w_bf16, norm2, rw, rb, tm=512, sub_rows=512):
    s, d = h.shape
    half = y_ab.shape[1]
    rw_pad = jnp.pad(rw, ((0, 0), (0, LANES - ROUTE_COLS)))
    rwh, rwl = _split_bf16(rw_pad)
    rb_pad = jnp.pad(rb, (0, LANES - ROUTE_COLS)).reshape(1, LANES)
    const = lambda i: (0, 0)
    return pl.pallas_call(
        _out_proj_kernel,
        grid=(s // tm,),
        in_specs=[
            pl.BlockSpec((tm, half), lambda i: (i, 0)),
            pl.BlockSpec((tm, half), lambda i: (i, 0)),
            pl.BlockSpec((1, half), const),
            pl.BlockSpec((tm, d), lambda i: (i, 0)),
            pl.BlockSpec((d, d), const, pipeline_mode=pl.Buffered(1)),
            pl.BlockSpec((1, d), const),
            pl.BlockSpec((d, LANES), const),
            pl.BlockSpec((d, LANES), const),
            pl.BlockSpec((1, LANES), const),
        ],
        out_specs=[
            pl.BlockSpec((tm, d), lambda i: (i, 0)),
            pl.BlockSpec((tm, d // 2), lambda i: (i, 0)),
            pl.BlockSpec((tm, LANES), lambda i: (i, 0)),
            pl.BlockSpec((8, tm), lambda i: (0, i)),
            pl.BlockSpec((1, LANES), const),
        ],
        out_shape=[
            jax.ShapeDtypeStruct((s, d), F32),
            jax.ShapeDtypeStruct((s, d // 2), U32),
            jax.ShapeDtypeStruct((s, LANES), F32),
            jax.ShapeDtypeStruct((8, s), F32),
            jax.ShapeDtypeStruct((1, LANES), F32),
        ],
        scratch_shapes=[pltpu.VMEM((sub_rows, sub_rows), BF16), pltpu.VMEM((1, LANES), F32)],
        compiler_params=pltpu.CompilerParams(
            dimension_semantics=("arbitrary",), vmem_limit_bytes=VMEM_LIMIT),
        name="out_proj",
    )(y_ab, y_c, on_c.reshape(1, half), h, w_bf16, norm2.reshape(1, d), rwh, rwl, rb_pad)


def _route_plan(route_t, cnt, t):
    nblk = t * TOP_K // MOE_BLK + N_EXPERTS
    ids = jnp.arange(N_EXPERTS, dtype=jnp.int32)
    e = route_t[0:TOP_K].astype(jnp.int32)
    rank = route_t[4:4 + TOP_K].astype(jnp.int32)
    counts = cnt[0, :N_EXPERTS].astype(jnp.int32)
    pcounts = (counts + MOE_BLK - 1) // MOE_BLK * MOE_BLK
    pends = jnp.cumsum(pcounts)
    pstarts = pends - pcounts
    dest = (jnp.sum(jnp.where(e[:, :, None] == ids, pstarts, 0), axis=-1) + rank).reshape(-1)
    blk = jnp.arange(nblk, dtype=jnp.int32)
    blk_e = jnp.minimum(jnp.sum((blk[:, None] * MOE_BLK >= pends[None, :]).astype(jnp.int32), axis=1),
                        N_EXPERTS - 1)
    n_act = pends[-1] // MOE_BLK
    mine = blk_e[:, None] == ids[None, :]
    first = (blk == 0) | (blk_e != jnp.roll(blk_e, 1))
    seg_par = (jnp.cumsum(first.astype(jnp.int32)) - 1) % 2
    nxt_blk = jnp.sum(jnp.where(mine, pends, 0), axis=1) // MOE_BLK
    nxt_e = jnp.where(nxt_blk < n_act,
                      jnp.sum(jnp.where(nxt_blk[:, None] == blk[None, :], blk_e[None, :], 0), axis=1),
                      -1)
    i32 = lambda a: a.astype(jnp.int32)
    return (i32(dest), i32(pstarts + counts), i32(pcounts - counts), i32(blk_e), i32(first),
            i32(seg_par), i32(nxt_e), i32(n_act).reshape(1), nblk)


def _dispatch_kernel(dest_ref, pad_start_ref, pad_n_ref, n_act_ref, xp_ref, xs_ref,
                     buf, zbuf, sem, zsem, *, tm, nsteps, nblk):
    i = pl.program_id(0)
    slot = i % 2

    def wait_slot(sl):
        for _ in range(TOP_K):
            pltpu.make_async_copy(buf.at[sl], xs_ref.at[pl.ds(0, tm)], sem.at[sl]).wait()

    def for_padding_copies(fn):
        sub = 8
        for e in range(N_EXPERTS):
            n = pad_n_ref[e]
            start = pad_start_ref[e]
            head = jnp.minimum((sub - (start & (sub - 1))) & (sub - 1), n)
            for j in range(sub - 1):
                @pl.when(j < head)
                def _(row=start + j):
                    fn(pltpu.make_async_copy(zbuf.at[pl.ds(0, 1)], xs_ref.at[pl.ds(row, 1)], zsem))
            n = n - head
            start = start + head
            p = MOE_BLK // 2
            while p >= sub:
                @pl.when((n & p) != 0)
                def _(start=start, p=p):
                    fn(pltpu.make_async_copy(zbuf.at[pl.ds(0, p)],
                                             xs_ref.at[pl.ds(pl.multiple_of(start, sub), p)], zsem))
                start = start + (n & p)
                p //= 2
        for j in range(nblk - nsteps * tm * TOP_K // MOE_BLK):
            blk = n_act_ref[0] + j

            @pl.when(blk < nblk)
            def _(blk=blk):
                fn(pltpu.make_async_copy(zbuf, xs_ref.at[pl.ds(blk * MOE_BLK, MOE_BLK)], zsem))

    @pl.when(i == 0)
    def _():
        zbuf[...] = jnp.zeros(zbuf.shape, zbuf.dtype)
        for_padding_copies(lambda cp: cp.start())

    @pl.when(i >= 2)
    def _():
        wait_slot(slot)

    buf[slot] = xp_ref[...]
    for rr in range(tm):
        for k in range(TOP_K):
            dst = dest_ref[k * tm * nsteps + i * tm + rr]
            pltpu.make_async_copy(buf.at[slot, pl.ds(rr, 1)], xs_ref.at[pl.ds(dst, 1)],
                                  sem.at[slot]).start()

    @pl.when(i == nsteps - 1)
    def _():
        wait_slot(slot)
        if nsteps > 1:
            wait_slot(1 - slot)
        for_padding_copies(lambda cp: cp.wait())


def _dispatch(dest, pad_start, pad_n, n_act, xp, nblk, tm=256):
    t, c = xp.shape
    nsteps = t // tm
    return pl.pallas_call(
        functools.partial(_dispatch_kernel, tm=tm, nsteps=nsteps, nblk=nblk),
        grid_spec=pltpu.PrefetchScalarGridSpec(
            num_scalar_prefetch=4,
            grid=(nsteps,),
            in_specs=[pl.BlockSpec((tm, c), lambda i, *_: (i, 0))],
            out_specs=pl.BlockSpec(memory_space=pl.ANY),
            scratch_shapes=[pltpu.VMEM((2, tm, c), U32), pltpu.VMEM((MOE_BLK, c), U32),
                            pltpu.SemaphoreType.DMA((2,)), pltpu.SemaphoreType.DMA],
        ),
        out_shape=jax.ShapeDtypeStruct((nblk * MOE_BLK, c), U32),
        compiler_params=pltpu.CompilerParams(
            dimension_semantics=("arbitrary",), vmem_limit_bytes=VMEM_LIMIT),
        name="dispatch",
    )(dest, pad_start, pad_n, n_act, xp)


def _experts_kernel(blk_e_ref, first_ref, par_ref, nxt_ref, n_act_ref,
                    xs_ref, wg_hbm, wu_hbm, wd_hbm, ys_ref,
                    wg_f, wu_f, wd_f, wg_b, wu_b, wd_b, wsem, *, layer):
    i = pl.program_id(0)
    c = xs_ref.shape[1]

    def weight_copies(e, s):
        return (pltpu.make_async_copy(wg_hbm.at[layer, e], wg_f.at[s], wsem.at[s]),
                pltpu.make_async_copy(wu_hbm.at[layer, e], wu_f.at[s], wsem.at[s]),
                pltpu.make_async_copy(wd_hbm.at[layer, e], wd_f.at[s], wsem.at[s]))

    @pl.when(i >= n_act_ref[0])
    def _():
        ys_ref[...] = jnp.zeros(ys_ref.shape, ys_ref.dtype)

    @pl.when(i < n_act_ref[0])
    def _():
        s = par_ref[i]

        @pl.when(first_ref[i] == 1)
        def _():
            @pl.when(i == 0)
            def _():
                for cp in weight_copies(blk_e_ref[0], 0):
                    cp.start()

            for cp in weight_copies(0, s):
                cp.wait()
            wg_b[...] = wg_f[s].astype(BF16)
            wu_b[...] = wu_f[s].astype(BF16)
            wd_b[...] = wd_f[s].astype(BF16)

            @pl.when(nxt_ref[i] >= 0)
            def _():
                for cp in weight_copies(nxt_ref[i], 1 - s):
                    cp.start()

        x_lo, x_hi = _unpack_halves(xs_ref[...])
        x_lo = x_lo.astype(BF16)
        x_hi = x_hi.astype(BF16)
        hg = (jnp.dot(x_lo, wg_b[0:c, :], preferred_element_type=F32)
              + jnp.dot(x_hi, wg_b[c:, :], preferred_element_type=F32))
        hu = (jnp.dot(x_lo, wu_b[0:c, :], preferred_element_type=F32)
              + jnp.dot(x_hi, wu_b[c:, :], preferred_element_type=F32))
        hb = (jax.nn.silu(hg) * hu).astype(BF16)
        ys_ref[...] = _pack_halves(jnp.dot(hb, wd_b[...], preferred_element_type=F32))


def _experts(xs, blk_e, first, seg_par, nxt_e, n_act, nblk, wg, wu, wd, layer):
    n_slots, c = xs.shape
    _, _, d, de = wg.shape
    active_rows = lambda i, be, fi, pa, nx, na: (jnp.minimum(i, na[0] - 1), 0)
    return pl.pallas_call(
        functools.partial(_experts_kernel, layer=layer),
        grid_spec=pltpu.PrefetchScalarGridSpec(
            num_scalar_prefetch=5,
            grid=(nblk,),
            in_specs=[
                pl.BlockSpec((MOE_BLK, c), active_rows),
                pl.BlockSpec(memory_space=pl.ANY),
                pl.BlockSpec(memory_space=pl.ANY),
                pl.BlockSpec(memory_space=pl.ANY),
            ],
            out_specs=pl.BlockSpec((MOE_BLK, c), lambda i, *_: (i, 0)),
            scratch_shapes=[
                pltpu.VMEM((2, d, de), F32),
                pltpu.VMEM((2, d, de), F32),
                pltpu.VMEM((2, de, d), F32),
                pltpu.VMEM((d, de), BF16),
                pltpu.VMEM((d, de), BF16),
                pltpu.VMEM((de, d), BF16),
                pltpu.SemaphoreType.DMA((2,)),
            ],
        ),
        out_shape=jax.ShapeDtypeStruct((n_slots, c), U32),
        compiler_params=pltpu.CompilerParams(
            dimension_semantics=("arbitrary",), vmem_limit_bytes=VMEM_LIMIT),
        name="experts",
    )(blk_e, first, seg_par, nxt_e, n_act, xs, wg, wu, wd)


def _combine_kernel(dest_ref, h_ref, route_ref, ys_hbm, *rest, tm, nsteps, emit_next):
    if emit_next:
        gn_ref, o_ref, xn_ref, gbuf, sem = rest
    else:
        o_ref, gbuf, sem = rest
    i = pl.program_id(0)

    @pl.when(i < nsteps)
    def _():
        for rr in range(tm):
            for k in range(TOP_K):
                src = dest_ref[k * tm * nsteps + i * tm + rr]
                pltpu.make_async_copy(ys_hbm.at[pl.ds(src, 1)],
                                      gbuf.at[i % 2, pl.ds(k * tm + rr, 1)],
                                      sem.at[i % 2]).start()

    @pl.when(i >= 1)
    def _():
        sl = (i - 1) % 2
        for k in range(TOP_K):
            pltpu.make_async_copy(ys_hbm.at[pl.ds(0, tm)], gbuf.at[sl, pl.ds(k * tm, tm)],
                                  sem.at[sl]).wait()
        r = route_ref[...]
        lane = lax.broadcasted_iota(jnp.int32, r.shape, 1)
        g0 = jnp.sum(jnp.where(lane == 2, r, 0.0), axis=-1, keepdims=True)
        g1 = jnp.sum(jnp.where(lane == 3, r, 0.0), axis=-1, keepdims=True)
        a_lo, a_hi = _unpack_halves(gbuf[sl, 0:tm])
        b_lo, b_hi = _unpack_halves(gbuf[sl, tm:2 * tm])
        c = a_lo.shape[1]
        h_lo = h_ref[:, 0:c] + (a_lo * g0 + b_lo * g1)
        h_hi = h_ref[:, c:] + (a_hi * g0 + b_hi * g1)
        o_ref[:, 0:c] = h_lo
        o_ref[:, c:] = h_hi
        if emit_next:
            ms = (jnp.sum(h_lo * h_lo, axis=-1, keepdims=True)
                  + jnp.sum(h_hi * h_hi, axis=-1, keepdims=True)) / (2 * c)
            inv = lax.rsqrt(ms + EPS)
            xn_ref[:, 0:c] = (h_lo * inv * gn_ref[:, 0:c]).astype(BF16)
            xn_ref[:, c:] = (h_hi * inv * gn_ref[:, c:]).astype(BF16)


def _combine(dest, h1, route, ys, next_gain=None, tm=256):
    t, d = h1.shape
    c = ys.shape[1]
    nsteps = t // tm
    emit_next = next_gain is not None
    prev = lambda i, dst: (jnp.maximum(i - 1, 0), 0)
    in_specs = [
        pl.BlockSpec((tm, d), prev),
        pl.BlockSpec((tm, LANES), prev),
        pl.BlockSpec(memory_space=pl.ANY),
    ]
    out_specs = [pl.BlockSpec((tm, d), prev)]
    out_shape = [jax.ShapeDtypeStruct((t, d), F32)]
    args = [dest, h1, route, ys]
    if emit_next:
        in_specs.append(pl.BlockSpec((1, d), lambda i, dst: (0, 0)))
        out_specs.append(pl.BlockSpec((tm, d), prev))
        out_shape.append(jax.ShapeDtypeStruct((t, d), BF16))
        args.append(next_gain.reshape(1, d))
    return pl.pallas_call(
        functools.partial(_combine_kernel, tm=tm, nsteps=nsteps, emit_next=emit_next),
        grid_spec=pltpu.PrefetchScalarGridSpec(
            num_scalar_prefetch=1,
            grid=(nsteps + 1,),
            in_specs=in_specs,
            out_specs=out_specs,
            scratch_shapes=[pltpu.VMEM((2, TOP_K * tm, c), U32), pltpu.SemaphoreType.DMA((2,))],
        ),
        out_shape=out_shape,
        compiler_params=pltpu.CompilerParams(
            dimension_semantics=("arbitrary",), vmem_limit_bytes=VMEM_LIMIT),
        name="combine",
    )(*args)


def kernel(x, norm1, w_in, sgu_norm, sgu_w, sgu_b, conv_w, conv_b, conv_ln_g, conv_ln_b,
           q_norm, k_norm, out_norm, w_out, norm2, w_router_group, b_router_group,
           w_router_expert, b_router_expert, w_expert_gate, w_expert_up, w_expert_down):
    b, s, d = x.shape
    depth = norm1.shape[0]
    t = b * s
    h = x.reshape(t, d)
    a_end, b_end = SGU_WIDTH, SGU_WIDTH + CONV_CH
    x_in = h
    for l in range(depth):
        proj = _in_proj(x_in, norm1[l], w_in, l)
        y_ab = _mix_ab(proj, sgu_norm[l], sgu_w[l], sgu_b[l], conv_w[l], conv_b[l],
                       conv_ln_g[l], conv_ln_b[l], out_norm[l, :a_end], out_norm[l, a_end:b_end])
        y_c = _attention(proj, q_norm[l], k_norm[l])
        rw = jnp.concatenate([w_router_group[l], w_router_expert[l]], axis=1)
        rb = jnp.concatenate([b_router_group[l], b_router_expert[l]], axis=0)
        h1, xp, route, route_t, cnt = _out_proj(y_ab, y_c, out_norm[l, b_end:], h,
                                                w_out[l].astype(BF16), norm2[l], rw, rb)
        (dest, pad_start, pad_n, blk_e, first, seg_par, nxt_e, n_act,
         nblk) = _route_plan(route_t, cnt, t)
        xs = _dispatch(dest, pad_start, pad_n, n_act, xp, nblk)
        ys = _experts(xs, blk_e, first, seg_par, nxt_e, n_act, nblk,
                      w_expert_gate, w_expert_up, w_expert_down, l)
        if l + 1 < depth:
            h, x_in = _combine(dest, h1, route, ys, next_gain=norm1[l + 1])
        else:
            (h,) = _combine(dest, h1, route, ys)
    return h.reshape(b, s, d)
```

```python
import functools

import jax
import jax.numpy as jnp
from jax import lax
from jax.experimental import pallas as pl
from jax.experimental.pallas import tpu as pltpu

F32 = jnp.float32
BF16 = jnp.bfloat16
U32 = jnp.uint32

D_MODEL = 2048
SGU_WIDTH = 512
SGU_HEADS = 4
SGU_CHUNK = 128
CONV_CH = 512
CONV_K = 31
ATTN_WIDTH = 1024
ATTN_HEADS = 8
HEAD_DIM = 128
DILATIONS = (1, 4, 16)
QBLK = 128
D_IN = 2 * SGU_WIDTH + 2 * CONV_CH + 3 * ATTN_WIDTH
N_GROUPS = 4
EXPERTS_PER_GROUP = 8
N_EXPERTS = 32
TOP_K = 2
D_EXPERT = 512
EPS = 1e-6

LANES = 128
VMEM_LIMIT = 56 * 1024 * 1024
NEG = -1e30

ATTN_TILE = QBLK * max(DILATIONS)
CONV_HALO = 32
MOE_BLK = 256
ROUTE_COLS = N_GROUPS + N_EXPERTS
ROUTE_LANES = 6


def _rms(x, g):
    return x * lax.rsqrt(jnp.mean(x * x, axis=-1, keepdims=True) + EPS) * g


def _in_proj_norm_kernel(x_ref, g_ref, w_ref, o_ref, xn_ref):
    @pl.when(pl.program_id(1) == 0)
    def _():
        xn_ref[...] = _rms(x_ref[...], g_ref[...]).astype(BF16)

    o_ref[...] = jnp.dot(xn_ref[...], w_ref[...].astype(BF16),
                         preferred_element_type=F32).astype(o_ref.dtype)


def _in_proj_kernel(xn_ref, w_ref, o_ref):
    o_ref[...] = jnp.dot(xn_ref[...], w_ref[...].astype(BF16),
                         preferred_element_type=F32).astype(o_ref.dtype)


def _in_proj(x, g, w_in, layer, tn=1024):
    s, d = x.shape
    n = w_in.shape[2]
    prenormed = x.dtype == BF16
    tm = 2048 if prenormed else 1024
    x_spec = pl.BlockSpec((tm, d), lambda i, j: (i, 0))
    w_spec = pl.BlockSpec((None, d, tn), lambda i, j: (layer, 0, j))
    common = dict(
        grid=(s // tm, n // tn),
        out_specs=pl.BlockSpec((tm, tn), lambda i, j: (i, j)),
        out_shape=jax.ShapeDtypeStruct((s, n), BF16),
        compiler_params=pltpu.CompilerParams(
            dimension_semantics=("parallel", "arbitrary"), vmem_limit_bytes=VMEM_LIMIT),
        name="in_proj",
    )
    if prenormed:
        return pl.pallas_call(_in_proj_kernel, in_specs=[x_spec, w_spec], **common)(x, w_in)
    return pl.pallas_call(
        _in_proj_norm_kernel,
        in_specs=[x_spec, pl.BlockSpec((1, d), lambda i, j: (0, 0)), w_spec],
        scratch_shapes=[pltpu.VMEM((tm, d), BF16)],
        **common,
    )(x, g.reshape(1, d), w_in)


def _mix_ab_kernel(u_ref, v_ref, a_ref, gt_ref, ah_ref, gh_ref,
                   sgn_ref, sw_ref, sb_ref, cw_ref, cb_ref, lng_ref, lnb_ref,
                   ona_ref, onb_ref, o_ref, zext_ref, conv_ref, stage_ref, *, tq):
    u = jax.nn.gelu(u_ref[...].astype(F32))
    v = _rms(jax.nn.gelu(v_ref[...].astype(F32)), sgn_ref[...]).astype(BF16)
    row = lax.broadcasted_iota(jnp.int32, (SGU_CHUNK, SGU_CHUNK), 0)
    col = lax.broadcasted_iota(jnp.int32, (SGU_CHUNK, SGU_CHUNK), 1)
    causal = col <= row
    wm = [jnp.where(causal, sw_ref[hh], 0.0).astype(BF16) for hh in range(SGU_HEADS)]
    for c in range(tq // SGU_CHUNK):
        rows = slice(c * SGU_CHUNK, (c + 1) * SGU_CHUNK)
        zs = []
        for hh in range(SGU_HEADS):
            cols = slice(hh * HEAD_DIM, (hh + 1) * HEAD_DIM)
            zs.append(jnp.dot(wm[hh], v[rows, cols], preferred_element_type=F32) + sb_ref[hh])
        ya = u[rows, :] * jnp.concatenate(zs, axis=1)
        o_ref[rows, 0:SGU_WIDTH] = _rms(ya, ona_ref[...]).astype(o_ref.dtype)

    first = pl.program_id(0) == 0
    zh = ah_ref[...].astype(F32) * jax.nn.sigmoid(gh_ref[...].astype(F32))
    zext_ref[0:CONV_HALO, :] = jnp.where(first, 0.0, zh)
    zext_ref[CONV_HALO:, :] = a_ref[...].astype(F32) * jax.nn.sigmoid(gt_ref[...].astype(F32))
    off0 = CONV_HALO - (CONV_K - 1)
    sub = 8
    rc = 128
    for j in range(CONV_CH // LANES):
        cols = slice(j * LANES, (j + 1) * LANES)
        for c in range(tq // rc):
            acc = jnp.zeros((rc, LANES), F32) + cb_ref[:, cols]
            for b in range(sub):
                taps = [(a, sub * a + b - off0) for a in range(CONV_HALO // sub + 1)
                        if 0 <= sub * a + b - off0 < CONV_K]
                rows = rc + sub * taps[-1][0]
                stage_ref[b, 0:rows, :] = zext_ref[pl.ds(c * rc + b, rows), cols]
                for a, k in taps:
                    acc = acc + stage_ref[b, sub * a:sub * a + rc, :] * cw_ref[k:k + 1, cols]
            conv_ref[c * rc:(c + 1) * rc, cols] = acc
    rc = 32
    for c in range(tq // rc):
        acc = conv_ref[c * rc:(c + 1) * rc, :]
        mu = jnp.mean(acc, axis=-1, keepdims=True)
        xc = acc - mu
        y = xc * lax.rsqrt(jnp.mean(xc * xc, axis=-1, keepdims=True) + EPS)
        y = jax.nn.silu(y * lng_ref[...] + lnb_ref[...])
        o_ref[c * rc:(c + 1) * rc, SGU_WIDTH:] = _rms(y, onb_ref[...]).astype(o_ref.dtype)


def _mix_ab(proj, sgu_norm, sgu_w, sgu_b, conv_w, conv_b, ln_g, ln_b, on_a, on_b, tq=512):
    s = proj.shape[0]
    w = SGU_WIDTH
    hb = tq // CONV_HALO
    sb = jnp.broadcast_to(sgu_b[:, :, None], (SGU_HEADS, SGU_CHUNK, HEAD_DIM))
    cw = jnp.pad(conv_w, ((0, 32 - CONV_K), (0, 0)))
    vec = lambda a: a.reshape(1, -1)
    const2 = lambda i: (0, 0)
    const3 = lambda i: (0, 0, 0)
    return pl.pallas_call(
        functools.partial(_mix_ab_kernel, tq=tq),
        grid=(s // tq,),
        in_specs=[
            pl.BlockSpec((tq, w), lambda i: (i, 0)),
            pl.BlockSpec((tq, w), lambda i: (i, 1)),
            pl.BlockSpec((tq, w), lambda i: (i, 2)),
            pl.BlockSpec((tq, w), lambda i: (i, 3)),
            pl.BlockSpec((CONV_HALO, w), lambda i: (jnp.maximum(i * hb - 1, 0), 2)),
            pl.BlockSpec((CONV_HALO, w), lambda i: (jnp.maximum(i * hb - 1, 0), 3)),
            pl.BlockSpec((1, w), const2),
            pl.BlockSpec((SGU_HEADS, SGU_CHUNK, SGU_CHUNK), const3),
            pl.BlockSpec((SGU_HEADS, SGU_CHUNK, HEAD_DIM), const3),
            pl.BlockSpec((32, w), const2),
            pl.BlockSpec((1, w), const2),
            pl.BlockSpec((1, w), const2),
            pl.BlockSpec((1, w), const2),
            pl.BlockSpec((1, w), const2),
            pl.BlockSpec((1, w), const2),
        ],
        out_specs=pl.BlockSpec((tq, 2 * w), lambda i: (i, 0)),
        out_shape=jax.ShapeDtypeStruct((s, 2 * w), BF16),
        scratch_shapes=[pltpu.VMEM((tq + CONV_HALO, w), F32), pltpu.VMEM((tq, w), F32),
                        pltpu.VMEM((8, SGU_CHUNK + CONV_HALO, LANES), F32)],
        compiler_params=pltpu.CompilerParams(
            dimension_semantics=("parallel",), vmem_limit_bytes=VMEM_LIMIT),
        name="mix_ab",
    )(proj, proj, proj, proj, proj, proj, vec(sgu_norm), sgu_w, sb, cw, vec(conv_b),
      vec(ln_g), vec(ln_b), vec(on_a), vec(on_b))


ATTN_PH = 4
ATTN_SL = ATTN_TILE // ATTN_PH


def _attn_kernel(q_ref, k_ref, v_ref, qg_ref, kg_ref, o_ref,
                 qn_ref, kn_ref, vv_ref, q4_ref, k4_ref, v4_ref, ob_ref, lb_ref, st_ref, yn_ref):
    t = pl.program_id(1)
    tl, ph, sl = ATTN_TILE, ATTN_PH, ATTN_SL

    @pl.when(t == 0)
    def _():
        kn_ref[0:tl, :] = jnp.zeros((tl, HEAD_DIM), F32)
        vv_ref[0:tl, :] = jnp.zeros((tl, HEAD_DIM), F32)
        k4_ref[:, 0:sl, :] = jnp.zeros((ph, sl, HEAD_DIM), F32)
        v4_ref[:, 0:sl, :] = jnp.zeros((ph, sl, HEAD_DIM), F32)

    @pl.when(t > 0)
    def _():
        kn_ref[0:tl, :] = kn_ref[tl:, :]
        vv_ref[0:tl, :] = vv_ref[tl:, :]
        k4_ref[:, 0:sl, :] = k4_ref[:, sl:, :]
        v4_ref[:, 0:sl, :] = v4_ref[:, sl:, :]

    kn_ref[tl:, :] = _rms(k_ref[...].astype(F32), kg_ref[...])
    vv_ref[tl:, :] = v_ref[...].astype(F32)
    qn_ref[...] = _rms(q_ref[...].astype(F32), qg_ref[...]) * (HEAD_DIM ** -0.5)
    for r in range(ph):
        q4_ref[r] = qn_ref[pl.ds(r, sl, stride=ph), :]
        k4_ref[r, sl:, :] = kn_ref[pl.ds(tl + r, sl, stride=ph), :]
        v4_ref[r, sl:, :] = vv_ref[pl.ds(tl + r, sl, stride=ph), :]

    qi = lax.broadcasted_iota(jnp.int32, (QBLK, 2 * QBLK), 0)
    kj = lax.broadcasted_iota(jnp.int32, (QBLK, 2 * QBLK), 1)
    band = (kj >= qi) & (kj <= qi + QBLK)
    band_first = band & (kj >= jnp.where(t > 0, 0, QBLK))

    def attend(qb, kb, vb, first):
        sc = lax.dot_general(qb.astype(BF16), kb.astype(BF16), (((1,), (1,)), ((), ())),
                             preferred_element_type=F32)
        sc = jnp.where(band_first if first else band, sc, NEG)
        m = jnp.max(sc, axis=-1, keepdims=True)
        p = jnp.exp(sc - m)
        l = jnp.sum(p, axis=-1, keepdims=True)
        ob = jnp.dot(p.astype(BF16), vb.astype(BF16), preferred_element_type=F32) / l
        return ob, jnp.broadcast_to(m + jnp.log(l), (QBLK, HEAD_DIM))

    for b in range(tl // QBLK):
        ob, lse = attend(qn_ref[b * QBLK:(b + 1) * QBLK, :],
                         kn_ref[tl + (b - 1) * QBLK:tl + (b + 1) * QBLK, :],
                         vv_ref[tl + (b - 1) * QBLK:tl + (b + 1) * QBLK, :], b == 0)
        s = b % 2
        st_ref[s, 0] = ob
        st_ref[s, 1] = lse
        n = QBLK // ph
        for r in range(ph):
            ob_ref[0, r, b * n:(b + 1) * n, :] = st_ref[s, 0, pl.ds(r, n, stride=ph), :]
            lb_ref[0, r, b * n:(b + 1) * n, :] = st_ref[s, 1, pl.ds(r, n, stride=ph), :]

    for r in range(ph):
        for b in range(sl // QBLK):
            ob, lse = attend(q4_ref[r, b * QBLK:(b + 1) * QBLK, :],
                             k4_ref[r, sl + (b - 1) * QBLK:sl + (b + 1) * QBLK, :],
                             v4_ref[r, sl + (b - 1) * QBLK:sl + (b + 1) * QBLK, :], b == 0)
            ob_ref[1, r, b * QBLK:(b + 1) * QBLK, :] = ob
            lb_ref[1, r, b * QBLK:(b + 1) * QBLK, :] = lse
        for a in range(ph):
            ob, lse = attend(q4_ref[r, pl.ds(a, QBLK, stride=ph), :],
                             k4_ref[r, pl.ds(a, 2 * QBLK, stride=ph), :],
                             v4_ref[r, pl.ds(a, 2 * QBLK, stride=ph), :], True)
            ob_ref[2, r, pl.ds(a, QBLK, stride=ph), :] = ob
            lb_ref[2, r, pl.ds(a, QBLK, stride=ph), :] = lse

    for r in range(ph):
        for c in range(sl // QBLK):
            rows = slice(c * QBLK, (c + 1) * QBLK)
            l0, l1, l2 = lb_ref[0, r, rows, :], lb_ref[1, r, rows, :], lb_ref[2, r, rows, :]
            mx = jnp.maximum(jnp.maximum(l0, l1), l2)
            w0, w1, w2 = jnp.exp(l0 - mx), jnp.exp(l1 - mx), jnp.exp(l2 - mx)
            y = (w0 * ob_ref[0, r, rows, :] + w1 * ob_ref[1, r, rows, :]
                 + w2 * ob_ref[2, r, rows, :]) / (w0 + w1 + w2)
            yn_ref[pl.ds(c * QBLK * ph + r, QBLK, stride=ph), :] = y
    o_ref[...] = yn_ref[...].astype(o_ref.dtype)


def _attention(proj, q_g, k_g):
    s = proj.shape[0]
    tl = ATTN_TILE
    qc = (2 * SGU_WIDTH + 2 * CONV_CH) // HEAD_DIM
    kc = qc + ATTN_HEADS
    vc = kc + ATTN_HEADS
    return pl.pallas_call(
        _attn_kernel,
        grid=(ATTN_HEADS, s // tl),
        in_specs=[
            pl.BlockSpec((tl, HEAD_DIM), lambda h, t: (t, qc + h)),
            pl.BlockSpec((tl, HEAD_DIM), lambda h, t: (t, kc + h)),
            pl.BlockSpec((tl, HEAD_DIM), lambda h, t: (t, vc + h)),
            pl.BlockSpec((1, HEAD_DIM), lambda h, t: (0, 0)),
            pl.BlockSpec((1, HEAD_DIM), lambda h, t: (0, 0)),
        ],
        out_specs=pl.BlockSpec((tl, HEAD_DIM), lambda h, t: (t, h)),
        out_shape=jax.ShapeDtypeStruct((s, ATTN_WIDTH), BF16),
        scratch_shapes=[
            pltpu.VMEM((tl, HEAD_DIM), F32),
            pltpu.VMEM((2 * tl, HEAD_DIM), F32),
            pltpu.VMEM((2 * tl, HEAD_DIM), F32),
            pltpu.VMEM((ATTN_PH, ATTN_SL, HEAD_DIM), F32),
            pltpu.VMEM((ATTN_PH, 2 * ATTN_SL, HEAD_DIM), F32),
            pltpu.VMEM((ATTN_PH, 2 * ATTN_SL, HEAD_DIM), F32),
            pltpu.VMEM((len(DILATIONS), ATTN_PH, ATTN_SL, HEAD_DIM), F32),
            pltpu.VMEM((len(DILATIONS), ATTN_PH, ATTN_SL, HEAD_DIM), F32),
            pltpu.VMEM((2, 2, QBLK, HEAD_DIM), F32),
            pltpu.VMEM((tl, HEAD_DIM), F32),
        ],
        compiler_params=pltpu.CompilerParams(
            dimension_semantics=("parallel", "arbitrary"), vmem_limit_bytes=VMEM_LIMIT),
        name="attention",
    )(proj, proj, proj, q_g.reshape(1, HEAD_DIM), k_g.reshape(1, HEAD_DIM))


def _split_bf16(x):
    hi = x.astype(BF16)
    lo = (x - hi.astype(F32)).astype(BF16)
    return hi, lo


def _pack_halves(x):
    c = x.shape[1] // 2
    lo = lax.bitcast_convert_type(x[:, :c].astype(BF16).astype(F32), U32)
    hi = lax.bitcast_convert_type(x[:, c:].astype(BF16).astype(F32), U32)
    return (lo >> 16) | hi


def _unpack_halves(u):
    lo = lax.bitcast_convert_type(u << 16, F32)
    hi = lax.bitcast_convert_type(u & jnp.uint32(0xFFFF0000), F32)
    return lo, hi


def _out_proj_kernel(yab_ref, yc_ref, onc_ref, h_ref, w_ref, n2_ref, rwh_ref, rwl_ref, rb_ref,
                     h1_ref, xp_ref, route_ref, routet_ref, cnt_ref, tri_ref, run_ref):
    tm = h_ref.shape[0]
    ts = tri_ref.shape[0]

    @pl.when(pl.program_id(0) == 0)
    def _():
        r = lax.broadcasted_iota(jnp.int32, (ts, ts), 0)
        c = lax.broadcasted_iota(jnp.int32, (ts, ts), 1)
        tri_ref[...] = jnp.where(c < r, 1.0, 0.0).astype(BF16)
        run_ref[...] = jnp.zeros(run_ref.shape, F32)

    half = yab_ref.shape[1]
    subs = [slice(p * ts, (p + 1) * ts) for p in range(tm // ts)]
    for rows in subs:
        ycn = _rms(yc_ref[rows, :].astype(F32), onc_ref[...]).astype(BF16)
        acc = jnp.dot(yab_ref[rows, :], w_ref[0:half, :], preferred_element_type=F32)
        acc = acc + jnp.dot(ycn, w_ref[half:, :], preferred_element_type=F32)
        h1_ref[rows, :] = h_ref[rows, :] + acc
    run = run_ref[...]
    for rows in subs:
        run = _route_rows(rows, run, n2_ref, rwh_ref, rwl_ref, rb_ref, h1_ref, xp_ref, route_ref,
                          routet_ref, tri_ref)
    run_ref[...] = run
    cnt_ref[...] = run


def _route_rows(rows, run, n2_ref, rwh_ref, rwl_ref, rb_ref, h1_ref, xp_ref, route_ref,
                routet_ref, tri_ref):
    xn = _rms(h1_ref[rows, :], n2_ref[...])
    xp_ref[rows, :] = _pack_halves(xn)

    xh, xl = _split_bf16(xn)
    lg = (jnp.dot(xh, rwh_ref[...], preferred_element_type=F32)
          + jnp.dot(xl, rwh_ref[...], preferred_element_type=F32)
          + jnp.dot(xh, rwl_ref[...], preferred_element_type=F32)) + rb_ref[...]

    lane = lax.broadcasted_iota(jnp.int32, lg.shape, 1)
    big = jnp.int32(LANES)
    gl = jnp.where(lane < N_GROUPS, lg, NEG)
    gmax = jnp.max(gl, axis=-1, keepdims=True)
    grp = jnp.min(jnp.where(gl == gmax, lane, big), axis=-1, keepdims=True)
    gate_g = 1.0 / jnp.sum(jnp.exp(gl - gmax), axis=-1, keepdims=True)
    lo = N_GROUPS + grp * EXPERTS_PER_GROUP
    el = jnp.where((lane >= lo) & (lane < lo + EXPERTS_PER_GROUP), lg, NEG)
    v1 = jnp.max(el, axis=-1, keepdims=True)
    i1 = jnp.min(jnp.where(el == v1, lane, big), axis=-1, keepdims=True)
    el2 = jnp.where(lane == i1, NEG, el)
    v2 = jnp.max(el2, axis=-1, keepdims=True)
    i2 = jnp.min(jnp.where(el2 == v2, lane, big), axis=-1, keepdims=True)
    e2 = jnp.exp(v2 - v1)
    g0 = gate_g / (1.0 + e2)
    g1 = gate_g * e2 / (1.0 + e2)
    oh0 = lane == i1 - N_GROUPS
    oh1 = lane == i2 - N_GROUPS
    cnt = jnp.where(oh0 | oh1, 1.0, 0.0)
    before = jnp.dot(tri_ref[...], cnt.astype(BF16), preferred_element_type=F32) + run
    r0 = jnp.sum(jnp.where(oh0, before, 0.0), axis=-1, keepdims=True)
    r1 = jnp.sum(jnp.where(oh1, before, 0.0), axis=-1, keepdims=True)

    route = (i1 - N_GROUPS).astype(F32)
    for k, val in enumerate(((i2 - N_GROUPS).astype(F32), g0, g1, r0, r1), start=1):
        route = jnp.where(lane == k, val, route)
    route = jnp.where(lane < ROUTE_LANES, route, 0.0)
    route_ref[rows, :] = route
    routet_ref[:, rows] = route.T[0:8, :]
    return run + jnp.sum(cnt, axis=0, keepdims=True)


def _out_proj(y_ab, y_c, on_c, h, w_bf16, norm2, rw, rb, tm=512, sub_rows=512):
    s, d = h.shape
    half = y_ab.shape[1]
    rw_pad = jnp.pad(rw, ((0, 0), (0, LANES - ROUTE_COLS)))
    rwh, rwl = _split_bf16(rw_pad)
    rb_pad = jnp.pad(rb, (0, LANES - ROUTE_COLS)).reshape(1, LANES)
    const = lambda i: (0, 0)
    return pl.pallas_call(
        _out_proj_kernel,
        grid=(s // tm,),
        in_specs=[
            pl.BlockSpec((tm, half), lambda i: (i, 0)),
            pl.BlockSpec((tm, half), lambda i: (i, 0)),
            pl.BlockSpec((1, half), const),
            pl.BlockSpec((tm, d), lambda i: (i, 0)),
            pl.BlockSpec((d, d), const, pipeline_mode=pl.Buffered(1)),
            pl.BlockSpec((1, d), const),
            pl.BlockSpec((d, LANES), const),
            pl.BlockSpec((d, LANES), const),
            pl.BlockSpec((1, LANES), const),
        ],
        out_specs=[
            pl.BlockSpec((tm, d), lambda i: (i, 0)),
            pl.BlockSpec((tm, d // 2), lambda i: (i, 0)),
            pl.BlockSpec((tm, LANES), lambda i: (i, 0)),
            pl.BlockSpec((8, tm), lambda i: (0, i)),
            pl.BlockSpec((1, LANES), const),
        ],
        out_shape=[
            jax.ShapeDtypeStruct((s, d), F32),
            jax.ShapeDtypeStruct((s, d // 2), U32),
            jax.ShapeDtypeStruct((s, LANES), F32),
            jax.ShapeDtypeStruct((8, s), F32),
            jax.ShapeDtypeStruct((1, LANES), F32),
        ],
        scratch_shapes=[pltpu.VMEM((sub_rows, sub_rows), BF16), pltpu.VMEM((1, LANES), F32)],
        compiler_params=pltpu.CompilerParams(
            dimension_semantics=("arbitrary",), vmem_limit_bytes=VMEM_LIMIT),
        name="out_proj",
    )(y_ab, y_c, on_c.reshape(1, half), h, w_bf16, norm2.reshape(1, d), rwh, rwl, rb_pad)


def _route_plan(route_t, cnt, t):
    nblk = t * TOP_K // MOE_BLK + N_EXPERTS
    ids = jnp.arange(N_EXPERTS, dtype=jnp.int32)
    e = route_t[0:TOP_K].astype(jnp.int32)
    rank = route_t[4:4 + TOP_K].astype(jnp.int32)
    counts = cnt[0, :N_EXPERTS].astype(jnp.int32)
    pcounts = (counts + MOE_BLK - 1) // MOE_BLK * MOE_BLK
    pends = jnp.cumsum(pcounts)
    pstarts = pends - pcounts
    dest = (jnp.sum(jnp.where(e[:, :, None] == ids, pstarts, 0), axis=-1) + rank).reshape(-1)
    blk = jnp.arange(nblk, dtype=jnp.int32)
    blk_e = jnp.minimum(jnp.sum((blk[:, None] * MOE_BLK >= pends[None, :]).astype(jnp.int32), axis=1),
                        N_EXPERTS - 1)
    n_act = pends[-1] // MOE_BLK
    mine = blk_e[:, None] == ids[None, :]
    first = (blk == 0) | (blk_e != jnp.roll(blk_e, 1))
    seg_par = (jnp.cumsum(first.astype(jnp.int32)) - 1) % 2
    nxt_blk = jnp.sum(jnp.where(mine, pends, 0), axis=1) // MOE_BLK
    nxt_e = jnp.where(nxt_blk < n_act,
                      jnp.sum(jnp.where(nxt_blk[:, None] == blk[None, :], blk_e[None, :], 0), axis=1),
                      -1)
    i32 = lambda a: a.astype(jnp.int32)
    return (i32(dest), i32(pstarts + counts), i32(pcounts - counts), i32(blk_e), i32(first),
            i32(seg_par), i32(nxt_e), i32(n_act).reshape(1), nblk)


def _dispatch_kernel(dest_ref, pad_start_ref, pad_n_ref, n_act_ref, xp_ref, xs_ref,
                     buf, zbuf, sem, zsem, *, tm, nsteps, nblk):
    i = pl.program_id(0)
    slot = i % 2

    def wait_slot(sl):
        for _ in range(TOP_K):
            pltpu.make_async_copy(buf.at[sl], xs_ref.at[pl.ds(0, tm)], sem.at[sl]).wait()

    def for_padding_copies(fn):
        sub = 8
        for e in range(N_EXPERTS):
            n = pad_n_ref[e]
            start = pad_start_ref[e]
            head = jnp.minimum((sub - (start & (sub - 1))) & (sub - 1), n)
            for j in range(sub - 1):
                @pl.when(j < head)
                def _(row=start + j):
                    fn(pltpu.make_async_copy(zbuf.at[pl.ds(0, 1)], xs_ref.at[pl.ds(row, 1)], zsem))
            n = n - head
            start = start + head
            p = MOE_BLK // 2
            while p >= sub:
                @pl.when((n & p) != 0)
                def _(start=start, p=p):
                    fn(pltpu.make_async_copy(zbuf.at[pl.ds(0, p)],
                                             xs_ref.at[pl.ds(pl.multiple_of(start, sub), p)], zsem))
                start = start + (n & p)
                p //= 2
        for j in range(nblk - nsteps * tm * TOP_K // MOE_BLK):
            blk = n_act_ref[0] + j

            @pl.when(blk < nblk)
            def _(blk=blk):
                fn(pltpu.make_async_copy(zbuf, xs_ref.at[pl.ds(blk * MOE_BLK, MOE_BLK)], zsem))

    @pl.when(i == 0)
    def _():
        zbuf[...] = jnp.zeros(zbuf.shape, zbuf.dtype)
        for_padding_copies(lambda cp: cp.start())

    @pl.when(i >= 2)
    def _():
        wait_slot(slot)

    buf[slot] = xp_ref[...]
    for rr in range(tm):
        for k in range(TOP_K):
            dst = dest_ref[k * tm * nsteps + i * tm + rr]
            pltpu.make_async_copy(buf.at[slot, pl.ds(rr, 1)], xs_ref.at[pl.ds(dst, 1)],
                                  sem.at[slot]).start()

    @pl.when(i == nsteps - 1)
    def _():
        wait_slot(slot)
        if nsteps > 1:
            wait_slot(1 - slot)
        for_padding_copies(lambda cp: cp.wait())


def _dispatch(dest, pad_start, pad_n, n_act, xp, nblk, tm=256):
    t, c = xp.shape
    nsteps = t // tm
    return pl.pallas_call(
        functools.partial(_dispatch_kernel, tm=tm, nsteps=nsteps, nblk=nblk),
        grid_spec=pltpu.PrefetchScalarGridSpec(
            num_scalar_prefetch=4,
            grid=(nsteps,),
            in_specs=[pl.BlockSpec((tm, c), lambda i, *_: (i, 0))],
            out_specs=pl.BlockSpec(memory_space=pl.ANY),
            scratch_shapes=[pltpu.VMEM((2, tm, c), U32), pltpu.VMEM((MOE_BLK, c), U32),
                            pltpu.SemaphoreType.DMA((2,)), pltpu.SemaphoreType.DMA],
        ),
        out_shape=jax.ShapeDtypeStruct((nblk * MOE_BLK, c), U32),
        compiler_params=pltpu.CompilerParams(
            dimension_semantics=("arbitrary",), vmem_limit_bytes=VMEM_LIMIT),
        name="dispatch",
    )(dest, pad_start, pad_n, n_act, xp)


def _experts_kernel(blk_e_ref, first_ref, par_ref, nxt_ref, n_act_ref,
                    xs_ref, wg_hbm, wu_hbm, wd_hbm, ys_ref,
                    wg_f, wu_f, wd_f, wg_b, wu_b, wd_b, wsem, *, layer):
    i = pl.program_id(0)
    c = xs_ref.shape[1]

    def weight_copies(e, s):
        return (pltpu.make_async_copy(wg_hbm.at[layer, e], wg_f.at[s], wsem.at[s]),
                pltpu.make_async_copy(wu_hbm.at[layer, e], wu_f.at[s], wsem.at[s]),
                pltpu.make_async_copy(wd_hbm.at[layer, e], wd_f.at[s], wsem.at[s]))

    @pl.when(i >= n_act_ref[0])
    def _():
        ys_ref[...] = jnp.zeros(ys_ref.shape, ys_ref.dtype)

    @pl.when(i < n_act_ref[0])
    def _():
        s = par_ref[i]

        @pl.when(first_ref[i] == 1)
        def _():
            @pl.when(i == 0)
            def _():
                for cp in weight_copies(blk_e_ref[0], 0):
                    cp.start()

            for cp in weight_copies(0, s):
                cp.wait()
            wg_b[...] = wg_f[s].astype(BF16)
            wu_b[...] = wu_f[s].astype(BF16)
            wd_b[...] = wd_f[s].astype(BF16)

            @pl.when(nxt_ref[i] >= 0)
            def _():
                for cp in weight_copies(nxt_ref[i], 1 - s):
                    cp.start()

        x_lo, x_hi = _unpack_halves(xs_ref[...])
        x_lo = x_lo.astype(BF16)
        x_hi = x_hi.astype(BF16)
        hg = (jnp.dot(x_lo, wg_b[0:c, :], preferred_element_type=F32)
              + jnp.dot(x_hi, wg_b[c:, :], preferred_element_type=F32))
        hu = (jnp.dot(x_lo, wu_b[0:c, :], preferred_element_type=F32)
              + jnp.dot(x_hi, wu_b[c:, :], preferred_element_type=F32))
        hb = (jax.nn.silu(hg) * hu).astype(BF16)
        ys_ref[...] = _pack_halves(jnp.dot(hb, wd_b[...], preferred_element_type=F32))


def _experts(xs, blk_e, first, seg_par, nxt_e, n_act, nblk, wg, wu, wd, layer):
    n_slots, c = xs.shape
    _, _, d, de = wg.shape
    active_rows = lambda i, be, fi, pa, nx, na: (jnp.minimum(i, na[0] - 1), 0)
    return pl.pallas_call(
        functools.partial(_experts_kernel, layer=layer),
        grid_spec=pltpu.PrefetchScalarGridSpec(
            num_scalar_prefetch=5,
            grid=(nblk,),
            in_specs=[
                pl.BlockSpec((MOE_BLK, c), active_rows),
                pl.BlockSpec(memory_space=pl.ANY),
                pl.BlockSpec(memory_space=pl.ANY),
                pl.BlockSpec(memory_space=pl.ANY),
            ],
            out_specs=pl.BlockSpec((MOE_BLK, c), lambda i, *_: (i, 0)),
            scratch_shapes=[
                pltpu.VMEM((2, d, de), F32),
                pltpu.VMEM((2, d, de), F32),
                pltpu.VMEM((2, de, d), F32),
                pltpu.VMEM((d, de), BF16),
                pltpu.VMEM((d, de), BF16),
                pltpu.VMEM((de, d), BF16),
                pltpu.SemaphoreType.DMA((2,)),
            ],
        ),
        out_shape=jax.ShapeDtypeStruct((n_slots, c), U32),
        compiler_params=pltpu.CompilerParams(
            dimension_semantics=("arbitrary",), vmem_limit_bytes=VMEM_LIMIT),
        name="experts",
    )(blk_e, first, seg_par, nxt_e, n_act, xs, wg, wu, wd)


def _combine_kernel(dest_ref, h_ref, route_ref, ys_hbm, *rest, tm, nsteps, emit_next):
    if emit_next:
        gn_ref, o_ref, xn_ref, gbuf, sem = rest
    else:
        o_ref, gbuf, sem = rest
    i = pl.program_id(0)

    @pl.when(i < nsteps)
    def _():
        for rr in range(tm):
            for k in range(TOP_K):
                src = dest_ref[k * tm * nsteps + i * tm + rr]
                pltpu.make_async_copy(ys_hbm.at[pl.ds(src, 1)],
                                      gbuf.at[i % 2, pl.ds(k * tm + rr, 1)],
                                      sem.at[i % 2]).start()

    @pl.when(i >= 1)
    def _():
        sl = (i - 1) % 2
        for k in range(TOP_K):
            pltpu.make_async_copy(ys_hbm.at[pl.ds(0, tm)], gbuf.at[sl, pl.ds(k * tm, tm)],
                                  sem.at[sl]).wait()
        r = route_ref[...]
        lane = lax.broadcasted_iota(jnp.int32, r.shape, 1)
        g0 = jnp.sum(jnp.where(lane == 2, r, 0.0), axis=-1, keepdims=True)
        g1 = jnp.sum(jnp.where(lane == 3, r, 0.0), axis=-1, keepdims=True)
        a_lo, a_hi = _unpack_halves(gbuf[sl, 0:tm])
        b_lo, b_hi = _unpack_halves(gbuf[sl, tm:2 * tm])
        c = a_lo.shape[1]
        h_lo = h_ref[:, 0:c] + (a_lo * g0 + b_lo * g1)
        h_hi = h_ref[:, c:] + (a_hi * g0 + b_hi * g1)
        o_ref[:, 0:c] = h_lo
        o_ref[:, c:] = h_hi
        if emit_next:
            ms = (jnp.sum(h_lo * h_lo, axis=-1, keepdims=True)
                  + jnp.sum(h_hi * h_hi, axis=-1, keepdims=True)) / (2 * c)
            inv = lax.rsqrt(ms + EPS)
            xn_ref[:, 0:c] = (h_lo * inv * gn_ref[:, 0:c]).astype(BF16)
            xn_ref[:, c:] = (h_hi * inv * gn_ref[:, c:]).astype(BF16)


def _combine(dest, h1, route, ys, next_gain=None, tm=256):
    t, d = h1.shape
    c = ys.shape[1]
    nsteps = t // tm
    emit_next = next_gain is not None
    prev = lambda i, dst: (jnp.maximum(i - 1, 0), 0)
    in_specs = [
        pl.BlockSpec((tm, d), prev),
        pl.BlockSpec((tm, LANES), prev),
        pl.BlockSpec(memory_space=pl.ANY),
    ]
    out_specs = [pl.BlockSpec((tm, d), prev)]
    out_shape = [jax.ShapeDtypeStruct((t, d), F32)]
    args = [dest, h1, route, ys]
    if emit_next:
        in_specs.append(pl.BlockSpec((1, d), lambda i, dst: (0, 0)))
        out_specs.append(pl.BlockSpec((tm, d), prev))
        out_shape.append(jax.ShapeDtypeStruct((t, d), BF16))
        args.append(next_gain.reshape(1, d))
    return pl.pallas_call(
        functools.partial(_combine_kernel, tm=tm, nsteps=nsteps, emit_next=emit_next),
        grid_spec=pltpu.PrefetchScalarGridSpec(
            num_scalar_prefetch=1,
            grid=(nsteps + 1,),
            in_specs=in_specs,
            out_specs=out_specs,
            scratch_shapes=[pltpu.VMEM((2, TOP_K * tm, c), U32), pltpu.SemaphoreType.DMA((2,))],
        ),
        out_shape=out_shape,
        compiler_params=pltpu.CompilerParams(
            dimension_semantics=("arbitrary",), vmem_limit_bytes=VMEM_LIMIT),
        name="combine",
    )(*args)


def kernel(x, norm1, w_in, sgu_norm, sgu_w, sgu_b, conv_w, conv_b, conv_ln_g, conv_ln_b,
           q_norm, k_norm, out_norm, w_out, norm2, w_router_group, b_router_group,
           w_router_expert, b_router_expert, w_expert_gate, w_expert_up, w_expert_down):
    b, s, d = x.shape
    depth = norm1.shape[0]
    t = b * s
    h = x.reshape(t, d)
    a_end, b_end = SGU_WIDTH, SGU_WIDTH + CONV_CH
    x_in = h
    for l in range(depth):
        proj = _in_proj(x_in, norm1[l], w_in, l)
        y_ab = _mix_ab(proj, sgu_norm[l], sgu_w[l], sgu_b[l], conv_w[l], conv_b[l],
                       conv_ln_g[l], conv_ln_b[l], out_norm[l, :a_end], out_norm[l, a_end:b_end])
        y_c = _attention(proj, q_norm[l], k_norm[l])
        rw = jnp.concatenate([w_router_group[l], w_router_expert[l]], axis=1)
        rb = jnp.concatenate([b_router_group[l], b_router_expert[l]], axis=0)
        h1, xp, route, route_t, cnt = _out_proj(y_ab, y_c, out_norm[l, b_end:], h,
                                                w_out[l].astype(BF16), norm2[l], rw, rb)
        (dest, pad_start, pad_n, blk_e, first, seg_par, nxt_e, n_act,
         nblk) = _route_plan(route_t, cnt, t)
        xs = _dispatch(dest, pad_start, pad_n, n_act, xp, nblk)
        ys = _experts(xs, blk_e, first, seg_par, nxt_e, n_act, nblk,
                      w_expert_gate, w_expert_up, w_expert_down, l)
        if l + 1 < depth:
            h, x_in = _combine(dest, h1, route, ys, next_gain=norm1[l + 1])
        else:
            (h,) = _combine(dest, h1, route, ys)
    return h.reshape(b, s, d)
```

```python
import functools

import jax
import jax.numpy as jnp
from jax import lax
from jax.experimental import pallas as pl
from jax.experimental.pallas import tpu as pltpu

F32 = jnp.float32
BF16 = jnp.bfloat16
U32 = jnp.uint32

D_MODEL = 2048
SGU_WIDTH = 512
SGU_HEADS = 4
SGU_CHUNK = 128
CONV_CH = 512
CONV_K = 31
ATTN_WIDTH = 1024
ATTN_HEADS = 8
HEAD_DIM = 128
DILATIONS = (1, 4, 16)
QBLK = 128
D_IN = 2 * SGU_WIDTH + 2 * CONV_CH + 3 * ATTN_WIDTH
N_GROUPS = 4
EXPERTS_PER_GROUP = 8
N_EXPERTS = 32
TOP_K = 2
D_EXPERT = 512
EPS = 1e-6

LANES = 128
VMEM_LIMIT = 56 * 1024 * 1024
NEG = -1e30

ATTN_TILE = QBLK * max(DILATIONS)
CONV_HALO = 32
MOE_BLK = 256
ROUTE_COLS = N_GROUPS + N_EXPERTS
ROUTE_LANES = 6


def _rms(x, g):
    return x * lax.rsqrt(jnp.mean(x * x, axis=-1, keepdims=True) + EPS) * g


def _in_proj_norm_kernel(x_ref, g_ref, w_ref, o_ref, xn_ref):
    @pl.when(pl.program_id(1) == 0)
    def _():
        xn_ref[...] = _rms(x_ref[...], g_ref[...]).astype(BF16)

    o_ref[...] = jnp.dot(xn_ref[...], w_ref[...].astype(BF16),
                         preferred_element_type=F32).astype(o_ref.dtype)


def _in_proj_kernel(xn_ref, w_ref, o_ref):
    o_ref[...] = jnp.dot(xn_ref[...], w_ref[...].astype(BF16),
                         preferred_element_type=F32).astype(o_ref.dtype)


def _in_proj(x, g, w_in, layer, tn=1024):
    s, d = x.shape
    n = w_in.shape[2]
    prenormed = x.dtype == BF16
    tm = 2048 if prenormed else 1024
    x_spec = pl.BlockSpec((tm, d), lambda i, j: (i, 0))
    w_spec = pl.BlockSpec((None, d, tn), lambda i, j: (layer, 0, j))
    common = dict(
        grid=(s // tm, n // tn),
        out_specs=pl.BlockSpec((tm, tn), lambda i, j: (i, j)),
        out_shape=jax.ShapeDtypeStruct((s, n), BF16),
        compiler_params=pltpu.CompilerParams(
            dimension_semantics=("parallel", "arbitrary"), vmem_limit_bytes=VMEM_LIMIT),
        name="in_proj",
    )
    if prenormed:
        return pl.pallas_call(_in_proj_kernel, in_specs=[x_spec, w_spec], **common)(x, w_in)
    return pl.pallas_call(
        _in_proj_norm_kernel,
        in_specs=[x_spec, pl.BlockSpec((1, d), lambda i, j: (0, 0)), w_spec],
        scratch_shapes=[pltpu.VMEM((tm, d), BF16)],
        **common,
    )(x, g.reshape(1, d), w_in)


def _mix_ab_kernel(u_ref, v_ref, a_ref, gt_ref, ah_ref, gh_ref,
                   sgn_ref, sw_ref, sb_ref, cw_ref, cb_ref, lng_ref, lnb_ref,
                   ona_ref, onb_ref, o_ref, zext_ref, conv_ref, stage_ref, *, tq):
    u = jax.nn.gelu(u_ref[...].astype(F32))
    v = _rms(jax.nn.gelu(v_ref[...].astype(F32)), sgn_ref[...]).astype(BF16)
    row = lax.broadcasted_iota(jnp.int32, (SGU_CHUNK, SGU_CHUNK), 0)
    col = lax.broadcasted_iota(jnp.int32, (SGU_CHUNK, SGU_CHUNK), 1)
    causal = col <= row
    wm = [jnp.where(causal, sw_ref[hh], 0.0).astype(BF16) for hh in range(SGU_HEADS)]
    for c in range(tq // SGU_CHUNK):
        rows = slice(c * SGU_CHUNK, (c + 1) * SGU_CHUNK)
        zs = []
        for hh in range(SGU_HEADS):
            cols = slice(hh * HEAD_DIM, (hh + 1) * HEAD_DIM)
            zs.append(jnp.dot(wm[hh], v[rows, cols], preferred_element_type=F32) + sb_ref[hh])
        ya = u[rows, :] * jnp.concatenate(zs, axis=1)
        o_ref[rows, 0:SGU_WIDTH] = _rms(ya, ona_ref[...]).astype(o_ref.dtype)

    first = pl.program_id(0) == 0
    zh = ah_ref[...].astype(F32) * jax.nn.sigmoid(gh_ref[...].astype(F32))
    zext_ref[0:CONV_HALO, :] = jnp.where(first, 0.0, zh)
    zext_ref[CONV_HALO:, :] = a_ref[...].astype(F32) * jax.nn.sigmoid(gt_ref[...].astype(F32))
    off0 = CONV_HALO - (CONV_K - 1)
    sub = 8
    rc = 128
    for j in range(CONV_CH // LANES):
        cols = slice(j * LANES, (j + 1) * LANES)
        for c in range(tq // rc):
            acc = jnp.zeros((rc, LANES), F32) + cb_ref[:, cols]
            for b in range(sub):
                taps = [(a, sub * a + b - off0) for a in range(CONV_HALO // sub + 1)
                        if 0 <= sub * a + b - off0 < CONV_K]
                rows = rc + sub * taps[-1][0]
                stage_ref[b, 0:rows, :] = zext_ref[pl.ds(c * rc + b, rows), cols]
                for a, k in taps:
                    acc = acc + stage_ref[b, sub * a:sub * a + rc, :] * cw_ref[k:k + 1, cols]
            conv_ref[c * rc:(c + 1) * rc, cols] = acc
    rc = 32
    for c in range(tq // rc):
        acc = conv_ref[c * rc:(c + 1) * rc, :]
        mu = jnp.mean(acc, axis=-1, keepdims=True)
        xc = acc - mu
        y = xc * lax.rsqrt(jnp.mean(xc * xc, axis=-1, keepdims=True) + EPS)
        y = jax.nn.silu(y * lng_ref[...] + lnb_ref[...])
        o_ref[c * rc:(c + 1) * rc, SGU_WIDTH:] = _rms(y, onb_ref[...]).astype(o_ref.dtype)


def _mix_ab(proj, sgu_norm, sgu_w, sgu_b, conv_w, conv_b, ln_g, ln_b, on_a, on_b, tq=512):
    s = proj.shape[0]
    w = SGU_WIDTH
    hb = tq // CONV_HALO
    sb = jnp.broadcast_to(sgu_b[:, :, None], (SGU_HEADS, SGU_CHUNK, HEAD_DIM))
    cw = jnp.pad(conv_w, ((0, 32 - CONV_K), (0, 0)))
    vec = lambda a: a.reshape(1, -1)
    const2 = lambda i: (0, 0)
    const3 = lambda i: (0, 0, 0)
    return pl.pallas_call(
        functools.partial(_mix_ab_kernel, tq=tq),
        grid=(s // tq,),
        in_specs=[
            pl.BlockSpec((tq, w), lambda i: (i, 0)),
            pl.BlockSpec((tq, w), lambda i: (i, 1)),
            pl.BlockSpec((tq, w), lambda i: (i, 2)),
            pl.BlockSpec((tq, w), lambda i: (i, 3)),
            pl.BlockSpec((CONV_HALO, w), lambda i: (jnp.maximum(i * hb - 1, 0), 2)),
            pl.BlockSpec((CONV_HALO, w), lambda i: (jnp.maximum(i * hb - 1, 0), 3)),
            pl.BlockSpec((1, w), const2),
            pl.BlockSpec((SGU_HEADS, SGU_CHUNK, SGU_CHUNK), const3),
            pl.BlockSpec((SGU_HEADS, SGU_CHUNK, HEAD_DIM), const3),
            pl.BlockSpec((32, w), const2),
            pl.BlockSpec((1, w), const2),
            pl.BlockSpec((1, w), const2),
            pl.BlockSpec((1, w), const2),
            pl.BlockSpec((1, w), const2),
            pl.BlockSpec((1, w), const2),
        ],
        out_specs=pl.BlockSpec((tq, 2 * w), lambda i: (i, 0)),
        out_shape=jax.ShapeDtypeStruct((s, 2 * w), BF16),
        scratch_shapes=[pltpu.VMEM((tq + CONV_HALO, w), F32), pltpu.VMEM((tq, w), F32),
                        pltpu.VMEM((8, SGU_CHUNK + CONV_HALO, LANES), F32)],
        compiler_params=pltpu.CompilerParams(
            dimension_semantics=("parallel",), vmem_limit_bytes=VMEM_LIMIT),
        name="mix_ab",
    )(proj, proj, proj, proj, proj, proj, vec(sgu_norm), sgu_w, sb, cw, vec(conv_b),
      vec(ln_g), vec(ln_b), vec(on_a), vec(on_b))


ATTN_PH = 4
ATTN_SL = ATTN_TILE // ATTN_PH


def _attn_kernel(q_ref, k_ref, v_ref, qg_ref, kg_ref, o_ref,
                 qn_ref, kn_ref, vv_ref, q4_ref, k4_ref, v4_ref, ob_ref, lb_ref, st_ref, yn_ref,
                 bias_ref):
    t = pl.program_id(1)
    tl, ph, sl = ATTN_TILE, ATTN_PH, ATTN_SL

    @pl.when(t == 0)
    def _():
        kn_ref[0:tl, :] = jnp.zeros((tl, HEAD_DIM), F32)
        vv_ref[0:tl, :] = jnp.zeros((tl, HEAD_DIM), F32)
        k4_ref[:, 0:sl, :] = jnp.zeros((ph, sl, HEAD_DIM), F32)
        v4_ref[:, 0:sl, :] = jnp.zeros((ph, sl, HEAD_DIM), F32)

    @pl.when(t > 0)
    def _():
        kn_ref[0:tl, :] = kn_ref[tl:, :]
        vv_ref[0:tl, :] = vv_ref[tl:, :]
        k4_ref[:, 0:sl, :] = k4_ref[:, sl:, :]
        v4_ref[:, 0:sl, :] = v4_ref[:, sl:, :]

    kn_ref[tl:, :] = _rms(k_ref[...].astype(F32), kg_ref[...])
    vv_ref[tl:, :] = v_ref[...].astype(F32)
    qn_ref[...] = _rms(q_ref[...].astype(F32), qg_ref[...]) * (HEAD_DIM ** -0.5)
    for r in range(ph):
        q4_ref[r] = qn_ref[pl.ds(r, sl, stride=ph), :]
        k4_ref[r, sl:, :] = kn_ref[pl.ds(tl + r, sl, stride=ph), :]
        v4_ref[r, sl:, :] = vv_ref[pl.ds(tl + r, sl, stride=ph), :]

    qi = lax.broadcasted_iota(jnp.int32, (QBLK, 2 * QBLK), 0)
    kj = lax.broadcasted_iota(jnp.int32, (QBLK, 2 * QBLK), 1)
    band = (kj >= qi) & (kj <= qi + QBLK)
    bias_ref[0] = jnp.where(band, 0.0, NEG)
    bias_ref[1] = jnp.where(band & (kj >= jnp.where(t > 0, 0, QBLK)), 0.0, NEG)

    def attend(qb, kb, vb, first):
        sc = lax.dot_general(qb.astype(BF16), kb.astype(BF16), (((1,), (1,)), ((), ())),
                             preferred_element_type=F32)
        sc = sc + bias_ref[1 if first else 0]
        m = jnp.max(sc, axis=-1, keepdims=True)
        p = jnp.exp(sc - m)
        l = jnp.sum(p, axis=-1, keepdims=True)
        ob = jnp.dot(p.astype(BF16), vb.astype(BF16), preferred_element_type=F32) / l
        return ob, jnp.broadcast_to(m + jnp.log(l), (QBLK, HEAD_DIM))

    for b in range(tl // QBLK):
        ob, lse = attend(qn_ref[b * QBLK:(b + 1) * QBLK, :],
                         kn_ref[tl + (b - 1) * QBLK:tl + (b + 1) * QBLK, :],
                         vv_ref[tl + (b - 1) * QBLK:tl + (b + 1) * QBLK, :], b == 0)
        s = b % 2
        st_ref[s, 0] = ob
        st_ref[s, 1] = lse
        n = QBLK // ph
        for r in range(ph):
            ob_ref[0, r, b * n:(b + 1) * n, :] = st_ref[s, 0, pl.ds(r, n, stride=ph), :]
            lb_ref[0, r, b * n:(b + 1) * n, :] = st_ref[s, 1, pl.ds(r, n, stride=ph), :]

    for r in range(ph):
        for b in range(sl // QBLK):
            ob, lse = attend(q4_ref[r, b * QBLK:(b + 1) * QBLK, :],
                             k4_ref[r, sl + (b - 1) * QBLK:sl + (b + 1) * QBLK, :],
                             v4_ref[r, sl + (b - 1) * QBLK:sl + (b + 1) * QBLK, :], b == 0)
            ob_ref[1, r, b * QBLK:(b + 1) * QBLK, :] = ob
            lb_ref[1, r, b * QBLK:(b + 1) * QBLK, :] = lse
        for a in range(ph):
            ob, lse = attend(q4_ref[r, pl.ds(a, QBLK, stride=ph), :],
                             k4_ref[r, pl.ds(a, 2 * QBLK, stride=ph), :],
                             v4_ref[r, pl.ds(a, 2 * QBLK, stride=ph), :], True)
            ob_ref[2, r, pl.ds(a, QBLK, stride=ph), :] = ob
            lb_ref[2, r, pl.ds(a, QBLK, stride=ph), :] = lse

    for r in range(ph):
        for c in range(sl // QBLK):
            rows = slice(c * QBLK, (c + 1) * QBLK)
            l0, l1, l2 = lb_ref[0, r, rows, :], lb_ref[1, r, rows, :], lb_ref[2, r, rows, :]
            mx = jnp.maximum(jnp.maximum(l0, l1), l2)
            w0, w1, w2 = jnp.exp(l0 - mx), jnp.exp(l1 - mx), jnp.exp(l2 - mx)
            y = (w0 * ob_ref[0, r, rows, :] + w1 * ob_ref[1, r, rows, :]
                 + w2 * ob_ref[2, r, rows, :]) / (w0 + w1 + w2)
            yn_ref[pl.ds(c * QBLK * ph + r, QBLK, stride=ph), :] = y
    o_ref[...] = yn_ref[...].astype(o_ref.dtype)


def _attention(proj, q_g, k_g):
    s = proj.shape[0]
    tl = ATTN_TILE
    qc = (2 * SGU_WIDTH + 2 * CONV_CH) // HEAD_DIM
    kc = qc + ATTN_HEADS
    vc = kc + ATTN_HEADS
    return pl.pallas_call(
        _attn_kernel,
        grid=(ATTN_HEADS, s // tl),
        in_specs=[
            pl.BlockSpec((tl, HEAD_DIM), lambda h, t: (t, qc + h)),
            pl.BlockSpec((tl, HEAD_DIM), lambda h, t: (t, kc + h)),
            pl.BlockSpec((tl, HEAD_DIM), lambda h, t: (t, vc + h)),
            pl.BlockSpec((1, HEAD_DIM), lambda h, t: (0, 0)),
            pl.BlockSpec((1, HEAD_DIM), lambda h, t: (0, 0)),
        ],
        out_specs=pl.BlockSpec((tl, HEAD_DIM), lambda h, t: (t, h)),
        out_shape=jax.ShapeDtypeStruct((s, ATTN_WIDTH), BF16),
        scratch_shapes=[
            pltpu.VMEM((tl, HEAD_DIM), F32),
            pltpu.VMEM((2 * tl, HEAD_DIM), F32),
            pltpu.VMEM((2 * tl, HEAD_DIM), F32),
            pltpu.VMEM((ATTN_PH, ATTN_SL, HEAD_DIM), F32),
            pltpu.VMEM((ATTN_PH, 2 * ATTN_SL, HEAD_DIM), F32),
            pltpu.VMEM((ATTN_PH, 2 * ATTN_SL, HEAD_DIM), F32),
            pltpu.VMEM((len(DILATIONS), ATTN_PH, ATTN_SL, HEAD_DIM), F32),
            pltpu.VMEM((len(DILATIONS), ATTN_PH, ATTN_SL, HEAD_DIM), F32),
            pltpu.VMEM((2, 2, QBLK, HEAD_DIM), F32),
            pltpu.VMEM((tl, HEAD_DIM), F32),
            pltpu.VMEM((2, QBLK, 2 * QBLK), F32),
        ],
        compiler_params=pltpu.CompilerParams(
            dimension_semantics=("parallel", "arbitrary"), vmem_limit_bytes=VMEM_LIMIT),
        name="attention",
    )(proj, proj, proj, q_g.reshape(1, HEAD_DIM), k_g.reshape(1, HEAD_DIM))


def _split_bf16(x):
    hi = x.astype(BF16)
    lo = (x - hi.astype(F32)).astype(BF16)
    return hi, lo


def _pack_halves(x):
    c = x.shape[1] // 2
    lo = lax.bitcast_convert_type(x[:, :c].astype(BF16).astype(F32), U32)
    hi = lax.bitcast_convert_type(x[:, c:].astype(BF16).astype(F32), U32)
    return (lo >> 16) | hi


def _unpack_halves(u):
    lo = lax.bitcast_convert_type(u << 16, F32)
    hi = lax.bitcast_convert_type(u & jnp.uint32(0xFFFF0000), F32)
    return lo, hi


def _out_proj_kernel(yab_ref, yc_ref, onc_ref, h_ref, w_ref, n2_ref, rwh_ref, rwl_ref, rb_ref,
                     h1_ref, xp_ref, route_ref, routet_ref, cnt_ref, tri_ref, run_ref):
    tm = h_ref.shape[0]
    ts = tri_ref.shape[0]

    @pl.when(pl.program_id(0) == 0)
    def _():
        r = lax.broadcasted_iota(jnp.int32, (ts, ts), 0)
        c = lax.broadcasted_iota(jnp.int32, (ts, ts), 1)
        tri_ref[...] = jnp.where(c < r, 1.0, 0.0).astype(BF16)
        run_ref[...] = jnp.zeros(run_ref.shape, F32)

    half = yab_ref.shape[1]
    subs = [slice(p * ts, (p + 1) * ts) for p in range(tm // ts)]
    for rows in subs:
        ycn = _rms(yc_ref[rows, :].astype(F32), onc_ref[...]).astype(BF16)
        acc = jnp.dot(yab_ref[rows, :], w_ref[0:half, :], preferred_element_type=F32)
        acc = acc + jnp.dot(ycn, w_ref[half:, :], preferred_element_type=F32)
        h1_ref[rows, :] = h_ref[rows, :] + acc
    run = run_ref[...]
    for rows in subs:
        run = _route_rows(rows, run, n2_ref, rwh_ref, rwl_ref, rb_ref, h1_ref, xp_ref, route_ref,
                          routet_ref, tri_ref)
    run_ref[...] = run
    cnt_ref[...] = run


def _route_rows(rows, run, n2_ref, rwh_ref, rwl_ref, rb_ref, h1_ref, xp_ref, route_ref,
                routet_ref, tri_ref):
    xn = _rms(h1_ref[rows, :], n2_ref[...])
    xp_ref[rows, :] = _pack_halves(xn)

    xh, xl = _split_bf16(xn)
    lg = (jnp.dot(xh, rwh_ref[...], preferred_element_type=F32)
          + jnp.dot(xl, rwh_ref[...], preferred_element_type=F32)
          + jnp.dot(xh, rwl_ref[...], preferred_element_type=F32)) + rb_ref[...]

    lane = lax.broadcasted_iota(jnp.int32, lg.shape, 1)
    big = jnp.int32(LANES)
    gl = jnp.where(lane < N_GROUPS, lg, NEG)
    gmax = jnp.max(gl, axis=-1, keepdims=True)
    grp = jnp.min(jnp.where(gl == gmax, lane, big), axis=-1, keepdims=True)
    gate_g = 1.0 / jnp.sum(jnp.exp(gl - gmax), axis=-1, keepdims=True)
    lo = N_GROUPS + grp * EXPERTS_PER_GROUP
    el = jnp.where((lane >= lo) & (lane < lo + EXPERTS_PER_GROUP), lg, NEG)
    v1 = jnp.max(el, axis=-1, keepdims=True)
    i1 = jnp.min(jnp.where(el == v1, lane, big), axis=-1, keepdims=True)
    el2 = jnp.where(lane == i1, NEG, el)
    v2 = jnp.max(el2, axis=-1, keepdims=True)
    i2 = jnp.min(jnp.where(el2 == v2, lane, big), axis=-1, keepdims=True)
    e2 = jnp.exp(v2 - v1)
    g0 = gate_g / (1.0 + e2)
    g1 = gate_g * e2 / (1.0 + e2)
    oh0 = lane == i1 - N_GROUPS
    oh1 = lane == i2 - N_GROUPS
    cnt = jnp.where(oh0 | oh1, 1.0, 0.0)
    before = jnp.dot(tri_ref[...], cnt.astype(BF16), preferred_element_type=F32) + run
    r0 = jnp.sum(jnp.where(oh0, before, 0.0), axis=-1, keepdims=True)
    r1 = jnp.sum(jnp.where(oh1, before, 0.0), axis=-1, keepdims=True)

    route = (i1 - N_GROUPS).astype(F32)
    for k, val in enumerate(((i2 - N_GROUPS).astype(F32), g0, g1, r0, r1), start=1):
        route = jnp.where(lane == k, val, route)
    route = jnp.where(lane < ROUTE_LANES, route, 0.0)
    route_ref[rows, :] = route
    routet_ref[:, rows] = route.T[0:8, :]
    return run + jnp.sum(cnt, axis=0, keepdims=True)


def _out_proj(y_ab, y_c, on_c, h, w_bf16, norm2, rw, rb, tm=512, sub_rows=512):
    s, d = h.shape
    half = y_ab.shape[1]
    rw_pad = jnp.pad(rw, ((0, 0), (0, LANES - ROUTE_COLS)))
    rwh, rwl = _split_bf16(rw_pad)
    rb_pad = jnp.pad(rb, (0, LANES - ROUTE_COLS)).reshape(1, LANES)
    const = lambda i: (0, 0)
    return pl.pallas_call(
        _out_proj_kernel,
        grid=(s // tm,),
        in_specs=[
            pl.BlockSpec((tm, half), lambda i: (i, 0)),
            pl.BlockSpec((tm, half), lambda i: (i, 0)),
            pl.BlockSpec((1, half), const),
            pl.BlockSpec((tm, d), lambda i: (i, 0)),
            pl.BlockSpec((d, d), const, pipeline_mode=pl.Buffered(1)),
            pl.BlockSpec((1, d), const),
            pl.BlockSpec((d, LANES), const),
            pl.BlockSpec((d, LANES), const),
            pl.BlockSpec((1, LANES), const),
        ],
        out_specs=[
            pl.BlockSpec((tm, d), lambda i: (i, 0)),
            pl.BlockSpec((tm, d // 2), lambda i: (i, 0)),
            pl.BlockSpec((tm, LANES), lambda i: (i, 0)),
            pl.BlockSpec((8, tm), lambda i: (0, i)),
            pl.BlockSpec((1, LANES), const),
        ],
        out_shape=[
            jax.ShapeDtypeStruct((s, d), F32),
            jax.ShapeDtypeStruct((s, d // 2), U32),
            jax.ShapeDtypeStruct((s, LANES), F32),
            jax.ShapeDtypeStruct((8, s), F32),
            jax.ShapeDtypeStruct((1, LANES), F32),
        ],
        scratch_shapes=[pltpu.VMEM((sub_rows, sub_rows), BF16), pltpu.VMEM((1, LANES), F32)],
        compiler_params=pltpu.CompilerParams(
            dimension_semantics=("arbitrary",), vmem_limit_bytes=VMEM_LIMIT),
        name="out_proj",
    )(y_ab, y_c, on_c.reshape(1, half), h, w_bf16, norm2.reshape(1, d), rwh, rwl, rb_pad)


def _route_plan(route_t, cnt, t):
    nblk = t * TOP_K // MOE_BLK + N_EXPERTS
    ids = jnp.arange(N_EXPERTS, dtype=jnp.int32)
    e = route_t[0:TOP_K].astype(jnp.int32)
    rank = route_t[4:4 + TOP_K].astype(jnp.int32)
    counts = cnt[0, :N_EXPERTS].astype(jnp.int32)
    pcounts = (counts + MOE_BLK - 1) // MOE_BLK * MOE_BLK
    pends = jnp.cumsum(pcounts)
    pstarts = pends - pcounts
    dest = (jnp.sum(jnp.where(e[:, :, None] == ids, pstarts, 0), axis=-1) + rank).reshape(-1)
    blk = jnp.arange(nblk, dtype=jnp.int32)
    blk_e = jnp.minimum(jnp.sum((blk[:, None] * MOE_BLK >= pends[None, :]).astype(jnp.int32), axis=1),
                        N_EXPERTS - 1)
    n_act = pends[-1] // MOE_BLK
    mine = blk_e[:, None] == ids[None, :]
    first = (blk == 0) | (blk_e != jnp.roll(blk_e, 1))
    seg_par = (jnp.cumsum(first.astype(jnp.int32)) - 1) % 2
    nxt_blk = jnp.sum(jnp.where(mine, pends, 0), axis=1) // MOE_BLK
    nxt_e = jnp.where(nxt_blk < n_act,
                      jnp.sum(jnp.where(nxt_blk[:, None] == blk[None, :], blk_e[None, :], 0), axis=1),
                      -1)
    i32 = lambda a: a.astype(jnp.int32)
    return (i32(dest), i32(pstarts + counts), i32(pcounts - counts), i32(blk_e), i32(first),
            i32(seg_par), i32(nxt_e), i32(n_act).reshape(1), nblk)


def _dispatch_kernel(dest_ref, pad_start_ref, pad_n_ref, n_act_ref, xp_ref, xs_ref,
                     buf, zbuf, sem, zsem, *, tm, nsteps, nblk):
    i = pl.program_id(0)
    slot = i % 2

    def wait_slot(sl):
        for _ in range(TOP_K):
            pltpu.make_async_copy(buf.at[sl], xs_ref.at[pl.ds(0, tm)], sem.at[sl]).wait()

    def for_padding_copies(fn):
        sub = 8
        for e in range(N_EXPERTS):
            n = pad_n_ref[e]
            start = pad_start_ref[e]
            head = jnp.minimum((sub - (start & (sub - 1))) & (sub - 1), n)
            for j in range(sub - 1):
                @pl.when(j < head)
                def _(row=start + j):
                    fn(pltpu.make_async_copy(zbuf.at[pl.ds(0, 1)], xs_ref.at[pl.ds(row, 1)], zsem))
            n = n - head
            start = start + head
            p = MOE_BLK // 2
            while p >= sub:
                @pl.when((n & p) != 0)
                def _(start=start, p=p):
                    fn(pltpu.make_async_copy(zbuf.at[pl.ds(0, p)],
                                             xs_ref.at[pl.ds(pl.multiple_of(start, sub), p)], zsem))
                start = start + (n & p)
                p //= 2
        for j in range(nblk - nsteps * tm * TOP_K // MOE_BLK):
            blk = n_act_ref[0] + j

            @pl.when(blk < nblk)
            def _(blk=blk):
                fn(pltpu.make_async_copy(zbuf, xs_ref.at[pl.ds(blk * MOE_BLK, MOE_BLK)], zsem))

    @pl.when(i == 0)
    def _():
        zbuf[...] = jnp.zeros(zbuf.shape, zbuf.dtype)
        for_padding_copies(lambda cp: cp.start())

    @pl.when(i >= 2)
    def _():
        wait_slot(slot)

    buf[slot] = xp_ref[...]
    for rr in range(tm):
        for k in range(TOP_K):
            dst = dest_ref[k * tm * nsteps + i * tm + rr]
            pltpu.make_async_copy(buf.at[slot, pl.ds(rr, 1)], xs_ref.at[pl.ds(dst, 1)],
                                  sem.at[slot]).start()

    @pl.when(i == nsteps - 1)
    def _():
        wait_slot(slot)
        if nsteps > 1:
            wait_slot(1 - slot)
        for_padding_copies(lambda cp: cp.wait())


def _dispatch(dest, pad_start, pad_n, n_act, xp, nblk, tm=256):
    t, c = xp.shape
    nsteps = t // tm
    return pl.pallas_call(
        functools.partial(_dispatch_kernel, tm=tm, nsteps=nsteps, nblk=nblk),
        grid_spec=pltpu.PrefetchScalarGridSpec(
            num_scalar_prefetch=4,
            grid=(nsteps,),
            in_specs=[pl.BlockSpec((tm, c), lambda i, *_: (i, 0))],
            out_specs=pl.BlockSpec(memory_space=pl.ANY),
            scratch_shapes=[pltpu.VMEM((2, tm, c), U32), pltpu.VMEM((MOE_BLK, c), U32),
                            pltpu.SemaphoreType.DMA((2,)), pltpu.SemaphoreType.DMA],
        ),
        out_shape=jax.ShapeDtypeStruct((nblk * MOE_BLK, c), U32),
        compiler_params=pltpu.CompilerParams(
            dimension_semantics=("arbitrary",), vmem_limit_bytes=VMEM_LIMIT),
        name="dispatch",
    )(dest, pad_start, pad_n, n_act, xp)


def _experts_kernel(blk_e_ref, first_ref, par_ref, nxt_ref, n_act_ref,
                    xs_ref, wg_hbm, wu_hbm, wd_hbm, ys_ref,
                    wg_f, wu_f, wd_f, wg_b, wu_b, wd_b, wsem, *, layer):
    i = pl.program_id(0)
    c = xs_ref.shape[1]

    def weight_copies(e, s):
        return (pltpu.make_async_copy(wg_hbm.at[layer, e], wg_f.at[s], wsem.at[s]),
                pltpu.make_async_copy(wu_hbm.at[layer, e], wu_f.at[s], wsem.at[s]),
                pltpu.make_async_copy(wd_hbm.at[layer, e], wd_f.at[s], wsem.at[s]))

    @pl.when(i >= n_act_ref[0])
    def _():
        ys_ref[...] = jnp.zeros(ys_ref.shape, ys_ref.dtype)

    @pl.when(i < n_act_ref[0])
    def _():
        s = par_ref[i]

        @pl.when(first_ref[i] == 1)
        def _():
            @pl.when(i == 0)
            def _():
                for cp in weight_copies(blk_e_ref[0], 0):
                    cp.start()

            for cp in weight_copies(0, s):
                cp.wait()
            wg_b[...] = wg_f[s].astype(BF16)
            wu_b[...] = wu_f[s].astype(BF16)
            wd_b[...] = wd_f[s].astype(BF16)

            @pl.when(nxt_ref[i] >= 0)
            def _():
                for cp in weight_copies(nxt_ref[i], 1 - s):
                    cp.start()

        x_lo, x_hi = _unpack_halves(xs_ref[...])
        x_lo = x_lo.astype(BF16)
        x_hi = x_hi.astype(BF16)
        hg = (jnp.dot(x_lo, wg_b[0:c, :], preferred_element_type=F32)
              + jnp.dot(x_hi, wg_b[c:, :], preferred_element_type=F32))
        hu = (jnp.dot(x_lo, wu_b[0:c, :], preferred_element_type=F32)
              + jnp.dot(x_hi, wu_b[c:, :], preferred_element_type=F32))
        hb = (jax.nn.silu(hg) * hu).astype(BF16)
        ys_ref[...] = _pack_halves(jnp.dot(hb, wd_b[...], preferred_element_type=F32))


def _experts(xs, blk_e, first, seg_par, nxt_e, n_act, nblk, wg, wu, wd, layer):
    n_slots, c = xs.shape
    _, _, d, de = wg.shape
    active_rows = lambda i, be, fi, pa, nx, na: (jnp.minimum(i, na[0] - 1), 0)
    return pl.pallas_call(
        functools.partial(_experts_kernel, layer=layer),
        grid_spec=pltpu.PrefetchScalarGridSpec(
            num_scalar_prefetch=5,
            grid=(nblk,),
            in_specs=[
                pl.BlockSpec((MOE_BLK, c), active_rows),
                pl.BlockSpec(memory_space=pl.ANY),
                pl.BlockSpec(memory_space=pl.ANY),
                pl.BlockSpec(memory_space=pl.ANY),
            ],
            out_specs=pl.BlockSpec((MOE_BLK, c), lambda i, *_: (i, 0)),
            scratch_shapes=[
                pltpu.VMEM((2, d, de), F32),
                pltpu.VMEM((2, d, de), F32),
                pltpu.VMEM((2, de, d), F32),
                pltpu.VMEM((d, de), BF16),
                pltpu.VMEM((d, de), BF16),
                pltpu.VMEM((de, d), BF16),
                pltpu.SemaphoreType.DMA((2,)),
            ],
        ),
        out_shape=jax.ShapeDtypeStruct((n_slots, c), U32),
        compiler_params=pltpu.CompilerParams(
            dimension_semantics=("arbitrary",), vmem_limit_bytes=VMEM_LIMIT),
        name="experts",
    )(blk_e, first, seg_par, nxt_e, n_act, xs, wg, wu, wd)


def _combine_kernel(dest_ref, h_ref, route_ref, ys_hbm, *rest, tm, nsteps, emit_next):
    if emit_next:
        gn_ref, o_ref, xn_ref, gbuf, sem = rest
    else:
        o_ref, gbuf, sem = rest
    i = pl.program_id(0)

    def start_gathers(tile, sl):
        for rr in range(tm):
            for k in range(TOP_K):
                src = dest_ref[k * tm * nsteps + tile * tm + rr]
                pltpu.make_async_copy(ys_hbm.at[pl.ds(src, 1)], gbuf.at[sl, pl.ds(k * tm + rr, 1)],
                                      sem.at[sl]).start()

    def wait_gathers(sl):
        for k in range(TOP_K):
            pltpu.make_async_copy(ys_hbm.at[pl.ds(0, tm)], gbuf.at[sl, pl.ds(k * tm, tm)],
                                  sem.at[sl]).wait()

    @pl.when(i == 0)
    def _():
        start_gathers(0, 1)

    sl = (i + 1) % 2
    wait_gathers(sl)
    r = route_ref[...]
    lane = lax.broadcasted_iota(jnp.int32, r.shape, 1)
    g0 = jnp.sum(jnp.where(lane == 2, r, 0.0), axis=-1, keepdims=True)
    g1 = jnp.sum(jnp.where(lane == 3, r, 0.0), axis=-1, keepdims=True)
    a_lo, a_hi = _unpack_halves(gbuf[sl, 0:tm])
    b_lo, b_hi = _unpack_halves(gbuf[sl, tm:2 * tm])
    c = a_lo.shape[1]
    h_lo = h_ref[:, 0:c] + (a_lo * g0 + b_lo * g1)
    h_hi = h_ref[:, c:] + (a_hi * g0 + b_hi * g1)
    o_ref[:, 0:c] = h_lo
    o_ref[:, c:] = h_hi
    if emit_next:
        ms = (jnp.sum(h_lo * h_lo, axis=-1, keepdims=True)
              + jnp.sum(h_hi * h_hi, axis=-1, keepdims=True)) / (2 * c)
        inv = lax.rsqrt(ms + EPS)
        xn_ref[:, 0:c] = (h_lo * inv * gn_ref[:, 0:c]).astype(BF16)
        xn_ref[:, c:] = (h_hi * inv * gn_ref[:, c:]).astype(BF16)
    start_gathers(jnp.minimum(i, nsteps - 1), i % 2)

    @pl.when(i == nsteps)
    def _():
        wait_gathers(i % 2)


def _combine(dest, h1, route, ys, next_gain=None, tm=256):
    t, d = h1.shape
    c = ys.shape[1]
    nsteps = t // tm
    emit_next = next_gain is not None
    prev = lambda i, dst: (jnp.maximum(i - 1, 0), 0)
    in_specs = [
        pl.BlockSpec((tm, d), prev),
        pl.BlockSpec((tm, LANES), prev),
        pl.BlockSpec(memory_space=pl.ANY),
    ]
    out_specs = [pl.BlockSpec((tm, d), prev)]
    out_shape = [jax.ShapeDtypeStruct((t, d), F32)]
    args = [dest, h1, route, ys]
    if emit_next:
        in_specs.append(pl.BlockSpec((1, d), lambda i, dst: (0, 0)))
        out_specs.append(pl.BlockSpec((tm, d), prev))
        out_shape.append(jax.ShapeDtypeStruct((t, d), BF16))
        args.append(next_gain.reshape(1, d))
    return pl.pallas_call(
        functools.partial(_combine_kernel, tm=tm, nsteps=nsteps, emit_next=emit_next),
        grid_spec=pltpu.PrefetchScalarGridSpec(
            num_scalar_prefetch=1,
            grid=(nsteps + 1,),
            in_specs=in_specs,
            out_specs=out_specs,
            scratch_shapes=[pltpu.VMEM((2, TOP_K * tm, c), U32), pltpu.SemaphoreType.DMA((2,))],
        ),
        out_shape=out_shape,
        compiler_params=pltpu.CompilerParams(
            dimension_semantics=("arbitrary",), vmem_limit_bytes=VMEM_LIMIT),
        name="combine",
    )(*args)


def kernel(x, norm1, w_in, sgu_norm, sgu_w, sgu_b, conv_w, conv_b, conv_ln_g, conv_ln_b,
           q_norm, k_norm, out_norm, w_out, norm2, w_router_group, b_router_group,
           w_router_expert, b_router_expert, w_expert_gate, w_expert_up, w_expert_down):
    b, s, d = x.shape
    depth = norm1.shape[0]
    t = b * s
    h = x.reshape(t, d)
    a_end, b_end = SGU_WIDTH, SGU_WIDTH + CONV_CH
    x_in = h
    for l in range(depth):
        proj = _in_proj(x_in, norm1[l], w_in, l)
        y_ab = _mix_ab(proj, sgu_norm[l], sgu_w[l], sgu_b[l], conv_w[l], conv_b[l],
                       conv_ln_g[l], conv_ln_b[l], out_norm[l, :a_end], out_norm[l, a_end:b_end])
        y_c = _attention(proj, q_norm[l], k_norm[l])
        rw = jnp.concatenate([w_router_group[l], w_router_expert[l]], axis=1)
        rb = jnp.concatenate([b_router_group[l], b_router_expert[l]], axis=0)
        h1, xp, route, route_t, cnt = _out_proj(y_ab, y_c, out_norm[l, b_end:], h,
                                                w_out[l].astype(BF16), norm2[l], rw, rb)
        (dest, pad_start, pad_n, blk_e, first, seg_par, nxt_e, n_act,
         nblk) = _route_plan(route_t, cnt, t)
        xs = _dispatch(dest, pad_start, pad_n, n_act, xp, nblk)
        ys = _experts(xs, blk_e, first, seg_par, nxt_e, n_act, nblk,
                      w_expert_gate, w_expert_up, w_expert_down, l)
        if l + 1 < depth:
            h, x_in = _combine(dest, h1, route, ys, next_gain=norm1[l + 1])
        else:
            (h,) = _combine(dest, h1, route, ys)
    return h.reshape(b, s, d)
```

```python
import functools

import jax
import jax.numpy as jnp
from jax import lax
from jax.experimental import pallas as pl
from jax.experimental.pallas import tpu as pltpu

F32 = jnp.float32
BF16 = jnp.bfloat16
U32 = jnp.uint32

D_MODEL = 2048
SGU_WIDTH = 512
SGU_HEADS = 4
SGU_CHUNK = 128
CONV_CH = 512
CONV_K = 31
ATTN_WIDTH = 1024
ATTN_HEADS = 8
HEAD_DIM = 128
DILATIONS = (1, 4, 16)
QBLK = 128
D_IN = 2 * SGU_WIDTH + 2 * CONV_CH + 3 * ATTN_WIDTH
N_GROUPS = 4
EXPERTS_PER_GROUP = 8
N_EXPERTS = 32
TOP_K = 2
D_EXPERT = 512
EPS = 1e-6

LANES = 128
VMEM_LIMIT = 56 * 1024 * 1024
NEG = -1e30

ATTN_TILE = QBLK * max(DILATIONS)
CONV_HALO = 32
MOE_BLK = 256
ROUTE_COLS = N_GROUPS + N_EXPERTS
ROUTE_LANES = 6


def _rms(x, g):
    return x * lax.rsqrt(jnp.mean(x * x, axis=-1, keepdims=True) + EPS) * g


def _in_proj_norm_kernel(x_ref, g_ref, w_ref, o_ref, xn_ref):
    @pl.when(pl.program_id(1) == 0)
    def _():
        xn_ref[...] = _rms(x_ref[...], g_ref[...]).astype(BF16)

    o_ref[...] = jnp.dot(xn_ref[...], w_ref[...].astype(BF16),
                         preferred_element_type=F32).astype(o_ref.dtype)


def _in_proj_kernel(xn_ref, w_ref, o_ref):
    o_ref[...] = jnp.dot(xn_ref[...], w_ref[...].astype(BF16),
                         preferred_element_type=F32).astype(o_ref.dtype)


def _in_proj(x, g, w_in, layer, tn=1024):
    s, d = x.shape
    n = w_in.shape[2]
    prenormed = x.dtype == BF16
    tm = 2048 if prenormed else 1024
    x_spec = pl.BlockSpec((tm, d), lambda i, j: (i, 0))
    w_spec = pl.BlockSpec((None, d, tn), lambda i, j: (layer, 0, j))
    common = dict(
        grid=(s // tm, n // tn),
        out_specs=pl.BlockSpec((tm, tn), lambda i, j: (i, j)),
        out_shape=jax.ShapeDtypeStruct((s, n), BF16),
        compiler_params=pltpu.CompilerParams(
            dimension_semantics=("parallel", "arbitrary"), vmem_limit_bytes=VMEM_LIMIT),
        name="in_proj",
    )
    if prenormed:
        return pl.pallas_call(_in_proj_kernel, in_specs=[x_spec, w_spec], **common)(x, w_in)
    return pl.pallas_call(
        _in_proj_norm_kernel,
        in_specs=[x_spec, pl.BlockSpec((1, d), lambda i, j: (0, 0)), w_spec],
        scratch_shapes=[pltpu.VMEM((tm, d), BF16)],
        **common,
    )(x, g.reshape(1, d), w_in)


def _mix_ab_kernel(u_ref, v_ref, a_ref, gt_ref, ah_ref, gh_ref,
                   sgn_ref, sw_ref, sb_ref, cw_ref, cb_ref, lng_ref, lnb_ref,
                   ona_ref, onb_ref, o_ref, zext_ref, conv_ref, stage_ref, *, tq):
    u = jax.nn.gelu(u_ref[...].astype(F32))
    v = _rms(jax.nn.gelu(v_ref[...].astype(F32)), sgn_ref[...]).astype(BF16)
    row = lax.broadcasted_iota(jnp.int32, (SGU_CHUNK, SGU_CHUNK), 0)
    col = lax.broadcasted_iota(jnp.int32, (SGU_CHUNK, SGU_CHUNK), 1)
    causal = col <= row
    wm = [jnp.where(causal, sw_ref[hh], 0.0).astype(BF16) for hh in range(SGU_HEADS)]
    for c in range(tq // SGU_CHUNK):
        rows = slice(c * SGU_CHUNK, (c + 1) * SGU_CHUNK)
        zs = []
        for hh in range(SGU_HEADS):
            cols = slice(hh * HEAD_DIM, (hh + 1) * HEAD_DIM)
            zs.append(jnp.dot(wm[hh], v[rows, cols], preferred_element_type=F32) + sb_ref[hh])
        ya = u[rows, :] * jnp.concatenate(zs, axis=1)
        o_ref[rows, 0:SGU_WIDTH] = _rms(ya, ona_ref[...]).astype(o_ref.dtype)

    first = pl.program_id(0) == 0
    zh = ah_ref[...].astype(F32) * jax.nn.sigmoid(gh_ref[...].astype(F32))
    zext_ref[0:CONV_HALO, :] = jnp.where(first, 0.0, zh)
    zext_ref[CONV_HALO:, :] = a_ref[...].astype(F32) * jax.nn.sigmoid(gt_ref[...].astype(F32))
    off0 = CONV_HALO - (CONV_K - 1)
    sub = 8
    rc = 128
    for j in range(CONV_CH // LANES):
        cols = slice(j * LANES, (j + 1) * LANES)
        for c in range(tq // rc):
            acc = jnp.zeros((rc, LANES), F32) + cb_ref[:, cols]
            for b in range(sub):
                taps = [(a, sub * a + b - off0) for a in range(CONV_HALO // sub + 1)
                        if 0 <= sub * a + b - off0 < CONV_K]
                rows = rc + sub * taps[-1][0]
                stage_ref[b, 0:rows, :] = zext_ref[pl.ds(c * rc + b, rows), cols]
                for a, k in taps:
                    acc = acc + stage_ref[b, sub * a:sub * a + rc, :] * cw_ref[k:k + 1, cols]
            conv_ref[c * rc:(c + 1) * rc, cols] = acc
    rc = 32
    for c in range(tq // rc):
        acc = conv_ref[c * rc:(c + 1) * rc, :]
        mu = jnp.mean(acc, axis=-1, keepdims=True)
        xc = acc - mu
        y = xc * lax.rsqrt(jnp.mean(xc * xc, axis=-1, keepdims=True) + EPS)
        y = jax.nn.silu(y * lng_ref[...] + lnb_ref[...])
        o_ref[c * rc:(c + 1) * rc, SGU_WIDTH:] = _rms(y, onb_ref[...]).astype(o_ref.dtype)


def _mix_ab(proj, sgu_norm, sgu_w, sgu_b, conv_w, conv_b, ln_g, ln_b, on_a, on_b, tq=512):
    s = proj.shape[0]
    w = SGU_WIDTH
    hb = tq // CONV_HALO
    sb = jnp.broadcast_to(sgu_b[:, :, None], (SGU_HEADS, SGU_CHUNK, HEAD_DIM))
    cw = jnp.pad(conv_w, ((0, 32 - CONV_K), (0, 0)))
    vec = lambda a: a.reshape(1, -1)
    const2 = lambda i: (0, 0)
    const3 = lambda i: (0, 0, 0)
    return pl.pallas_call(
        functools.partial(_mix_ab_kernel, tq=tq),
        grid=(s // tq,),
        in_specs=[
            pl.BlockSpec((tq, w), lambda i: (i, 0)),
            pl.BlockSpec((tq, w), lambda i: (i, 1)),
            pl.BlockSpec((tq, w), lambda i: (i, 2)),
            pl.BlockSpec((tq, w), lambda i: (i, 3)),
            pl.BlockSpec((CONV_HALO, w), lambda i: (jnp.maximum(i * hb - 1, 0), 2)),
            pl.BlockSpec((CONV_HALO, w), lambda i: (jnp.maximum(i * hb - 1, 0), 3)),
            pl.BlockSpec((1, w), const2),
            pl.BlockSpec((SGU_HEADS, SGU_CHUNK, SGU_CHUNK), const3),
            pl.BlockSpec((SGU_HEADS, SGU_CHUNK, HEAD_DIM), const3),
            pl.BlockSpec((32, w), const2),
            pl.BlockSpec((1, w), const2),
            pl.BlockSpec((1, w), const2),
            pl.BlockSpec((1, w), const2),
            pl.BlockSpec((1, w), const2),
            pl.BlockSpec((1, w), const2),
        ],
        out_specs=pl.BlockSpec((tq, 2 * w), lambda i: (i, 0)),
        out_shape=jax.ShapeDtypeStruct((s, 2 * w), BF16),
        scratch_shapes=[pltpu.VMEM((tq + CONV_HALO, w), F32), pltpu.VMEM((tq, w), F32),
                        pltpu.VMEM((8, SGU_CHUNK + CONV_HALO, LANES), F32)],
        compiler_params=pltpu.CompilerParams(
            dimension_semantics=("parallel",), vmem_limit_bytes=VMEM_LIMIT),
        name="mix_ab",
    )(proj, proj, proj, proj, proj, proj, vec(sgu_norm), sgu_w, sb, cw, vec(conv_b),
      vec(ln_g), vec(ln_b), vec(on_a), vec(on_b))


ATTN_PH = 4
ATTN_SL = ATTN_TILE // ATTN_PH


def _attn_kernel(q_ref, k_ref, v_ref, qg_ref, kg_ref, o_ref,
                 qn_ref, kn_ref, vv_ref, q4_ref, k4_ref, v4_ref, ob_ref, lb_ref, st_ref, yn_ref,
                 bias_ref):
    t = pl.program_id(1)
    tl, ph, sl = ATTN_TILE, ATTN_PH, ATTN_SL

    @pl.when(t == 0)
    def _():
        kn_ref[0:tl, :] = jnp.zeros((tl, HEAD_DIM), F32)
        vv_ref[0:tl, :] = jnp.zeros((tl, HEAD_DIM), F32)
        k4_ref[:, 0:sl, :] = jnp.zeros((ph, sl, HEAD_DIM), F32)
        v4_ref[:, 0:sl, :] = jnp.zeros((ph, sl, HEAD_DIM), F32)

    @pl.when(t > 0)
    def _():
        kn_ref[0:tl, :] = kn_ref[tl:, :]
        vv_ref[0:tl, :] = vv_ref[tl:, :]
        k4_ref[:, 0:sl, :] = k4_ref[:, sl:, :]
        v4_ref[:, 0:sl, :] = v4_ref[:, sl:, :]

    kn_ref[tl:, :] = _rms(k_ref[...].astype(F32), kg_ref[...])
    vv_ref[tl:, :] = v_ref[...].astype(F32)
    qn_ref[...] = _rms(q_ref[...].astype(F32), qg_ref[...]) * (HEAD_DIM ** -0.5)
    for r in range(ph):
        q4_ref[r] = qn_ref[pl.ds(r, sl, stride=ph), :]
        k4_ref[r, sl:, :] = kn_ref[pl.ds(tl + r, sl, stride=ph), :]
        v4_ref[r, sl:, :] = vv_ref[pl.ds(tl + r, sl, stride=ph), :]

    qi = lax.broadcasted_iota(jnp.int32, (QBLK, 2 * QBLK), 0)
    kj = lax.broadcasted_iota(jnp.int32, (QBLK, 2 * QBLK), 1)
    band = (kj >= qi) & (kj <= qi + QBLK)
    bias_ref[0] = jnp.where(band, 0.0, NEG)
    bias_ref[1] = jnp.where(band & (kj >= jnp.where(t > 0, 0, QBLK)), 0.0, NEG)

    def attend(qb, kb, vb, first):
        sc = lax.dot_general(qb.astype(BF16), kb.astype(BF16), (((1,), (1,)), ((), ())),
                             preferred_element_type=F32)
        sc = sc + bias_ref[1 if first else 0]
        m = jnp.max(sc, axis=-1, keepdims=True)
        p = jnp.exp(sc - m)
        l = jnp.sum(p, axis=-1, keepdims=True)
        ob = jnp.dot(p.astype(BF16), vb.astype(BF16), preferred_element_type=F32) / l
        return ob, jnp.broadcast_to(m + jnp.log(l), (QBLK, HEAD_DIM))

    for b in range(tl // QBLK):
        ob, lse = attend(qn_ref[b * QBLK:(b + 1) * QBLK, :],
                         kn_ref[tl + (b - 1) * QBLK:tl + (b + 1) * QBLK, :],
                         vv_ref[tl + (b - 1) * QBLK:tl + (b + 1) * QBLK, :], b == 0)
        s = b % 2
        st_ref[s, 0] = ob
        st_ref[s, 1] = lse
        n = QBLK // ph
        for r in range(ph):
            ob_ref[0, r, b * n:(b + 1) * n, :] = st_ref[s, 0, pl.ds(r, n, stride=ph), :]
            lb_ref[0, r, b * n:(b + 1) * n, :] = st_ref[s, 1, pl.ds(r, n, stride=ph), :]

    for r in range(ph):
        for b in range(sl // QBLK):
            ob, lse = attend(q4_ref[r, b * QBLK:(b + 1) * QBLK, :],
                             k4_ref[r, sl + (b - 1) * QBLK:sl + (b + 1) * QBLK, :],
                             v4_ref[r, sl + (b - 1) * QBLK:sl + (b + 1) * QBLK, :], b == 0)
            ob_ref[1, r, b * QBLK:(b + 1) * QBLK, :] = ob
            lb_ref[1, r, b * QBLK:(b + 1) * QBLK, :] = lse
        for a in range(ph):
            ob, lse = attend(q4_ref[r, pl.ds(a, QBLK, stride=ph), :],
                             k4_ref[r, pl.ds(a, 2 * QBLK, stride=ph), :],
                             v4_ref[r, pl.ds(a, 2 * QBLK, stride=ph), :], True)
            ob_ref[2, r, pl.ds(a, QBLK, stride=ph), :] = ob
            lb_ref[2, r, pl.ds(a, QBLK, stride=ph), :] = lse

    for r in range(ph):
        for c in range(sl // QBLK):
            rows = slice(c * QBLK, (c + 1) * QBLK)
            l0, l1, l2 = lb_ref[0, r, rows, :], lb_ref[1, r, rows, :], lb_ref[2, r, rows, :]
            mx = jnp.maximum(jnp.maximum(l0, l1), l2)
            w0, w1, w2 = jnp.exp(l0 - mx), jnp.exp(l1 - mx), jnp.exp(l2 - mx)
            y = (w0 * ob_ref[0, r, rows, :] + w1 * ob_ref[1, r, rows, :]
                 + w2 * ob_ref[2, r, rows, :]) / (w0 + w1 + w2)
            yn_ref[pl.ds(c * QBLK * ph + r, QBLK, stride=ph), :] = y
    o_ref[...] = yn_ref[...].astype(o_ref.dtype)


def _attention(proj, q_g, k_g):
    s = proj.shape[0]
    tl = ATTN_TILE
    qc = (2 * SGU_WIDTH + 2 * CONV_CH) // HEAD_DIM
    kc = qc + ATTN_HEADS
    vc = kc + ATTN_HEADS
    return pl.pallas_call(
        _attn_kernel,
        grid=(ATTN_HEADS, s // tl),
        in_specs=[
            pl.BlockSpec((tl, HEAD_DIM), lambda h, t: (t, qc + h)),
            pl.BlockSpec((tl, HEAD_DIM), lambda h, t: (t, kc + h)),
            pl.BlockSpec((tl, HEAD_DIM), lambda h, t: (t, vc + h)),
            pl.BlockSpec((1, HEAD_DIM), lambda h, t: (0, 0)),
            pl.BlockSpec((1, HEAD_DIM), lambda h, t: (0, 0)),
        ],
        out_specs=pl.BlockSpec((tl, HEAD_DIM), lambda h, t: (t, h)),
        out_shape=jax.ShapeDtypeStruct((s, ATTN_WIDTH), BF16),
        scratch_shapes=[
            pltpu.VMEM((tl, HEAD_DIM), F32),
            pltpu.VMEM((2 * tl, HEAD_DIM), F32),
            pltpu.VMEM((2 * tl, HEAD_DIM), F32),
            pltpu.VMEM((ATTN_PH, ATTN_SL, HEAD_DIM), F32),
            pltpu.VMEM((ATTN_PH, 2 * ATTN_SL, HEAD_DIM), F32),
            pltpu.VMEM((ATTN_PH, 2 * ATTN_SL, HEAD_DIM), F32),
            pltpu.VMEM((len(DILATIONS), ATTN_PH, ATTN_SL, HEAD_DIM), F32),
            pltpu.VMEM((len(DILATIONS), ATTN_PH, ATTN_SL, HEAD_DIM), F32),
            pltpu.VMEM((2, 2, QBLK, HEAD_DIM), F32),
            pltpu.VMEM((tl, HEAD_DIM), F32),
            pltpu.VMEM((2, QBLK, 2 * QBLK), F32),
        ],
        compiler_params=pltpu.CompilerParams(
            dimension_semantics=("parallel", "arbitrary"), vmem_limit_bytes=VMEM_LIMIT),
        name="attention",
    )(proj, proj, proj, q_g.reshape(1, HEAD_DIM), k_g.reshape(1, HEAD_DIM))


def _split_bf16(x):
    hi = x.astype(BF16)
    lo = (x - hi.astype(F32)).astype(BF16)
    return hi, lo


def _pack_halves(x):
    c = x.shape[1] // 2
    lo = lax.bitcast_convert_type(x[:, :c].astype(BF16).astype(F32), U32)
    hi = lax.bitcast_convert_type(x[:, c:].astype(BF16).astype(F32), U32)
    return (lo >> 16) | hi


def _unpack_halves(u):
    lo = lax.bitcast_convert_type(u << 16, F32)
    hi = lax.bitcast_convert_type(u & jnp.uint32(0xFFFF0000), F32)
    return lo, hi


def _out_proj_kernel(yab_ref, yc_ref, onc_ref, h_ref, w_ref, n2_ref, rwh_ref, rwl_ref, rb_ref,
                     h1_ref, xp_ref, route_ref, routet_ref, cnt_ref, tri_ref, run_ref):
    tm = h_ref.shape[0]
    ts = tri_ref.shape[0]

    @pl.when(pl.program_id(0) == 0)
    def _():
        r = lax.broadcasted_iota(jnp.int32, (ts, ts), 0)
        c = lax.broadcasted_iota(jnp.int32, (ts, ts), 1)
        tri_ref[...] = jnp.where(c < r, 1.0, 0.0).astype(BF16)
        run_ref[...] = jnp.zeros(run_ref.shape, F32)

    half = yab_ref.shape[1]
    subs = [slice(p * ts, (p + 1) * ts) for p in range(tm // ts)]
    for rows in subs:
        ycn = _rms(yc_ref[rows, :].astype(F32), onc_ref[...]).astype(BF16)
        acc = jnp.dot(yab_ref[rows, :], w_ref[0:half, :], preferred_element_type=F32)
        acc = acc + jnp.dot(ycn, w_ref[half:, :], preferred_element_type=F32)
        h1_ref[rows, :] = h_ref[rows, :] + acc
    run = run_ref[...]
    for rows in subs:
        run = _route_rows(rows, run, n2_ref, rwh_ref, rwl_ref, rb_ref, h1_ref, xp_ref, route_ref,
                          routet_ref, tri_ref)
    run_ref[...] = run
    cnt_ref[...] = run


def _route_rows(rows, run, n2_ref, rwh_ref, rwl_ref, rb_ref, h1_ref, xp_ref, route_ref,
                routet_ref, tri_ref):
    xn = _rms(h1_ref[rows, :], n2_ref[...])
    xp_ref[rows, :] = _pack_halves(xn)

    xh, xl = _split_bf16(xn)
    lg = (jnp.dot(xh, rwh_ref[...], preferred_element_type=F32)
          + jnp.dot(xl, rwh_ref[...], preferred_element_type=F32)
          + jnp.dot(xh, rwl_ref[...], preferred_element_type=F32)) + rb_ref[...]

    lane = lax.broadcasted_iota(jnp.int32, lg.shape, 1)
    big = jnp.int32(LANES)
    gl = jnp.where(lane < N_GROUPS, lg, NEG)
    gmax = jnp.max(gl, axis=-1, keepdims=True)
    grp = jnp.min(jnp.where(gl == gmax, lane, big), axis=-1, keepdims=True)
    gate_g = 1.0 / jnp.sum(jnp.exp(gl - gmax), axis=-1, keepdims=True)
    lo = N_GROUPS + grp * EXPERTS_PER_GROUP
    el = jnp.where((lane >= lo) & (lane < lo + EXPERTS_PER_GROUP), lg, NEG)
    v1 = jnp.max(el, axis=-1, keepdims=True)
    i1 = jnp.min(jnp.where(el == v1, lane, big), axis=-1, keepdims=True)
    el2 = jnp.where(lane == i1, NEG, el)
    v2 = jnp.max(el2, axis=-1, keepdims=True)
    i2 = jnp.min(jnp.where(el2 == v2, lane, big), axis=-1, keepdims=True)
    e2 = jnp.exp(v2 - v1)
    g0 = gate_g / (1.0 + e2)
    g1 = gate_g * e2 / (1.0 + e2)
    oh0 = lane == i1 - N_GROUPS
    oh1 = lane == i2 - N_GROUPS
    cnt = jnp.where(oh0 | oh1, 1.0, 0.0)
    before = jnp.dot(tri_ref[...], cnt.astype(BF16), preferred_element_type=F32) + run
    r0 = jnp.sum(jnp.where(oh0, before, 0.0), axis=-1, keepdims=True)
    r1 = jnp.sum(jnp.where(oh1, before, 0.0), axis=-1, keepdims=True)

    route = (i1 - N_GROUPS).astype(F32)
    for k, val in enumerate(((i2 - N_GROUPS).astype(F32), g0, g1, r0, r1), start=1):
        route = jnp.where(lane == k, val, route)
    route = jnp.where(lane < ROUTE_LANES, route, 0.0)
    route_ref[rows, :] = route
    routet_ref[:, rows] = route.T[0:8, :]
    return run + jnp.sum(cnt, axis=0, keepdims=True)


def _out_proj(y_ab, y_c, on_c, h, w_bf16, norm2, rw, rb, tm=512, sub_rows=512):
    s, d = h.shape
    half = y_ab.shape[1]
    rw_pad = jnp.pad(rw, ((0, 0), (0, LANES - ROUTE_COLS)))
    rwh, rwl = _split_bf16(rw_pad)
    rb_pad = jnp.pad(rb, (0, LANES - ROUTE_COLS)).reshape(1, LANES)
    const = lambda i: (0, 0)
    return pl.pallas_call(
        _out_proj_kernel,
        grid=(s // tm,),
        in_specs=[
            pl.BlockSpec((tm, half), lambda i: (i, 0)),
            pl.BlockSpec((tm, half), lambda i: (i, 0)),
            pl.BlockSpec((1, half), const),
            pl.BlockSpec((tm, d), lambda i: (i, 0)),
            pl.BlockSpec((d, d), const, pipeline_mode=pl.Buffered(1)),
            pl.BlockSpec((1, d), const),
            pl.BlockSpec((d, LANES), const),
            pl.BlockSpec((d, LANES), const),
            pl.BlockSpec((1, LANES), const),
        ],
        out_specs=[
            pl.BlockSpec((tm, d), lambda i: (i, 0)),
            pl.BlockSpec((tm, d // 2), lambda i: (i, 0)),
            pl.BlockSpec((tm, LANES), lambda i: (i, 0)),
            pl.BlockSpec((8, tm), lambda i: (0, i)),
            pl.BlockSpec((1, LANES), const),
        ],
        out_shape=[
            jax.ShapeDtypeStruct((s, d), F32),
            jax.ShapeDtypeStruct((s, d // 2), U32),
            jax.ShapeDtypeStruct((s, LANES), F32),
            jax.ShapeDtypeStruct((8, s), F32),
            jax.ShapeDtypeStruct((1, LANES), F32),
        ],
        scratch_shapes=[pltpu.VMEM((sub_rows, sub_rows), BF16), pltpu.VMEM((1, LANES), F32)],
        compiler_params=pltpu.CompilerParams(
            dimension_semantics=("arbitrary",), vmem_limit_bytes=VMEM_LIMIT),
        name="out_proj",
    )(y_ab, y_c, on_c.reshape(1, half), h, w_bf16, norm2.reshape(1, d), rwh, rwl, rb_pad)


def _route_plan(route_t, cnt, t):
    nblk = t * TOP_K // MOE_BLK + N_EXPERTS
    ids = jnp.arange(N_EXPERTS, dtype=jnp.int32)
    e = route_t[0:TOP_K].astype(jnp.int32)
    rank = route_t[4:4 + TOP_K].astype(jnp.int32)
    counts = cnt[0, :N_EXPERTS].astype(jnp.int32)
    pcounts = (counts + MOE_BLK - 1) // MOE_BLK * MOE_BLK
    pends = jnp.cumsum(pcounts)
    pstarts = pends - pcounts
    dest = (jnp.sum(jnp.where(e[:, :, None] == ids, pstarts, 0), axis=-1) + rank).reshape(-1)
    blk = jnp.arange(nblk, dtype=jnp.int32)
    blk_e = jnp.minimum(jnp.sum((blk[:, None] * MOE_BLK >= pends[None, :]).astype(jnp.int32), axis=1),
                        N_EXPERTS - 1)
    n_act = pends[-1] // MOE_BLK
    mine = blk_e[:, None] == ids[None, :]
    first = (blk == 0) | (blk_e != jnp.roll(blk_e, 1))
    seg_par = (jnp.cumsum(first.astype(jnp.int32)) - 1) % 2
    nxt_blk = jnp.sum(jnp.where(mine, pends, 0), axis=1) // MOE_BLK
    nxt_e = jnp.where(nxt_blk < n_act,
                      jnp.sum(jnp.where(nxt_blk[:, None] == blk[None, :], blk_e[None, :], 0), axis=1),
                      -1)
    i32 = lambda a: a.astype(jnp.int32)
    return (i32(dest), i32(pstarts + counts), i32(pcounts - counts), i32(blk_e), i32(first),
            i32(seg_par), i32(nxt_e), i32(n_act).reshape(1), nblk)


def _dispatch_kernel(dest_ref, pad_start_ref, pad_n_ref, n_act_ref, xp_ref, xs_ref,
                     buf, zbuf, sem, zsem, *, tm, nsteps, nblk):
    i = pl.program_id(0)
    slot = i % 2

    def wait_slot(sl):
        for _ in range(TOP_K):
            pltpu.make_async_copy(buf.at[sl], xs_ref.at[pl.ds(0, tm)], sem.at[sl]).wait()

    def for_padding_copies(fn):
        sub = 8
        for e in range(N_EXPERTS):
            n = pad_n_ref[e]
            start = pad_start_ref[e]
            head = jnp.minimum((sub - (start & (sub - 1))) & (sub - 1), n)
            for j in range(sub - 1):
                @pl.when(j < head)
                def _(row=start + j):
                    fn(pltpu.make_async_copy(zbuf.at[pl.ds(0, 1)], xs_ref.at[pl.ds(row, 1)], zsem))
            n = n - head
            start = start + head
            p = MOE_BLK // 2
            while p >= sub:
                @pl.when((n & p) != 0)
                def _(start=start, p=p):
                    fn(pltpu.make_async_copy(zbuf.at[pl.ds(0, p)],
                                             xs_ref.at[pl.ds(pl.multiple_of(start, sub), p)], zsem))
                start = start + (n & p)
                p //= 2
        for j in range(nblk - nsteps * tm * TOP_K // MOE_BLK):
            blk = n_act_ref[0] + j

            @pl.when(blk < nblk)
            def _(blk=blk):
                fn(pltpu.make_async_copy(zbuf, xs_ref.at[pl.ds(blk * MOE_BLK, MOE_BLK)], zsem))

    @pl.when(i == 0)
    def _():
        zbuf[...] = jnp.zeros(zbuf.shape, zbuf.dtype)
        for_padding_copies(lambda cp: cp.start())

    @pl.when(i >= 2)
    def _():
        wait_slot(slot)

    buf[slot] = xp_ref[...]
    for rr in range(tm):
        for k in range(TOP_K):
            dst = dest_ref[k * tm * nsteps + i * tm + rr]
            pltpu.make_async_copy(buf.at[slot, pl.ds(rr, 1)], xs_ref.at[pl.ds(dst, 1)],
                                  sem.at[slot]).start()

    @pl.when(i == nsteps - 1)
    def _():
        wait_slot(slot)
        if nsteps > 1:
            wait_slot(1 - slot)
        for_padding_copies(lambda cp: cp.wait())


def _dispatch(dest, pad_start, pad_n, n_act, xp, nblk, tm=256):
    t, c = xp.shape
    nsteps = t // tm
    return pl.pallas_call(
        functools.partial(_dispatch_kernel, tm=tm, nsteps=nsteps, nblk=nblk),
        grid_spec=pltpu.PrefetchScalarGridSpec(
            num_scalar_prefetch=4,
            grid=(nsteps,),
            in_specs=[pl.BlockSpec((tm, c), lambda i, *_: (i, 0))],
            out_specs=pl.BlockSpec(memory_space=pl.ANY),
            scratch_shapes=[pltpu.VMEM((2, tm, c), U32), pltpu.VMEM((MOE_BLK, c), U32),
                            pltpu.SemaphoreType.DMA((2,)), pltpu.SemaphoreType.DMA],
        ),
        out_shape=jax.ShapeDtypeStruct((nblk * MOE_BLK, c), U32),
        compiler_params=pltpu.CompilerParams(
            dimension_semantics=("arbitrary",), vmem_limit_bytes=VMEM_LIMIT),
        name="dispatch",
    )(dest, pad_start, pad_n, n_act, xp)


def _experts_kernel(dest_ref, pad_start_ref, pad_n_ref, blk_e_ref, first_ref, par_ref, nxt_ref,
                    n_act_ref, xp_hbm, wg_hbm, wu_hbm, wd_hbm, ys_ref,
                    slot_tok, xbuf, gsem, wg_f, wu_f, wd_f, wg_b, wu_b, wd_b, wsem,
                    *, layer, n_tok):
    i = pl.program_id(0)
    n_act = n_act_ref[0]
    c = xbuf.shape[2]

    def weight_copies(e, s):
        return (pltpu.make_async_copy(wg_hbm.at[layer, e], wg_f.at[s], wsem.at[s]),
                pltpu.make_async_copy(wu_hbm.at[layer, e], wu_f.at[s], wsem.at[s]),
                pltpu.make_async_copy(wd_hbm.at[layer, e], wd_f.at[s], wsem.at[s]))

    def start_gathers(blk, sl):
        for rr in range(MOE_BLK):
            tok = slot_tok[blk * MOE_BLK + rr]
            pltpu.make_async_copy(xp_hbm.at[pl.ds(tok, 1)], xbuf.at[sl, pl.ds(rr, 1)],
                                  gsem.at[sl]).start()

    def wait_gathers(sl):
        pltpu.make_async_copy(xp_hbm.at[pl.ds(0, MOE_BLK)], xbuf.at[sl], gsem.at[sl]).wait()

    @pl.when(i == 0)
    def _():
        for e in range(N_EXPERTS):
            def fill(j, carry, e=e):
                slot_tok[pad_start_ref[e] + j] = 0
                return carry
            lax.fori_loop(0, pad_n_ref[e], fill, 0)

        unroll = 16

        def claim(chunk, carry):
            for u in range(unroll):
                a = chunk * unroll + u
                slot_tok[dest_ref[a]] = jnp.where(a >= n_tok, a - n_tok, a)
            return carry
        lax.fori_loop(0, TOP_K * n_tok // unroll, claim, 0)
        start_gathers(0, 0)

    @pl.when(i >= n_act)
    def _():
        ys_ref[...] = jnp.zeros(ys_ref.shape, ys_ref.dtype)

    @pl.when(i < n_act)
    def _():
        s = par_ref[i]

        @pl.when(first_ref[i] == 1)
        def _():
            @pl.when(i == 0)
            def _():
                for cp in weight_copies(blk_e_ref[0], 0):
                    cp.start()

            for cp in weight_copies(0, s):
                cp.wait()
            wg_b[...] = wg_f[s].astype(BF16)
            wu_b[...] = wu_f[s].astype(BF16)
            wd_b[...] = wd_f[s].astype(BF16)

            @pl.when(nxt_ref[i] >= 0)
            def _():
                for cp in weight_copies(nxt_ref[i], 1 - s):
                    cp.start()

        wait_gathers(i % 2)
        x_lo, x_hi = _unpack_halves(xbuf[i % 2])
        x_lo = x_lo.astype(BF16)
        x_hi = x_hi.astype(BF16)
        hg = (jnp.dot(x_lo, wg_b[0:c, :], preferred_element_type=F32)
              + jnp.dot(x_hi, wg_b[c:, :], preferred_element_type=F32))
        hu = (jnp.dot(x_lo, wu_b[0:c, :], preferred_element_type=F32)
              + jnp.dot(x_hi, wu_b[c:, :], preferred_element_type=F32))
        hb = (jax.nn.silu(hg) * hu).astype(BF16)
        ys_ref[...] = _pack_halves(jnp.dot(hb, wd_b[...], preferred_element_type=F32))
        start_gathers(jnp.minimum(i + 1, n_act - 1), (i + 1) % 2)

    @pl.when(i == n_act - 1)
    def _():
        wait_gathers((i + 1) % 2)


def _experts(xp, dest, pad_start, pad_n, blk_e, first, seg_par, nxt_e, n_act, nblk, wg, wu, wd,
             layer):
    n_tok, c = xp.shape
    _, _, d, de = wg.shape
    return pl.pallas_call(
        functools.partial(_experts_kernel, layer=layer, n_tok=n_tok),
        grid_spec=pltpu.PrefetchScalarGridSpec(
            num_scalar_prefetch=8,
            grid=(nblk,),
            in_specs=[
                pl.BlockSpec(memory_space=pl.ANY),
                pl.BlockSpec(memory_space=pl.ANY),
                pl.BlockSpec(memory_space=pl.ANY),
                pl.BlockSpec(memory_space=pl.ANY),
            ],
            out_specs=pl.BlockSpec((MOE_BLK, c), lambda i, *_: (i, 0)),
            scratch_shapes=[
                pltpu.SMEM((nblk * MOE_BLK,), jnp.int32),
                pltpu.VMEM((2, MOE_BLK, c), U32),
                pltpu.SemaphoreType.DMA((2,)),
                pltpu.VMEM((2, d, de), F32),
                pltpu.VMEM((2, d, de), F32),
                pltpu.VMEM((2, de, d), F32),
                pltpu.VMEM((d, de), BF16),
                pltpu.VMEM((d, de), BF16),
                pltpu.VMEM((de, d), BF16),
                pltpu.SemaphoreType.DMA((2,)),
            ],
        ),
        out_shape=jax.ShapeDtypeStruct((nblk * MOE_BLK, c), U32),
        compiler_params=pltpu.CompilerParams(
            dimension_semantics=("arbitrary",), vmem_limit_bytes=VMEM_LIMIT),
        name="experts",
    )(dest, pad_start, pad_n, blk_e, first, seg_par, nxt_e, n_act, xp, wg, wu, wd)


def _combine_kernel(dest_ref, h_ref, route_ref, ys_hbm, *rest, tm, nsteps, emit_next):
    if emit_next:
        gn_ref, o_ref, xn_ref, gbuf, sem = rest
    else:
        o_ref, gbuf, sem = rest
    i = pl.program_id(0)

    def start_gathers(tile, sl):
        for rr in range(tm):
            for k in range(TOP_K):
                src = dest_ref[k * tm * nsteps + tile * tm + rr]
                pltpu.make_async_copy(ys_hbm.at[pl.ds(src, 1)], gbuf.at[sl, pl.ds(k * tm + rr, 1)],
                                      sem.at[sl]).start()

    def wait_gathers(sl):
        for k in range(TOP_K):
            pltpu.make_async_copy(ys_hbm.at[pl.ds(0, tm)], gbuf.at[sl, pl.ds(k * tm, tm)],
                                  sem.at[sl]).wait()

    @pl.when(i < nsteps)
    def _():
        start_gathers(i, i % 2)

    @pl.when(i >= 1)
    def _():
        sl = (i - 1) % 2
        wait_gathers(sl)
        r = route_ref[...]
        lane = lax.broadcasted_iota(jnp.int32, r.shape, 1)
        g0 = jnp.sum(jnp.where(lane == 2, r, 0.0), axis=-1, keepdims=True)
        g1 = jnp.sum(jnp.where(lane == 3, r, 0.0), axis=-1, keepdims=True)
        a_lo, a_hi = _unpack_halves(gbuf[sl, 0:tm])
        b_lo, b_hi = _unpack_halves(gbuf[sl, tm:2 * tm])
        c = a_lo.shape[1]
        h_lo = h_ref[:, 0:c] + (a_lo * g0 + b_lo * g1)
        h_hi = h_ref[:, c:] + (a_hi * g0 + b_hi * g1)
        o_ref[:, 0:c] = h_lo
        o_ref[:, c:] = h_hi
        if emit_next:
            ms = (jnp.sum(h_lo * h_lo, axis=-1, keepdims=True)
                  + jnp.sum(h_hi * h_hi, axis=-1, keepdims=True)) / (2 * c)
            inv = lax.rsqrt(ms + EPS)
            xn_ref[:, 0:c] = (h_lo * inv * gn_ref[:, 0:c]).astype(BF16)
            xn_ref[:, c:] = (h_hi * inv * gn_ref[:, c:]).astype(BF16)


def _combine(dest, h1, route, ys, next_gain=None, tm=256):
    t, d = h1.shape
    c = ys.shape[1]
    nsteps = t // tm
    emit_next = next_gain is not None
    prev = lambda i, dst: (jnp.maximum(i - 1, 0), 0)
    in_specs = [
        pl.BlockSpec((tm, d), prev),
        pl.BlockSpec((tm, LANES), prev),
        pl.BlockSpec(memory_space=pl.ANY),
    ]
    out_specs = [pl.BlockSpec((tm, d), prev)]
    out_shape = [jax.ShapeDtypeStruct((t, d), F32)]
    args = [dest, h1, route, ys]
    if emit_next:
        in_specs.append(pl.BlockSpec((1, d), lambda i, dst: (0, 0)))
        out_specs.append(pl.BlockSpec((tm, d), prev))
        out_shape.append(jax.ShapeDtypeStruct((t, d), BF16))
        args.append(next_gain.reshape(1, d))
    return pl.pallas_call(
        functools.partial(_combine_kernel, tm=tm, nsteps=nsteps, emit_next=emit_next),
        grid_spec=pltpu.PrefetchScalarGridSpec(
            num_scalar_prefetch=1,
            grid=(nsteps + 1,),
            in_specs=in_specs,
            out_specs=out_specs,
            scratch_shapes=[pltpu.VMEM((2, TOP_K * tm, c), U32), pltpu.SemaphoreType.DMA((2,))],
        ),
        out_shape=out_shape,
        compiler_params=pltpu.CompilerParams(
            dimension_semantics=("arbitrary",), vmem_limit_bytes=VMEM_LIMIT),
        name="combine",
    )(*args)


def kernel(x, norm1, w_in, sgu_norm, sgu_w, sgu_b, conv_w, conv_b, conv_ln_g, conv_ln_b,
           q_norm, k_norm, out_norm, w_out, norm2, w_router_group, b_router_group,
           w_router_expert, b_router_expert, w_expert_gate, w_expert_up, w_expert_down):
    b, s, d = x.shape
    depth = norm1.shape[0]
    t = b * s
    h = x.reshape(t, d)
    a_end, b_end = SGU_WIDTH, SGU_WIDTH + CONV_CH
    x_in = h
    for l in range(depth):
        proj = _in_proj(x_in, norm1[l], w_in, l)
        y_ab = _mix_ab(proj, sgu_norm[l], sgu_w[l], sgu_b[l], conv_w[l], conv_b[l],
                       conv_ln_g[l], conv_ln_b[l], out_norm[l, :a_end], out_norm[l, a_end:b_end])
        y_c = _attention(proj, q_norm[l], k_norm[l])
        rw = jnp.concatenate([w_router_group[l], w_router_expert[l]], axis=1)
        rb = jnp.concatenate([b_router_group[l], b_router_expert[l]], axis=0)
        h1, xp, route, route_t, cnt = _out_proj(y_ab, y_c, out_norm[l, b_end:], h,
                                                w_out[l].astype(BF16), norm2[l], rw, rb)
        (dest, pad_start, pad_n, blk_e, first, seg_par, nxt_e, n_act,
         nblk) = _route_plan(route_t, cnt, t)
        ys = _experts(xp, dest, pad_start, pad_n, blk_e, first, seg_par, nxt_e, n_act, nblk,
                      w_expert_gate, w_expert_up, w_expert_down, l)
        if l + 1 < depth:
            h, x_in = _combine(dest, h1, route, ys, next_gain=norm1[l + 1])
        else:
            (h,) = _combine(dest, h1, route, ys)
    return h.reshape(b, s, d)
```

```python
import functools

import jax
import jax.numpy as jnp
from jax import lax
from jax.experimental import pallas as pl
from jax.experimental.pallas import tpu as pltpu

F32 = jnp.float32
BF16 = jnp.bfloat16
U32 = jnp.uint32

D_MODEL = 2048
SGU_WIDTH = 512
SGU_HEADS = 4
SGU_CHUNK = 128
CONV_CH = 512
CONV_K = 31
ATTN_WIDTH = 1024
ATTN_HEADS = 8
HEAD_DIM = 128
DILATIONS = (1, 4, 16)
QBLK = 128
D_IN = 2 * SGU_WIDTH + 2 * CONV_CH + 3 * ATTN_WIDTH
N_GROUPS = 4
EXPERTS_PER_GROUP = 8
N_EXPERTS = 32
TOP_K = 2
D_EXPERT = 512
EPS = 1e-6

LANES = 128
VMEM_LIMIT = 56 * 1024 * 1024
NEG = -1e30

ATTN_TILE = QBLK * max(DILATIONS)
CONV_HALO = 32
MOE_BLK = 256
WEIGHT_DMA_PRIORITY = 1
ROUTE_COLS = N_GROUPS + N_EXPERTS
ROUTE_LANES = 6


def _rms(x, g):
    return x * lax.rsqrt(jnp.mean(x * x, axis=-1, keepdims=True) + EPS) * g


def _in_proj_norm_kernel(x_ref, g_ref, w_ref, o_ref, xn_ref):
    @pl.when(pl.program_id(1) == 0)
    def _():
        xn_ref[...] = _rms(x_ref[...], g_ref[...]).astype(BF16)

    o_ref[...] = jnp.dot(xn_ref[...], w_ref[...].astype(BF16),
                         preferred_element_type=F32).astype(o_ref.dtype)


def _in_proj_kernel(xn_ref, w_ref, o_ref):
    o_ref[...] = jnp.dot(xn_ref[...], w_ref[...].astype(BF16),
                         preferred_element_type=F32).astype(o_ref.dtype)


def _in_proj(x, g, w_in, layer, tn=1024):
    s, d = x.shape
    n = w_in.shape[2]
    prenormed = x.dtype == BF16
    tm = 2048 if prenormed else 1024
    x_spec = pl.BlockSpec((tm, d), lambda i, j: (i, 0))
    w_spec = pl.BlockSpec((None, d, tn), lambda i, j: (layer, 0, j))
    common = dict(
        grid=(s // tm, n // tn),
        out_specs=pl.BlockSpec((tm, tn), lambda i, j: (i, j)),
        out_shape=jax.ShapeDtypeStruct((s, n), BF16),
        compiler_params=pltpu.CompilerParams(
            dimension_semantics=("parallel", "arbitrary"), vmem_limit_bytes=VMEM_LIMIT),
        name="in_proj",
    )
    if prenormed:
        return pl.pallas_call(_in_proj_kernel, in_specs=[x_spec, w_spec], **common)(x, w_in)
    return pl.pallas_call(
        _in_proj_norm_kernel,
        in_specs=[x_spec, pl.BlockSpec((1, d), lambda i, j: (0, 0)), w_spec],
        scratch_shapes=[pltpu.VMEM((tm, d), BF16)],
        **common,
    )(x, g.reshape(1, d), w_in)


def _mix_ab_kernel(u_ref, v_ref, a_ref, gt_ref, ah_ref, gh_ref,
                   sgn_ref, sw_ref, sb_ref, cw_ref, cb_ref, lng_ref, lnb_ref,
                   ona_ref, onb_ref, o_ref, zext_ref, conv_ref, stage_ref, *, tq):
    u = jax.nn.gelu(u_ref[...].astype(F32))
    v = _rms(jax.nn.gelu(v_ref[...].astype(F32)), sgn_ref[...]).astype(BF16)
    row = lax.broadcasted_iota(jnp.int32, (SGU_CHUNK, SGU_CHUNK), 0)
    col = lax.broadcasted_iota(jnp.int32, (SGU_CHUNK, SGU_CHUNK), 1)
    causal = col <= row
    wm = [jnp.where(causal, sw_ref[hh], 0.0).astype(BF16) for hh in range(SGU_HEADS)]
    for c in range(tq // SGU_CHUNK):
        rows = slice(c * SGU_CHUNK, (c + 1) * SGU_CHUNK)
        zs = []
        for hh in range(SGU_HEADS):
            cols = slice(hh * HEAD_DIM, (hh + 1) * HEAD_DIM)
            zs.append(jnp.dot(wm[hh], v[rows, cols], preferred_element_type=F32) + sb_ref[hh])
        ya = u[rows, :] * jnp.concatenate(zs, axis=1)
        o_ref[rows, 0:SGU_WIDTH] = _rms(ya, ona_ref[...]).astype(o_ref.dtype)

    first = pl.program_id(0) == 0
    zh = ah_ref[...].astype(F32) * jax.nn.sigmoid(gh_ref[...].astype(F32))
    zext_ref[0:CONV_HALO, :] = jnp.where(first, 0.0, zh)
    zext_ref[CONV_HALO:, :] = a_ref[...].astype(F32) * jax.nn.sigmoid(gt_ref[...].astype(F32))
    off0 = CONV_HALO - (CONV_K - 1)
    sub = 8
    rc = 128
    for j in range(CONV_CH // LANES):
        cols = slice(j * LANES, (j + 1) * LANES)
        for c in range(tq // rc):
            acc = jnp.zeros((rc, LANES), F32) + cb_ref[:, cols]
            for b in range(sub):
                taps = [(a, sub * a + b - off0) for a in range(CONV_HALO // sub + 1)
                        if 0 <= sub * a + b - off0 < CONV_K]
                rows = rc + sub * taps[-1][0]
                stage_ref[b, 0:rows, :] = zext_ref[pl.ds(c * rc + b, rows), cols]
                for a, k in taps:
                    acc = acc + stage_ref[b, sub * a:sub * a + rc, :] * cw_ref[k:k + 1, cols]
            conv_ref[c * rc:(c + 1) * rc, cols] = acc
    rc = 32
    for c in range(tq // rc):
        acc = conv_ref[c * rc:(c + 1) * rc, :]
        mu = jnp.mean(acc, axis=-1, keepdims=True)
        xc = acc - mu
        y = xc * lax.rsqrt(jnp.mean(xc * xc, axis=-1, keepdims=True) + EPS)
        y = jax.nn.silu(y * lng_ref[...] + lnb_ref[...])
        o_ref[c * rc:(c + 1) * rc, SGU_WIDTH:] = _rms(y, onb_ref[...]).astype(o_ref.dtype)


def _mix_ab(proj, sgu_norm, sgu_w, sgu_b, conv_w, conv_b, ln_g, ln_b, on_a, on_b, tq=512):
    s = proj.shape[0]
    w = SGU_WIDTH
    hb = tq // CONV_HALO
    sb = jnp.broadcast_to(sgu_b[:, :, None], (SGU_HEADS, SGU_CHUNK, HEAD_DIM))
    cw = jnp.pad(conv_w, ((0, 32 - CONV_K), (0, 0)))
    vec = lambda a: a.reshape(1, -1)
    const2 = lambda i: (0, 0)
    const3 = lambda i: (0, 0, 0)
    return pl.pallas_call(
        functools.partial(_mix_ab_kernel, tq=tq),
        grid=(s // tq,),
        in_specs=[
            pl.BlockSpec((tq, w), lambda i: (i, 0)),
            pl.BlockSpec((tq, w), lambda i: (i, 1)),
            pl.BlockSpec((tq, w), lambda i: (i, 2)),
            pl.BlockSpec((tq, w), lambda i: (i, 3)),
            pl.BlockSpec((CONV_HALO, w), lambda i: (jnp.maximum(i * hb - 1, 0), 2)),
            pl.BlockSpec((CONV_HALO, w), lambda i: (jnp.maximum(i * hb - 1, 0), 3)),
            pl.BlockSpec((1, w), const2),
            pl.BlockSpec((SGU_HEADS, SGU_CHUNK, SGU_CHUNK), const3),
            pl.BlockSpec((SGU_HEADS, SGU_CHUNK, HEAD_DIM), const3),
            pl.BlockSpec((32, w), const2),
            pl.BlockSpec((1, w), const2),
            pl.BlockSpec((1, w), const2),
            pl.BlockSpec((1, w), const2),
            pl.BlockSpec((1, w), const2),
            pl.BlockSpec((1, w), const2),
        ],
        out_specs=pl.BlockSpec((tq, 2 * w), lambda i: (i, 0)),
        out_shape=jax.ShapeDtypeStruct((s, 2 * w), BF16),
        scratch_shapes=[pltpu.VMEM((tq + CONV_HALO, w), F32), pltpu.VMEM((tq, w), F32),
                        pltpu.VMEM((8, SGU_CHUNK + CONV_HALO, LANES), F32)],
        compiler_params=pltpu.CompilerParams(
            dimension_semantics=("parallel",), vmem_limit_bytes=VMEM_LIMIT),
        name="mix_ab",
    )(proj, proj, proj, proj, proj, proj, vec(sgu_norm), sgu_w, sb, cw, vec(conv_b),
      vec(ln_g), vec(ln_b), vec(on_a), vec(on_b))


ATTN_PH = 4
ATTN_SL = ATTN_TILE // ATTN_PH


def _attn_kernel(q_ref, k_ref, v_ref, qg_ref, kg_ref, o_ref,
                 qn_ref, kn_ref, vv_ref, q4_ref, k4_ref, v4_ref, ob_ref, lb_ref, st_ref, yn_ref,
                 bias_ref):
    t = pl.program_id(1)
    tl, ph, sl = ATTN_TILE, ATTN_PH, ATTN_SL

    @pl.when(t == 0)
    def _():
        kn_ref[0:tl, :] = jnp.zeros((tl, HEAD_DIM), F32)
        vv_ref[0:tl, :] = jnp.zeros((tl, HEAD_DIM), F32)
        k4_ref[:, 0:sl, :] = jnp.zeros((ph, sl, HEAD_DIM), F32)
        v4_ref[:, 0:sl, :] = jnp.zeros((ph, sl, HEAD_DIM), F32)

    @pl.when(t > 0)
    def _():
        kn_ref[0:tl, :] = kn_ref[tl:, :]
        vv_ref[0:tl, :] = vv_ref[tl:, :]
        k4_ref[:, 0:sl, :] = k4_ref[:, sl:, :]
        v4_ref[:, 0:sl, :] = v4_ref[:, sl:, :]

    kn_ref[tl:, :] = _rms(k_ref[...].astype(F32), kg_ref[...])
    vv_ref[tl:, :] = v_ref[...].astype(F32)
    qn_ref[...] = _rms(q_ref[...].astype(F32), qg_ref[...]) * (HEAD_DIM ** -0.5)
    for r in range(ph):
        q4_ref[r] = qn_ref[pl.ds(r, sl, stride=ph), :]
        k4_ref[r, sl:, :] = kn_ref[pl.ds(tl + r, sl, stride=ph), :]
        v4_ref[r, sl:, :] = vv_ref[pl.ds(tl + r, sl, stride=ph), :]

    qi = lax.broadcasted_iota(jnp.int32, (QBLK, 2 * QBLK), 0)
    kj = lax.broadcasted_iota(jnp.int32, (QBLK, 2 * QBLK), 1)
    band = (kj >= qi) & (kj <= qi + QBLK)
    bias_ref[0] = jnp.where(band, 0.0, NEG)
    bias_ref[1] = jnp.where(band & (kj >= jnp.where(t > 0, 0, QBLK)), 0.0, NEG)

    def attend(qb, kb, vb, first):
        sc = lax.dot_general(qb.astype(BF16), kb.astype(BF16), (((1,), (1,)), ((), ())),
                             preferred_element_type=F32)
        sc = sc + bias_ref[1 if first else 0]
        m = jnp.max(sc, axis=-1, keepdims=True)
        p = jnp.exp(sc - m)
        l = jnp.sum(p, axis=-1, keepdims=True)
        ob = jnp.dot(p.astype(BF16), vb.astype(BF16), preferred_element_type=F32) / l
        return ob, jnp.broadcast_to(m + jnp.log(l), (QBLK, HEAD_DIM))

    for b in range(tl // QBLK):
        ob, lse = attend(qn_ref[b * QBLK:(b + 1) * QBLK, :],
                         kn_ref[tl + (b - 1) * QBLK:tl + (b + 1) * QBLK, :],
                         vv_ref[tl + (b - 1) * QBLK:tl + (b + 1) * QBLK, :], b == 0)
        s = b % 2
        st_ref[s, 0] = ob
        st_ref[s, 1] = lse
        n = QBLK // ph
        for r in range(ph):
            ob_ref[0, r, b * n:(b + 1) * n, :] = st_ref[s, 0, pl.ds(r, n, stride=ph), :]
            lb_ref[0, r, b * n:(b + 1) * n, :] = st_ref[s, 1, pl.ds(r, n, stride=ph), :]

    for r in range(ph):
        for b in range(sl // QBLK):
            ob, lse = attend(q4_ref[r, b * QBLK:(b + 1) * QBLK, :],
                             k4_ref[r, sl + (b - 1) * QBLK:sl + (b + 1) * QBLK, :],
                             v4_ref[r, sl + (b - 1) * QBLK:sl + (b + 1) * QBLK, :], b == 0)
            ob_ref[1, r, b * QBLK:(b + 1) * QBLK, :] = ob
            lb_ref[1, r, b * QBLK:(b + 1) * QBLK, :] = lse
        for a in range(ph):
            ob, lse = attend(q4_ref[r, pl.ds(a, QBLK, stride=ph), :],
                             k4_ref[r, pl.ds(a, 2 * QBLK, stride=ph), :],
                             v4_ref[r, pl.ds(a, 2 * QBLK, stride=ph), :], True)
            ob_ref[2, r, pl.ds(a, QBLK, stride=ph), :] = ob
            lb_ref[2, r, pl.ds(a, QBLK, stride=ph), :] = lse

    for r in range(ph):
        for c in range(sl // QBLK):
            rows = slice(c * QBLK, (c + 1) * QBLK)
            l0, l1, l2 = lb_ref[0, r, rows, :], lb_ref[1, r, rows, :], lb_ref[2, r, rows, :]
            mx = jnp.maximum(jnp.maximum(l0, l1), l2)
            w0, w1, w2 = jnp.exp(l0 - mx), jnp.exp(l1 - mx), jnp.exp(l2 - mx)
            y = (w0 * ob_ref[0, r, rows, :] + w1 * ob_ref[1, r, rows, :]
                 + w2 * ob_ref[2, r, rows, :]) / (w0 + w1 + w2)
            yn_ref[pl.ds(c * QBLK * ph + r, QBLK, stride=ph), :] = y
    o_ref[...] = yn_ref[...].astype(o_ref.dtype)


def _attention(proj, q_g, k_g):
    s = proj.shape[0]
    tl = ATTN_TILE
    qc = (2 * SGU_WIDTH + 2 * CONV_CH) // HEAD_DIM
    kc = qc + ATTN_HEADS
    vc = kc + ATTN_HEADS
    return pl.pallas_call(
        _attn_kernel,
        grid=(ATTN_HEADS, s // tl),
        in_specs=[
            pl.BlockSpec((tl, HEAD_DIM), lambda h, t: (t, qc + h)),
            pl.BlockSpec((tl, HEAD_DIM), lambda h, t: (t, kc + h)),
            pl.BlockSpec((tl, HEAD_DIM), lambda h, t: (t, vc + h)),
            pl.BlockSpec((1, HEAD_DIM), lambda h, t: (0, 0)),
            pl.BlockSpec((1, HEAD_DIM), lambda h, t: (0, 0)),
        ],
        out_specs=pl.BlockSpec((tl, HEAD_DIM), lambda h, t: (t, h)),
        out_shape=jax.ShapeDtypeStruct((s, ATTN_WIDTH), BF16),
        scratch_shapes=[
            pltpu.VMEM((tl, HEAD_DIM), F32),
            pltpu.VMEM((2 * tl, HEAD_DIM), F32),
            pltpu.VMEM((2 * tl, HEAD_DIM), F32),
            pltpu.VMEM((ATTN_PH, ATTN_SL, HEAD_DIM), F32),
            pltpu.VMEM((ATTN_PH, 2 * ATTN_SL, HEAD_DIM), F32),
            pltpu.VMEM((ATTN_PH, 2 * ATTN_SL, HEAD_DIM), F32),
            pltpu.VMEM((len(DILATIONS), ATTN_PH, ATTN_SL, HEAD_DIM), F32),
            pltpu.VMEM((len(DILATIONS), ATTN_PH, ATTN_SL, HEAD_DIM), F32),
            pltpu.VMEM((2, 2, QBLK, HEAD_DIM), F32),
            pltpu.VMEM((tl, HEAD_DIM), F32),
            pltpu.VMEM((2, QBLK, 2 * QBLK), F32),
        ],
        compiler_params=pltpu.CompilerParams(
            dimension_semantics=("parallel", "arbitrary"), vmem_limit_bytes=VMEM_LIMIT),
        name="attention",
    )(proj, proj, proj, q_g.reshape(1, HEAD_DIM), k_g.reshape(1, HEAD_DIM))


def _split_bf16(x):
    hi = x.astype(BF16)
    lo = (x - hi.astype(F32)).astype(BF16)
    return hi, lo


def _pack_halves(x):
    c = x.shape[1] // 2
    lo = lax.bitcast_convert_type(x[:, :c].astype(BF16).astype(F32), U32)
    hi = lax.bitcast_convert_type(x[:, c:].astype(BF16).astype(F32), U32)
    return (lo >> 16) | hi


def _unpack_halves(u):
    lo = lax.bitcast_convert_type(u << 16, F32)
    hi = lax.bitcast_convert_type(u & jnp.uint32(0xFFFF0000), F32)
    return lo, hi


def _out_proj_kernel(yab_ref, yc_ref, onc_ref, h_ref, w_ref, n2_ref, rwh_ref, rwl_ref, rb_ref,
                     h1_ref, xp_ref, route_ref, routet_ref, cnt_ref, tri_ref, run_ref):
    tm = h_ref.shape[0]
    ts = tri_ref.shape[0]

    @pl.when(pl.program_id(0) == 0)
    def _():
        r = lax.broadcasted_iota(jnp.int32, (ts, ts), 0)
        c = lax.broadcasted_iota(jnp.int32, (ts, ts), 1)
        tri_ref[...] = jnp.where(c < r, 1.0, 0.0).astype(BF16)
        run_ref[...] = jnp.zeros(run_ref.shape, F32)

    half = yab_ref.shape[1]
    subs = [slice(p * ts, (p + 1) * ts) for p in range(tm // ts)]
    for rows in subs:
        ycn = _rms(yc_ref[rows, :].astype(F32), onc_ref[...]).astype(BF16)
        acc = jnp.dot(yab_ref[rows, :], w_ref[0:half, :], preferred_element_type=F32)
        acc = acc + jnp.dot(ycn, w_ref[half:, :], preferred_element_type=F32)
        h1_ref[rows, :] = h_ref[rows, :] + acc
    run = run_ref[...]
    for rows in subs:
        run = _route_rows(rows, run, n2_ref, rwh_ref, rwl_ref, rb_ref, h1_ref, xp_ref, route_ref,
                          routet_ref, tri_ref)
    run_ref[...] = run
    cnt_ref[...] = run


def _route_rows(rows, run, n2_ref, rwh_ref, rwl_ref, rb_ref, h1_ref, xp_ref, route_ref,
                routet_ref, tri_ref):
    xn = _rms(h1_ref[rows, :], n2_ref[...])
    xp_ref[rows, :] = _pack_halves(xn)

    xh, xl = _split_bf16(xn)
    lg = (jnp.dot(xh, rwh_ref[...], preferred_element_type=F32)
          + jnp.dot(xl, rwh_ref[...], preferred_element_type=F32)
          + jnp.dot(xh, rwl_ref[...], preferred_element_type=F32)) + rb_ref[...]

    lane = lax.broadcasted_iota(jnp.int32, lg.shape, 1)
    big = jnp.int32(LANES)
    gl = jnp.where(lane < N_GROUPS, lg, NEG)
    gmax = jnp.max(gl, axis=-1, keepdims=True)
    grp = jnp.min(jnp.where(gl == gmax, lane, big), axis=-1, keepdims=True)
    gate_g = 1.0 / jnp.sum(jnp.exp(gl - gmax), axis=-1, keepdims=True)
    lo = N_GROUPS + grp * EXPERTS_PER_GROUP
    el = jnp.where((lane >= lo) & (lane < lo + EXPERTS_PER_GROUP), lg, NEG)
    v1 = jnp.max(el, axis=-1, keepdims=True)
    i1 = jnp.min(jnp.where(el == v1, lane, big), axis=-1, keepdims=True)
    el2 = jnp.where(lane == i1, NEG, el)
    v2 = jnp.max(el2, axis=-1, keepdims=True)
    i2 = jnp.min(jnp.where(el2 == v2, lane, big), axis=-1, keepdims=True)
    e2 = jnp.exp(v2 - v1)
    g0 = gate_g / (1.0 + e2)
    g1 = gate_g * e2 / (1.0 + e2)
    oh0 = lane == i1 - N_GROUPS
    oh1 = lane == i2 - N_GROUPS
    cnt = jnp.where(oh0 | oh1, 1.0, 0.0)
    before = jnp.dot(tri_ref[...], cnt.astype(BF16), preferred_element_type=F32) + run
    r0 = jnp.sum(jnp.where(oh0, before, 0.0), axis=-1, keepdims=True)
    r1 = jnp.sum(jnp.where(oh1, before, 0.0), axis=-1, keepdims=True)

    route = (i1 - N_GROUPS).astype(F32)
    for k, val in enumerate(((i2 - N_GROUPS).astype(F32), g0, g1, r0, r1), start=1):
        route = jnp.where(lane == k, val, route)
    route = jnp.where(lane < ROUTE_LANES, route, 0.0)
    route_ref[rows, :] = route
    routet_ref[:, rows] = route.T[0:8, :]
    return run + jnp.sum(cnt, axis=0, keepdims=True)


def _out_proj(y_ab, y_c, on_c, h, w_bf16, norm2, rw, rb, tm=512, sub_rows=512):
    s, d = h.shape
    half = y_ab.shape[1]
    rw_pad = jnp.pad(rw, ((0, 0), (0, LANES - ROUTE_COLS)))
    rwh, rwl = _split_bf16(rw_pad)
    rb_pad = jnp.pad(rb, (0, LANES - ROUTE_COLS)).reshape(1, LANES)
    const = lambda i: (0, 0)
    return pl.pallas_call(
        _out_proj_kernel,
        grid=(s // tm,),
        in_specs=[
            pl.BlockSpec((tm, half), lambda i: (i, 0)),
            pl.BlockSpec((tm, half), lambda i: (i, 0)),
            pl.BlockSpec((1, half), const),
            pl.BlockSpec((tm, d), lambda i: (i, 0)),
            pl.BlockSpec((d, d), const, pipeline_mode=pl.Buffered(1)),
            pl.BlockSpec((1, d), const),
            pl.BlockSpec((d, LANES), const),
            pl.BlockSpec((d, LANES), const),
            pl.BlockSpec((1, LANES), const),
        ],
        out_specs=[
            pl.BlockSpec((tm, d), lambda i: (i, 0)),
            pl.BlockSpec((tm, d // 2), lambda i: (i, 0)),
            pl.BlockSpec((tm, LANES), lambda i: (i, 0)),
            pl.BlockSpec((8, tm), lambda i: (0, i)),
            pl.BlockSpec((1, LANES), const),
        ],
        out_shape=[
            jax.ShapeDtypeStruct((s, d), F32),
            jax.ShapeDtypeStruct((s, d // 2), U32),
            jax.ShapeDtypeStruct((s, LANES), F32),
            jax.ShapeDtypeStruct((8, s), F32),
            jax.ShapeDtypeStruct((1, LANES), F32),
        ],
        scratch_shapes=[pltpu.VMEM((sub_rows, sub_rows), BF16), pltpu.VMEM((1, LANES), F32)],
        compiler_params=pltpu.CompilerParams(
            dimension_semantics=("arbitrary",), vmem_limit_bytes=VMEM_LIMIT),
        name="out_proj",
    )(y_ab, y_c, on_c.reshape(1, half), h, w_bf16, norm2.reshape(1, d), rwh, rwl, rb_pad)


def _route_plan(route_t, cnt, t):
    nblk = t * TOP_K // MOE_BLK + N_EXPERTS
    ids = jnp.arange(N_EXPERTS, dtype=jnp.int32)
    e = route_t[0:TOP_K].astype(jnp.int32)
    rank = route_t[4:4 + TOP_K].astype(jnp.int32)
    counts = cnt[0, :N_EXPERTS].astype(jnp.int32)
    pcounts = (counts + MOE_BLK - 1) // MOE_BLK * MOE_BLK
    pends = jnp.cumsum(pcounts)
    pstarts = pends - pcounts
    dest = (jnp.sum(jnp.where(e[:, :, None] == ids, pstarts, 0), axis=-1) + rank).reshape(-1)
    blk = jnp.arange(nblk, dtype=jnp.int32)
    blk_e = jnp.minimum(jnp.sum((blk[:, None] * MOE_BLK >= pends[None, :]).astype(jnp.int32), axis=1),
                        N_EXPERTS - 1)
    n_act = pends[-1] // MOE_BLK
    mine = blk_e[:, None] == ids[None, :]
    first = (blk == 0) | (blk_e != jnp.roll(blk_e, 1))
    seg_par = (jnp.cumsum(first.astype(jnp.int32)) - 1) % 2
    nxt_blk = jnp.sum(jnp.where(mine, pends, 0), axis=1) // MOE_BLK
    nxt_e = jnp.where(nxt_blk < n_act,
                      jnp.sum(jnp.where(nxt_blk[:, None] == blk[None, :], blk_e[None, :], 0), axis=1),
                      -1)
    i32 = lambda a: a.astype(jnp.int32)
    return (i32(dest), i32(pstarts + counts), i32(pcounts - counts), i32(blk_e), i32(first),
            i32(seg_par), i32(nxt_e), i32(n_act).reshape(1), nblk)


def _dispatch_kernel(dest_ref, pad_start_ref, pad_n_ref, n_act_ref, xp_ref, xs_ref,
                     buf, zbuf, sem, zsem, *, tm, nsteps, nblk):
    i = pl.program_id(0)
    slot = i % 2

    def wait_slot(sl):
        for _ in range(TOP_K):
            pltpu.make_async_copy(buf.at[sl], xs_ref.at[pl.ds(0, tm)], sem.at[sl]).wait()

    def for_padding_copies(fn):
        sub = 8
        for e in range(N_EXPERTS):
            n = pad_n_ref[e]
            start = pad_start_ref[e]
            head = jnp.minimum((sub - (start & (sub - 1))) & (sub - 1), n)
            for j in range(sub - 1):
                @pl.when(j < head)
                def _(row=start + j):
                    fn(pltpu.make_async_copy(zbuf.at[pl.ds(0, 1)], xs_ref.at[pl.ds(row, 1)], zsem))
            n = n - head
            start = start + head
            p = MOE_BLK // 2
            while p >= sub:
                @pl.when((n & p) != 0)
                def _(start=start, p=p):
                    fn(pltpu.make_async_copy(zbuf.at[pl.ds(0, p)],
                                             xs_ref.at[pl.ds(pl.multiple_of(start, sub), p)], zsem))
                start = start + (n & p)
                p //= 2
        for j in range(nblk - nsteps * tm * TOP_K // MOE_BLK):
            blk = n_act_ref[0] + j

            @pl.when(blk < nblk)
            def _(blk=blk):
                fn(pltpu.make_async_copy(zbuf, xs_ref.at[pl.ds(blk * MOE_BLK, MOE_BLK)], zsem))

    @pl.when(i == 0)
    def _():
        zbuf[...] = jnp.zeros(zbuf.shape, zbuf.dtype)
        for_padding_copies(lambda cp: cp.start())

    @pl.when(i >= 2)
    def _():
        wait_slot(slot)

    buf[slot] = xp_ref[...]
    for rr in range(tm):
        for k in range(TOP_K):
            dst = dest_ref[k * tm * nsteps + i * tm + rr]
            pltpu.make_async_copy(buf.at[slot, pl.ds(rr, 1)], xs_ref.at[pl.ds(dst, 1)],
                                  sem.at[slot]).start()

    @pl.when(i == nsteps - 1)
    def _():
        wait_slot(slot)
        if nsteps > 1:
            wait_slot(1 - slot)
        for_padding_copies(lambda cp: cp.wait())


def _dispatch(dest, pad_start, pad_n, n_act, xp, nblk, tm=256):
    t, c = xp.shape
    nsteps = t // tm
    return pl.pallas_call(
        functools.partial(_dispatch_kernel, tm=tm, nsteps=nsteps, nblk=nblk),
        grid_spec=pltpu.PrefetchScalarGridSpec(
            num_scalar_prefetch=4,
            grid=(nsteps,),
            in_specs=[pl.BlockSpec((tm, c), lambda i, *_: (i, 0))],
            out_specs=pl.BlockSpec(memory_space=pl.ANY),
            scratch_shapes=[pltpu.VMEM((2, tm, c), U32), pltpu.VMEM((MOE_BLK, c), U32),
                            pltpu.SemaphoreType.DMA((2,)), pltpu.SemaphoreType.DMA],
        ),
        out_shape=jax.ShapeDtypeStruct((nblk * MOE_BLK, c), U32),
        compiler_params=pltpu.CompilerParams(
            dimension_semantics=("arbitrary",), vmem_limit_bytes=VMEM_LIMIT),
        name="dispatch",
    )(dest, pad_start, pad_n, n_act, xp)


def _experts_kernel(dest_ref, pad_start_ref, pad_n_ref, blk_e_ref, first_ref, par_ref, nxt_ref,
                    n_act_ref, xp_hbm, wg_hbm, wu_hbm, wd_hbm, ys_ref,
                    slot_tok, xbuf, gsem, wg_f, wu_f, wd_f, wg_b, wu_b, wd_b, wsem,
                    *, layer, n_tok):
    i = pl.program_id(0)
    n_act = n_act_ref[0]
    c = xbuf.shape[2]

    def weight_copies(e, s):
        return (pltpu.make_async_copy(wg_hbm.at[layer, e], wg_f.at[s], wsem.at[s]),
                pltpu.make_async_copy(wu_hbm.at[layer, e], wu_f.at[s], wsem.at[s]),
                pltpu.make_async_copy(wd_hbm.at[layer, e], wd_f.at[s], wsem.at[s]))

    def start_gathers(blk, sl):
        for rr in range(MOE_BLK):
            tok = slot_tok[blk * MOE_BLK + rr]
            pltpu.make_async_copy(xp_hbm.at[pl.ds(tok, 1)], xbuf.at[sl, pl.ds(rr, 1)],
                                  gsem.at[sl]).start()

    def wait_gathers(sl):
        pltpu.make_async_copy(xp_hbm.at[pl.ds(0, MOE_BLK)], xbuf.at[sl], gsem.at[sl]).wait()

    @pl.when(i == 0)
    def _():
        for e in range(N_EXPERTS):
            def fill(j, carry, e=e):
                slot_tok[pad_start_ref[e] + j] = 0
                return carry
            lax.fori_loop(0, pad_n_ref[e], fill, 0)

        unroll = 16

        def claim(chunk, carry):
            for u in range(unroll):
                a = chunk * unroll + u
                slot_tok[dest_ref[a]] = jnp.where(a >= n_tok, a - n_tok, a)
            return carry
        lax.fori_loop(0, TOP_K * n_tok // unroll, claim, 0)
        start_gathers(0, 0)

    @pl.when(i >= n_act)
    def _():
        ys_ref[...] = jnp.zeros(ys_ref.shape, ys_ref.dtype)

    @pl.when(i < n_act)
    def _():
        s = par_ref[i]

        @pl.when(first_ref[i] == 1)
        def _():
            @pl.when(i == 0)
            def _():
                for cp in weight_copies(blk_e_ref[0], 0):
                    cp.start(priority=WEIGHT_DMA_PRIORITY)

            for cp in weight_copies(0, s):
                cp.wait()
            wg_b[...] = wg_f[s].astype(BF16)
            wu_b[...] = wu_f[s].astype(BF16)
            wd_b[...] = wd_f[s].astype(BF16)

            @pl.when(nxt_ref[i] >= 0)
            def _():
                for cp in weight_copies(nxt_ref[i], 1 - s):
                    cp.start(priority=WEIGHT_DMA_PRIORITY)

        wait_gathers(i % 2)
        x_lo, x_hi = _unpack_halves(xbuf[i % 2])
        x_lo = x_lo.astype(BF16)
        x_hi = x_hi.astype(BF16)
        hg = (jnp.dot(x_lo, wg_b[0:c, :], preferred_element_type=F32)
              + jnp.dot(x_hi, wg_b[c:, :], preferred_element_type=F32))
        hu = (jnp.dot(x_lo, wu_b[0:c, :], preferred_element_type=F32)
              + jnp.dot(x_hi, wu_b[c:, :], preferred_element_type=F32))
        hb = (jax.nn.silu(hg) * hu).astype(BF16)
        ys_ref[...] = _pack_halves(jnp.dot(hb, wd_b[...], preferred_element_type=F32))
        start_gathers(jnp.minimum(i + 1, n_act - 1), (i + 1) % 2)

    @pl.when(i == n_act - 1)
    def _():
        wait_gathers((i + 1) % 2)


def _experts(xp, dest, pad_start, pad_n, blk_e, first, seg_par, nxt_e, n_act, nblk, wg, wu, wd,
             layer):
    n_tok, c = xp.shape
    _, _, d, de = wg.shape
    return pl.pallas_call(
        functools.partial(_experts_kernel, layer=layer, n_tok=n_tok),
        grid_spec=pltpu.PrefetchScalarGridSpec(
            num_scalar_prefetch=8,
            grid=(nblk,),
            in_specs=[
                pl.BlockSpec(memory_space=pl.ANY),
                pl.BlockSpec(memory_space=pl.ANY),
                pl.BlockSpec(memory_space=pl.ANY),
                pl.BlockSpec(memory_space=pl.ANY),
            ],
            out_specs=pl.BlockSpec((MOE_BLK, c), lambda i, *_: (i, 0)),
            scratch_shapes=[
                pltpu.SMEM((nblk * MOE_BLK,), jnp.int32),
                pltpu.VMEM((2, MOE_BLK, c), U32),
                pltpu.SemaphoreType.DMA((2,)),
                pltpu.VMEM((2, d, de), F32),
                pltpu.VMEM((2, d, de), F32),
                pltpu.VMEM((2, de, d), F32),
                pltpu.VMEM((d, de), BF16),
                pltpu.VMEM((d, de), BF16),
                pltpu.VMEM((de, d), BF16),
                pltpu.SemaphoreType.DMA((2,)),
            ],
        ),
        out_shape=jax.ShapeDtypeStruct((nblk * MOE_BLK, c), U32),
        compiler_params=pltpu.CompilerParams(
            dimension_semantics=("arbitrary",), vmem_limit_bytes=VMEM_LIMIT),
        name="experts",
    )(dest, pad_start, pad_n, blk_e, first, seg_par, nxt_e, n_act, xp, wg, wu, wd)


def _combine_kernel(dest_ref, h_ref, route_ref, ys_hbm, *rest, tm, nsteps, emit_next):
    if emit_next:
        gn_ref, o_ref, xn_ref, gbuf, sem = rest
    else:
        o_ref, gbuf, sem = rest
    i = pl.program_id(0)

    def start_gathers(tile, sl):
        for rr in range(tm):
            for k in range(TOP_K):
                src = dest_ref[k * tm * nsteps + tile * tm + rr]
                pltpu.make_async_copy(ys_hbm.at[pl.ds(src, 1)], gbuf.at[sl, pl.ds(k * tm + rr, 1)],
                                      sem.at[sl]).start(priority=k % 2)

    def wait_gathers(sl):
        for k in range(TOP_K):
            pltpu.make_async_copy(ys_hbm.at[pl.ds(0, tm)], gbuf.at[sl, pl.ds(k * tm, tm)],
                                  sem.at[sl]).wait()

    @pl.when(i < nsteps)
    def _():
        start_gathers(i, i % 2)

    @pl.when(i >= 1)
    def _():
        sl = (i - 1) % 2
        wait_gathers(sl)
        r = route_ref[...]
        lane = lax.broadcasted_iota(jnp.int32, r.shape, 1)
        g0 = jnp.sum(jnp.where(lane == 2, r, 0.0), axis=-1, keepdims=True)
        g1 = jnp.sum(jnp.where(lane == 3, r, 0.0), axis=-1, keepdims=True)
        a_lo, a_hi = _unpack_halves(gbuf[sl, 0:tm])
        b_lo, b_hi = _unpack_halves(gbuf[sl, tm:2 * tm])
        c = a_lo.shape[1]
        h_lo = h_ref[:, 0:c] + (a_lo * g0 + b_lo * g1)
        h_hi = h_ref[:, c:] + (a_hi * g0 + b_hi * g1)
        o_ref[:, 0:c] = h_lo
        o_ref[:, c:] = h_hi
        if emit_next:
            ms = (jnp.sum(h_lo * h_lo, axis=-1, keepdims=True)
                  + jnp.sum(h_hi * h_hi, axis=-1, keepdims=True)) / (2 * c)
            inv = lax.rsqrt(ms + EPS)
            xn_ref[:, 0:c] = (h_lo * inv * gn_ref[:, 0:c]).astype(BF16)
            xn_ref[:, c:] = (h_hi * inv * gn_ref[:, c:]).astype(BF16)


def _combine(dest, h1, route, ys, next_gain=None, tm=256):
    t, d = h1.shape
    c = ys.shape[1]
    nsteps = t // tm
    emit_next = next_gain is not None
    prev = lambda i, dst: (jnp.maximum(i - 1, 0), 0)
    in_specs = [
        pl.BlockSpec((tm, d), prev),
        pl.BlockSpec((tm, LANES), prev),
        pl.BlockSpec(memory_space=pl.ANY),
    ]
    out_specs = [pl.BlockSpec((tm, d), prev)]
    out_shape = [jax.ShapeDtypeStruct((t, d), F32)]
    args = [dest, h1, route, ys]
    if emit_next:
        in_specs.append(pl.BlockSpec((1, d), lambda i, dst: (0, 0)))
        out_specs.append(pl.BlockSpec((tm, d), prev))
        out_shape.append(jax.ShapeDtypeStruct((t, d), BF16))
        args.append(next_gain.reshape(1, d))
    return pl.pallas_call(
        functools.partial(_combine_kernel, tm=tm, nsteps=nsteps, emit_next=emit_next),
        grid_spec=pltpu.PrefetchScalarGridSpec(
            num_scalar_prefetch=1,
            grid=(nsteps + 1,),
            in_specs=in_specs,
            out_specs=out_specs,
            scratch_shapes=[pltpu.VMEM((2, TOP_K * tm, c), U32), pltpu.SemaphoreType.DMA((2,))],
        ),
        out_shape=out_shape,
        compiler_params=pltpu.CompilerParams(
            dimension_semantics=("arbitrary",), vmem_limit_bytes=VMEM_LIMIT),
        name="combine",
    )(*args)


def kernel(x, norm1, w_in, sgu_norm, sgu_w, sgu_b, conv_w, conv_b, conv_ln_g, conv_ln_b,
           q_norm, k_norm, out_norm, w_out, norm2, w_router_group, b_router_group,
           w_router_expert, b_router_expert, w_expert_gate, w_expert_up, w_expert_down):
    b, s, d = x.shape
    depth = norm1.shape[0]
    t = b * s
    h = x.reshape(t, d)
    a_end, b_end = SGU_WIDTH, SGU_WIDTH + CONV_CH
    x_in = h
    for l in range(depth):
        proj = _in_proj(x_in, norm1[l], w_in, l)
        y_ab = _mix_ab(proj, sgu_norm[l], sgu_w[l], sgu_b[l], conv_w[l], conv_b[l],
                       conv_ln_g[l], conv_ln_b[l], out_norm[l, :a_end], out_norm[l, a_end:b_end])
        y_c = _attention(proj, q_norm[l], k_norm[l])
        rw = jnp.concatenate([w_router_group[l], w_router_expert[l]], axis=1)
        rb = jnp.concatenate([b_router_group[l], b_router_expert[l]], axis=0)
        h1, xp, route, route_t, cnt = _out_proj(y_ab, y_c, out_norm[l, b_end:], h,
                                                w_out[l].astype(BF16), norm2[l], rw, rb)
        (dest, pad_start, pad_n, blk_e, first, seg_par, nxt_e, n_act,
         nblk) = _route_plan(route_t, cnt, t)
        ys = _experts(xp, dest, pad_start, pad_n, blk_e, first, seg_par, nxt_e, n_act, nblk,
                      w_expert_gate, w_expert_up, w_expert_down, l)
        if l + 1 < depth:
            h, x_in = _combine(dest, h1, route, ys, next_gain=norm1[l + 1])
        else:
            (h,) = _combine(dest, h1, route, ys)
    return h.reshape(b, s, d)
```

```python
import functools

import jax
import jax.numpy as jnp
from jax import lax
from jax.experimental import pallas as pl
from jax.experimental.pallas import tpu as pltpu

F32 = jnp.float32
BF16 = jnp.bfloat16
U32 = jnp.uint32

D_MODEL = 2048
SGU_WIDTH = 512
SGU_HEADS = 4
SGU_CHUNK = 128
CONV_CH = 512
CONV_K = 31
ATTN_WIDTH = 1024
ATTN_HEADS = 8
HEAD_DIM = 128
DILATIONS = (1, 4, 16)
QBLK = 128
D_IN = 2 * SGU_WIDTH + 2 * CONV_CH + 3 * ATTN_WIDTH
N_GROUPS = 4
EXPERTS_PER_GROUP = 8
N_EXPERTS = 32
TOP_K = 2
D_EXPERT = 512
EPS = 1e-6

LANES = 128
VMEM_LIMIT = 56 * 1024 * 1024
NEG = -1e30

ATTN_TILE = QBLK * max(DILATIONS)
CONV_HALO = 32
MOE_BLK = 256
WEIGHT_DMA_PRIORITY = 1
GATHER_AHEAD = 2
ROUTE_COLS = N_GROUPS + N_EXPERTS
ROUTE_LANES = 6


def _rms(x, g):
    return x * lax.rsqrt(jnp.mean(x * x, axis=-1, keepdims=True) + EPS) * g


def _in_proj_norm_kernel(x_ref, g_ref, w_ref, o_ref, xn_ref):
    @pl.when(pl.program_id(1) == 0)
    def _():
        xn_ref[...] = _rms(x_ref[...], g_ref[...]).astype(BF16)

    o_ref[...] = jnp.dot(xn_ref[...], w_ref[...].astype(BF16),
                         preferred_element_type=F32).astype(o_ref.dtype)


def _in_proj_kernel(xn_ref, w_ref, o_ref):
    o_ref[...] = jnp.dot(xn_ref[...], w_ref[...].astype(BF16),
                         preferred_element_type=F32).astype(o_ref.dtype)


def _in_proj(x, g, w_in, layer, tn=1024):
    s, d = x.shape
    n = w_in.shape[2]
    prenormed = x.dtype == BF16
    tm = 2048 if prenormed else 1024
    x_spec = pl.BlockSpec((tm, d), lambda i, j: (i, 0))
    w_spec = pl.BlockSpec((None, d, tn), lambda i, j: (layer, 0, j))
    common = dict(
        grid=(s // tm, n // tn),
        out_specs=pl.BlockSpec((tm, tn), lambda i, j: (i, j)),
        out_shape=jax.ShapeDtypeStruct((s, n), BF16),
        compiler_params=pltpu.CompilerParams(
            dimension_semantics=("parallel", "arbitrary"), vmem_limit_bytes=VMEM_LIMIT),
        name="in_proj",
    )
    if prenormed:
        return pl.pallas_call(_in_proj_kernel, in_specs=[x_spec, w_spec], **common)(x, w_in)
    return pl.pallas_call(
        _in_proj_norm_kernel,
        in_specs=[x_spec, pl.BlockSpec((1, d), lambda i, j: (0, 0)), w_spec],
        scratch_shapes=[pltpu.VMEM((tm, d), BF16)],
        **common,
    )(x, g.reshape(1, d), w_in)


def _mix_ab_kernel(u_ref, v_ref, a_ref, gt_ref, ah_ref, gh_ref,
                   sgn_ref, sw_ref, sb_ref, cw_ref, cb_ref, lng_ref, lnb_ref,
                   ona_ref, onb_ref, o_ref, zext_ref, conv_ref, stage_ref, *, tq):
    u = jax.nn.gelu(u_ref[...].astype(F32))
    v = _rms(jax.nn.gelu(v_ref[...].astype(F32)), sgn_ref[...]).astype(BF16)
    row = lax.broadcasted_iota(jnp.int32, (SGU_CHUNK, SGU_CHUNK), 0)
    col = lax.broadcasted_iota(jnp.int32, (SGU_CHUNK, SGU_CHUNK), 1)
    causal = col <= row
    wm = [jnp.where(causal, sw_ref[hh], 0.0).astype(BF16) for hh in range(SGU_HEADS)]
    for c in range(tq // SGU_CHUNK):
        rows = slice(c * SGU_CHUNK, (c + 1) * SGU_CHUNK)
        zs = []
        for hh in range(SGU_HEADS):
            cols = slice(hh * HEAD_DIM, (hh + 1) * HEAD_DIM)
            zs.append(jnp.dot(wm[hh], v[rows, cols], preferred_element_type=F32) + sb_ref[hh])
        ya = u[rows, :] * jnp.concatenate(zs, axis=1)
        o_ref[rows, 0:SGU_WIDTH] = _rms(ya, ona_ref[...]).astype(o_ref.dtype)

    first = pl.program_id(0) == 0
    zh = ah_ref[...].astype(F32) * jax.nn.sigmoid(gh_ref[...].astype(F32))
    zext_ref[0:CONV_HALO, :] = jnp.where(first, 0.0, zh)
    zext_ref[CONV_HALO:, :] = a_ref[...].astype(F32) * jax.nn.sigmoid(gt_ref[...].astype(F32))
    off0 = CONV_HALO - (CONV_K - 1)
    sub = 8
    rc = 128
    for j in range(CONV_CH // LANES):
        cols = slice(j * LANES, (j + 1) * LANES)
        for c in range(tq // rc):
            acc = jnp.zeros((rc, LANES), F32) + cb_ref[:, cols]
            for b in range(sub):
                taps = [(a, sub * a + b - off0) for a in range(CONV_HALO // sub + 1)
                        if 0 <= sub * a + b - off0 < CONV_K]
                rows = rc + sub * taps[-1][0]
                stage_ref[b, 0:rows, :] = zext_ref[pl.ds(c * rc + b, rows), cols]
                for a, k in taps:
                    acc = acc + stage_ref[b, sub * a:sub * a + rc, :] * cw_ref[k:k + 1, cols]
            conv_ref[c * rc:(c + 1) * rc, cols] = acc
    rc = 32
    for c in range(tq // rc):
        acc = conv_ref[c * rc:(c + 1) * rc, :]
        mu = jnp.mean(acc, axis=-1, keepdims=True)
        xc = acc - mu
        y = xc * lax.rsqrt(jnp.mean(xc * xc, axis=-1, keepdims=True) + EPS)
        y = jax.nn.silu(y * lng_ref[...] + lnb_ref[...])
        o_ref[c * rc:(c + 1) * rc, SGU_WIDTH:] = _rms(y, onb_ref[...]).astype(o_ref.dtype)


def _mix_ab(proj, sgu_norm, sgu_w, sgu_b, conv_w, conv_b, ln_g, ln_b, on_a, on_b, tq=512):
    s = proj.shape[0]
    w = SGU_WIDTH
    hb = tq // CONV_HALO
    sb = jnp.broadcast_to(sgu_b[:, :, None], (SGU_HEADS, SGU_CHUNK, HEAD_DIM))
    cw = jnp.pad(conv_w, ((0, 32 - CONV_K), (0, 0)))
    vec = lambda a: a.reshape(1, -1)
    const2 = lambda i: (0, 0)
    const3 = lambda i: (0, 0, 0)
    return pl.pallas_call(
        functools.partial(_mix_ab_kernel, tq=tq),
        grid=(s // tq,),
        in_specs=[
            pl.BlockSpec((tq, w), lambda i: (i, 0)),
            pl.BlockSpec((tq, w), lambda i: (i, 1)),
            pl.BlockSpec((tq, w), lambda i: (i, 2)),
            pl.BlockSpec((tq, w), lambda i: (i, 3)),
            pl.BlockSpec((CONV_HALO, w), lambda i: (jnp.maximum(i * hb - 1, 0), 2)),
            pl.BlockSpec((CONV_HALO, w), lambda i: (jnp.maximum(i * hb - 1, 0), 3)),
            pl.BlockSpec((1, w), const2),
            pl.BlockSpec((SGU_HEADS, SGU_CHUNK, SGU_CHUNK), const3),
            pl.BlockSpec((SGU_HEADS, SGU_CHUNK, HEAD_DIM), const3),
            pl.BlockSpec((32, w), const2),
            pl.BlockSpec((1, w), const2),
            pl.BlockSpec((1, w), const2),
            pl.BlockSpec((1, w), const2),
            pl.BlockSpec((1, w), const2),
            pl.BlockSpec((1, w), const2),
        ],
        out_specs=pl.BlockSpec((tq, 2 * w), lambda i: (i, 0)),
        out_shape=jax.ShapeDtypeStruct((s, 2 * w), BF16),
        scratch_shapes=[pltpu.VMEM((tq + CONV_HALO, w), F32), pltpu.VMEM((tq, w), F32),
                        pltpu.VMEM((8, SGU_CHUNK + CONV_HALO, LANES), F32)],
        compiler_params=pltpu.CompilerParams(
            dimension_semantics=("parallel",), vmem_limit_bytes=VMEM_LIMIT),
        name="mix_ab",
    )(proj, proj, proj, proj, proj, proj, vec(sgu_norm), sgu_w, sb, cw, vec(conv_b),
      vec(ln_g), vec(ln_b), vec(on_a), vec(on_b))


ATTN_PH = 4
ATTN_SL = ATTN_TILE // ATTN_PH


def _attn_kernel(q_ref, k_ref, v_ref, qg_ref, kg_ref, o_ref,
                 qn_ref, kn_ref, vv_ref, q4_ref, k4_ref, v4_ref, ob_ref, lb_ref, st_ref, yn_ref,
                 bias_ref):
    t = pl.program_id(1)
    tl, ph, sl = ATTN_TILE, ATTN_PH, ATTN_SL

    @pl.when(t == 0)
    def _():
        kn_ref[0:tl, :] = jnp.zeros((tl, HEAD_DIM), F32)
        vv_ref[0:tl, :] = jnp.zeros((tl, HEAD_DIM), F32)
        k4_ref[:, 0:sl, :] = jnp.zeros((ph, sl, HEAD_DIM), F32)
        v4_ref[:, 0:sl, :] = jnp.zeros((ph, sl, HEAD_DIM), F32)

    @pl.when(t > 0)
    def _():
        kn_ref[0:tl, :] = kn_ref[tl:, :]
        vv_ref[0:tl, :] = vv_ref[tl:, :]
        k4_ref[:, 0:sl, :] = k4_ref[:, sl:, :]
        v4_ref[:, 0:sl, :] = v4_ref[:, sl:, :]

    kn_ref[tl:, :] = _rms(k_ref[...].astype(F32), kg_ref[...])
    vv_ref[tl:, :] = v_ref[...].astype(F32)
    qn_ref[...] = _rms(q_ref[...].astype(F32), qg_ref[...]) * (HEAD_DIM ** -0.5)
    for r in range(ph):
        q4_ref[r] = qn_ref[pl.ds(r, sl, stride=ph), :]
        k4_ref[r, sl:, :] = kn_ref[pl.ds(tl + r, sl, stride=ph), :]
        v4_ref[r, sl:, :] = vv_ref[pl.ds(tl + r, sl, stride=ph), :]

    qi = lax.broadcasted_iota(jnp.int32, (QBLK, 2 * QBLK), 0)
    kj = lax.broadcasted_iota(jnp.int32, (QBLK, 2 * QBLK), 1)
    band = (kj >= qi) & (kj <= qi + QBLK)
    bias_ref[0] = jnp.where(band, 0.0, NEG)
    bias_ref[1] = jnp.where(band & (kj >= jnp.where(t > 0, 0, QBLK)), 0.0, NEG)

    def attend(qb, kb, vb, first):
        sc = lax.dot_general(qb.astype(BF16), kb.astype(BF16), (((1,), (1,)), ((), ())),
                             preferred_element_type=F32)
        sc = sc + bias_ref[1 if first else 0]
        m = jnp.max(sc, axis=-1, keepdims=True)
        p = jnp.exp(sc - m)
        l = jnp.sum(p, axis=-1, keepdims=True)
        ob = jnp.dot(p.astype(BF16), vb.astype(BF16), preferred_element_type=F32) / l
        return ob, jnp.broadcast_to(m + jnp.log(l), (QBLK, HEAD_DIM))

    for b in range(tl // QBLK):
        ob, lse = attend(qn_ref[b * QBLK:(b + 1) * QBLK, :],
                         kn_ref[tl + (b - 1) * QBLK:tl + (b + 1) * QBLK, :],
                         vv_ref[tl + (b - 1) * QBLK:tl + (b + 1) * QBLK, :], b == 0)
        s = b % 2
        st_ref[s, 0] = ob
        st_ref[s, 1] = lse
        n = QBLK // ph
        for r in range(ph):
            ob_ref[0, r, b * n:(b + 1) * n, :] = st_ref[s, 0, pl.ds(r, n, stride=ph), :]
            lb_ref[0, r, b * n:(b + 1) * n, :] = st_ref[s, 1, pl.ds(r, n, stride=ph), :]

    for r in range(ph):
        for b in range(sl // QBLK):
            ob, lse = attend(q4_ref[r, b * QBLK:(b + 1) * QBLK, :],
                             k4_ref[r, sl + (b - 1) * QBLK:sl + (b + 1) * QBLK, :],
                             v4_ref[r, sl + (b - 1) * QBLK:sl + (b + 1) * QBLK, :], b == 0)
            ob_ref[1, r, b * QBLK:(b + 1) * QBLK, :] = ob
            lb_ref[1, r, b * QBLK:(b + 1) * QBLK, :] = lse
        for a in range(ph):
            ob, lse = attend(q4_ref[r, pl.ds(a, QBLK, stride=ph), :],
                             k4_ref[r, pl.ds(a, 2 * QBLK, stride=ph), :],
                             v4_ref[r, pl.ds(a, 2 * QBLK, stride=ph), :], True)
            ob_ref[2, r, pl.ds(a, QBLK, stride=ph), :] = ob
            lb_ref[2, r, pl.ds(a, QBLK, stride=ph), :] = lse

    for r in range(ph):
        for c in range(sl // QBLK):
            rows = slice(c * QBLK, (c + 1) * QBLK)
            l0, l1, l2 = lb_ref[0, r, rows, :], lb_ref[1, r, rows, :], lb_ref[2, r, rows, :]
            mx = jnp.maximum(jnp.maximum(l0, l1), l2)
            w0, w1, w2 = jnp.exp(l0 - mx), jnp.exp(l1 - mx), jnp.exp(l2 - mx)
            y = (w0 * ob_ref[0, r, rows, :] + w1 * ob_ref[1, r, rows, :]
                 + w2 * ob_ref[2, r, rows, :]) / (w0 + w1 + w2)
            yn_ref[pl.ds(c * QBLK * ph + r, QBLK, stride=ph), :] = y
    o_ref[...] = yn_ref[...].astype(o_ref.dtype)


def _attention(proj, q_g, k_g):
    s = proj.shape[0]
    tl = ATTN_TILE
    qc = (2 * SGU_WIDTH + 2 * CONV_CH) // HEAD_DIM
    kc = qc + ATTN_HEADS
    vc = kc + ATTN_HEADS
    return pl.pallas_call(
        _attn_kernel,
        grid=(ATTN_HEADS, s // tl),
        in_specs=[
            pl.BlockSpec((tl, HEAD_DIM), lambda h, t: (t, qc + h)),
            pl.BlockSpec((tl, HEAD_DIM), lambda h, t: (t, kc + h)),
            pl.BlockSpec((tl, HEAD_DIM), lambda h, t: (t, vc + h)),
            pl.BlockSpec((1, HEAD_DIM), lambda h, t: (0, 0)),
            pl.BlockSpec((1, HEAD_DIM), lambda h, t: (0, 0)),
        ],
        out_specs=pl.BlockSpec((tl, HEAD_DIM), lambda h, t: (t, h)),
        out_shape=jax.ShapeDtypeStruct((s, ATTN_WIDTH), BF16),
        scratch_shapes=[
            pltpu.VMEM((tl, HEAD_DIM), F32),
            pltpu.VMEM((2 * tl, HEAD_DIM), F32),
            pltpu.VMEM((2 * tl, HEAD_DIM), F32),
            pltpu.VMEM((ATTN_PH, ATTN_SL, HEAD_DIM), F32),
            pltpu.VMEM((ATTN_PH, 2 * ATTN_SL, HEAD_DIM), F32),
            pltpu.VMEM((ATTN_PH, 2 * ATTN_SL, HEAD_DIM), F32),
            pltpu.VMEM((len(DILATIONS), ATTN_PH, ATTN_SL, HEAD_DIM), F32),
            pltpu.VMEM((len(DILATIONS), ATTN_PH, ATTN_SL, HEAD_DIM), F32),
            pltpu.VMEM((2, 2, QBLK, HEAD_DIM), F32),
            pltpu.VMEM((tl, HEAD_DIM), F32),
            pltpu.VMEM((2, QBLK, 2 * QBLK), F32),
        ],
        compiler_params=pltpu.CompilerParams(
            dimension_semantics=("parallel", "arbitrary"), vmem_limit_bytes=VMEM_LIMIT),
        name="attention",
    )(proj, proj, proj, q_g.reshape(1, HEAD_DIM), k_g.reshape(1, HEAD_DIM))


def _split_bf16(x):
    hi = x.astype(BF16)
    lo = (x - hi.astype(F32)).astype(BF16)
    return hi, lo


def _pack_halves(x):
    c = x.shape[1] // 2
    lo = lax.bitcast_convert_type(x[:, :c].astype(BF16).astype(F32), U32)
    hi = lax.bitcast_convert_type(x[:, c:].astype(BF16).astype(F32), U32)
    return (lo >> 16) | hi


def _unpack_halves(u):
    lo = lax.bitcast_convert_type(u << 16, F32)
    hi = lax.bitcast_convert_type(u & jnp.uint32(0xFFFF0000), F32)
    return lo, hi


def _out_proj_kernel(yab_ref, yc_ref, onc_ref, h_ref, w_ref, n2_ref, rwh_ref, rwl_ref, rb_ref,
                     h1_ref, xp_ref, route_ref, routet_ref, cnt_ref, tri_ref, run_ref):
    tm = h_ref.shape[0]
    ts = tri_ref.shape[0]

    @pl.when(pl.program_id(0) == 0)
    def _():
        r = lax.broadcasted_iota(jnp.int32, (ts, ts), 0)
        c = lax.broadcasted_iota(jnp.int32, (ts, ts), 1)
        tri_ref[...] = jnp.where(c < r, 1.0, 0.0).astype(BF16)
        run_ref[...] = jnp.zeros(run_ref.shape, F32)

    half = yab_ref.shape[1]
    subs = [slice(p * ts, (p + 1) * ts) for p in range(tm // ts)]
    for rows in subs:
        ycn = _rms(yc_ref[rows, :].astype(F32), onc_ref[...]).astype(BF16)
        acc = jnp.dot(yab_ref[rows, :], w_ref[0:half, :], preferred_element_type=F32)
        acc = acc + jnp.dot(ycn, w_ref[half:, :], preferred_element_type=F32)
        h1_ref[rows, :] = h_ref[rows, :] + acc
    run = run_ref[...]
    for rows in subs:
        run = _route_rows(rows, run, n2_ref, rwh_ref, rwl_ref, rb_ref, h1_ref, xp_ref, route_ref,
                          routet_ref, tri_ref)
    run_ref[...] = run
    cnt_ref[...] = run


def _route_rows(rows, run, n2_ref, rwh_ref, rwl_ref, rb_ref, h1_ref, xp_ref, route_ref,
                routet_ref, tri_ref):
    xn = _rms(h1_ref[rows, :], n2_ref[...])
    xp_ref[rows, :] = _pack_halves(xn)

    xh, xl = _split_bf16(xn)
    lg = (jnp.dot(xh, rwh_ref[...], preferred_element_type=F32)
          + jnp.dot(xl, rwh_ref[...], preferred_element_type=F32)
          + jnp.dot(xh, rwl_ref[...], preferred_element_type=F32)) + rb_ref[...]

    lane = lax.broadcasted_iota(jnp.int32, lg.shape, 1)
    big = jnp.int32(LANES)
    gl = jnp.where(lane < N_GROUPS, lg, NEG)
    gmax = jnp.max(gl, axis=-1, keepdims=True)
    grp = jnp.min(jnp.where(gl == gmax, lane, big), axis=-1, keepdims=True)
    gate_g = 1.0 / jnp.sum(jnp.exp(gl - gmax), axis=-1, keepdims=True)
    lo = N_GROUPS + grp * EXPERTS_PER_GROUP
    el = jnp.where((lane >= lo) & (lane < lo + EXPERTS_PER_GROUP), lg, NEG)
    v1 = jnp.max(el, axis=-1, keepdims=True)
    i1 = jnp.min(jnp.where(el == v1, lane, big), axis=-1, keepdims=True)
    el2 = jnp.where(lane == i1, NEG, el)
    v2 = jnp.max(el2, axis=-1, keepdims=True)
    i2 = jnp.min(jnp.where(el2 == v2, lane, big), axis=-1, keepdims=True)
    e2 = jnp.exp(v2 - v1)
    g0 = gate_g / (1.0 + e2)
    g1 = gate_g * e2 / (1.0 + e2)
    oh0 = lane == i1 - N_GROUPS
    oh1 = lane == i2 - N_GROUPS
    cnt = jnp.where(oh0 | oh1, 1.0, 0.0)
    before = jnp.dot(tri_ref[...], cnt.astype(BF16), preferred_element_type=F32) + run
    r0 = jnp.sum(jnp.where(oh0, before, 0.0), axis=-1, keepdims=True)
    r1 = jnp.sum(jnp.where(oh1, before, 0.0), axis=-1, keepdims=True)

    route = (i1 - N_GROUPS).astype(F32)
    for k, val in enumerate(((i2 - N_GROUPS).astype(F32), g0, g1, r0, r1), start=1):
        route = jnp.where(lane == k, val, route)
    route = jnp.where(lane < ROUTE_LANES, route, 0.0)
    route_ref[rows, :] = route
    routet_ref[:, rows] = route.T[0:8, :]
    return run + jnp.sum(cnt, axis=0, keepdims=True)


def _out_proj(y_ab, y_c, on_c, h, w_bf16, norm2, rw, rb, tm=512, sub_rows=512):
    s, d = h.shape
    half = y_ab.shape[1]
    rw_pad = jnp.pad(rw, ((0, 0), (0, LANES - ROUTE_COLS)))
    rwh, rwl = _split_bf16(rw_pad)
    rb_pad = jnp.pad(rb, (0, LANES - ROUTE_COLS)).reshape(1, LANES)
    const = lambda i: (0, 0)
    return pl.pallas_call(
        _out_proj_kernel,
        grid=(s // tm,),
        in_specs=[
            pl.BlockSpec((tm, half), lambda i: (i, 0)),
            pl.BlockSpec((tm, half), lambda i: (i, 0)),
            pl.BlockSpec((1, half), const),
            pl.BlockSpec((tm, d), lambda i: (i, 0)),
            pl.BlockSpec((d, d), const, pipeline_mode=pl.Buffered(1)),
            pl.BlockSpec((1, d), const),
            pl.BlockSpec((d, LANES), const),
            pl.BlockSpec((d, LANES), const),
            pl.BlockSpec((1, LANES), const),
        ],
        out_specs=[
            pl.BlockSpec((tm, d), lambda i: (i, 0)),
            pl.BlockSpec((tm, d // 2), lambda i: (i, 0)),
            pl.BlockSpec((tm, LANES), lambda i: (i, 0)),
            pl.BlockSpec((8, tm), lambda i: (0, i)),
            pl.BlockSpec((1, LANES), const),
        ],
        out_shape=[
            jax.ShapeDtypeStruct((s, d), F32),
            jax.ShapeDtypeStruct((s, d // 2), U32),
            jax.ShapeDtypeStruct((s, LANES), F32),
            jax.ShapeDtypeStruct((8, s), F32),
            jax.ShapeDtypeStruct((1, LANES), F32),
        ],
        scratch_shapes=[pltpu.VMEM((sub_rows, sub_rows), BF16), pltpu.VMEM((1, LANES), F32)],
        compiler_params=pltpu.CompilerParams(
            dimension_semantics=("arbitrary",), vmem_limit_bytes=VMEM_LIMIT),
        name="out_proj",
    )(y_ab, y_c, on_c.reshape(1, half), h, w_bf16, norm2.reshape(1, d), rwh, rwl, rb_pad)


def _route_plan(route_t, cnt, t):
    nblk = t * TOP_K // MOE_BLK + N_EXPERTS
    ids = jnp.arange(N_EXPERTS, dtype=jnp.int32)
    e = route_t[0:TOP_K].astype(jnp.int32)
    rank = route_t[4:4 + TOP_K].astype(jnp.int32)
    counts = cnt[0, :N_EXPERTS].astype(jnp.int32)
    pcounts = (counts + MOE_BLK - 1) // MOE_BLK * MOE_BLK
    pends = jnp.cumsum(pcounts)
    pstarts = pends - pcounts
    dest = (jnp.sum(jnp.where(e[:, :, None] == ids, pstarts, 0), axis=-1) + rank).reshape(-1)
    blk = jnp.arange(nblk, dtype=jnp.int32)
    blk_e = jnp.minimum(jnp.sum((blk[:, None] * MOE_BLK >= pends[None, :]).astype(jnp.int32), axis=1),
                        N_EXPERTS - 1)
    n_act = pends[-1] // MOE_BLK
    mine = blk_e[:, None] == ids[None, :]
    first = (blk == 0) | (blk_e != jnp.roll(blk_e, 1))
    seg_par = (jnp.cumsum(first.astype(jnp.int32)) - 1) % 2
    nxt_blk = jnp.sum(jnp.where(mine, pends, 0), axis=1) // MOE_BLK
    nxt_e = jnp.where(nxt_blk < n_act,
                      jnp.sum(jnp.where(nxt_blk[:, None] == blk[None, :], blk_e[None, :], 0), axis=1),
                      -1)
    i32 = lambda a: a.astype(jnp.int32)
    return (i32(dest), i32(pstarts + counts), i32(pcounts - counts), i32(blk_e), i32(first),
            i32(seg_par), i32(nxt_e), i32(n_act).reshape(1), nblk)


def _dispatch_kernel(dest_ref, pad_start_ref, pad_n_ref, n_act_ref, xp_ref, xs_ref,
                     buf, zbuf, sem, zsem, *, tm, nsteps, nblk):
    i = pl.program_id(0)
    slot = i % 2

    def wait_slot(sl):
        for _ in range(TOP_K):
            pltpu.make_async_copy(buf.at[sl], xs_ref.at[pl.ds(0, tm)], sem.at[sl]).wait()

    def for_padding_copies(fn):
        sub = 8
        for e in range(N_EXPERTS):
            n = pad_n_ref[e]
            start = pad_start_ref[e]
            head = jnp.minimum((sub - (start & (sub - 1))) & (sub - 1), n)
            for j in range(sub - 1):
                @pl.when(j < head)
                def _(row=start + j):
                    fn(pltpu.make_async_copy(zbuf.at[pl.ds(0, 1)], xs_ref.at[pl.ds(row, 1)], zsem))
            n = n - head
            start = start + head
            p = MOE_BLK // 2
            while p >= sub:
                @pl.when((n & p) != 0)
                def _(start=start, p=p):
                    fn(pltpu.make_async_copy(zbuf.at[pl.ds(0, p)],
                                             xs_ref.at[pl.ds(pl.multiple_of(start, sub), p)], zsem))
                start = start + (n & p)
                p //= 2
        for j in range(nblk - nsteps * tm * TOP_K // MOE_BLK):
            blk = n_act_ref[0] + j

            @pl.when(blk < nblk)
            def _(blk=blk):
                fn(pltpu.make_async_copy(zbuf, xs_ref.at[pl.ds(blk * MOE_BLK, MOE_BLK)], zsem))

    @pl.when(i == 0)
    def _():
        zbuf[...] = jnp.zeros(zbuf.shape, zbuf.dtype)
        for_padding_copies(lambda cp: cp.start())

    @pl.when(i >= 2)
    def _():
        wait_slot(slot)

    buf[slot] = xp_ref[...]
    for rr in range(tm):
        for k in range(TOP_K):
            dst = dest_ref[k * tm * nsteps + i * tm + rr]
            pltpu.make_async_copy(buf.at[slot, pl.ds(rr, 1)], xs_ref.at[pl.ds(dst, 1)],
                                  sem.at[slot]).start()

    @pl.when(i == nsteps - 1)
    def _():
        wait_slot(slot)
        if nsteps > 1:
            wait_slot(1 - slot)
        for_padding_copies(lambda cp: cp.wait())


def _dispatch(dest, pad_start, pad_n, n_act, xp, nblk, tm=256):
    t, c = xp.shape
    nsteps = t // tm
    return pl.pallas_call(
        functools.partial(_dispatch_kernel, tm=tm, nsteps=nsteps, nblk=nblk),
        grid_spec=pltpu.PrefetchScalarGridSpec(
            num_scalar_prefetch=4,
            grid=(nsteps,),
            in_specs=[pl.BlockSpec((tm, c), lambda i, *_: (i, 0))],
            out_specs=pl.BlockSpec(memory_space=pl.ANY),
            scratch_shapes=[pltpu.VMEM((2, tm, c), U32), pltpu.VMEM((MOE_BLK, c), U32),
                            pltpu.SemaphoreType.DMA((2,)), pltpu.SemaphoreType.DMA],
        ),
        out_shape=jax.ShapeDtypeStruct((nblk * MOE_BLK, c), U32),
        compiler_params=pltpu.CompilerParams(
            dimension_semantics=("arbitrary",), vmem_limit_bytes=VMEM_LIMIT),
        name="dispatch",
    )(dest, pad_start, pad_n, n_act, xp)


def _experts_kernel(dest_ref, last_blk_ref, blk_e_ref, first_ref, par_ref, nxt_ref,
                    n_act_ref, xp_hbm, wg_hbm, wu_hbm, wd_hbm, ys_ref,
                    slot_tok, xbuf, gsem, wg_f, wu_f, wd_f, wg_b, wu_b, wd_b, wsem,
                    *, layer, n_tok):
    i = pl.program_id(0)
    n_act = n_act_ref[0]
    c = xbuf.shape[2]

    def weight_copies(e, s):
        return (pltpu.make_async_copy(wg_hbm.at[layer, e], wg_f.at[s], wsem.at[s]),
                pltpu.make_async_copy(wu_hbm.at[layer, e], wu_f.at[s], wsem.at[s]),
                pltpu.make_async_copy(wd_hbm.at[layer, e], wd_f.at[s], wsem.at[s]))

    def start_gathers(blk, sl):
        for rr in range(MOE_BLK):
            tok = slot_tok[blk * MOE_BLK + rr]
            pltpu.make_async_copy(xp_hbm.at[pl.ds(tok, 1)], xbuf.at[sl, pl.ds(rr, 1)],
                                  gsem.at[sl]).start()

    def wait_gathers(sl):
        pltpu.make_async_copy(xp_hbm.at[pl.ds(0, MOE_BLK)], xbuf.at[sl], gsem.at[sl]).wait()

    @pl.when(i == 0)
    def _():
        unroll = 16
        for e in range(N_EXPERTS):
            def fill(chunk, carry, e=e):
                for u in range(unroll):
                    slot_tok[last_blk_ref[e] + chunk * unroll + u] = 0
                return carry
            lax.fori_loop(0, MOE_BLK // unroll, fill, 0)
        for k in range(TOP_K):
            def claim(chunk, carry, k=k):
                for u in range(unroll):
                    tok = chunk * unroll + u
                    slot_tok[dest_ref[k * n_tok + tok]] = tok
                return carry
            lax.fori_loop(0, n_tok // unroll, claim, 0)
        for ahead in range(GATHER_AHEAD):
            start_gathers(jnp.minimum(ahead, n_act - 1), ahead)

    @pl.when(i >= n_act)
    def _():
        ys_ref[...] = jnp.zeros(ys_ref.shape, ys_ref.dtype)

    @pl.when(i < n_act)
    def _():
        s = par_ref[i]

        @pl.when(first_ref[i] == 1)
        def _():
            @pl.when(i == 0)
            def _():
                for cp in weight_copies(blk_e_ref[0], 0):
                    cp.start(priority=WEIGHT_DMA_PRIORITY)

            for cp in weight_copies(0, s):
                cp.wait()
            wg_b[...] = wg_f[s].astype(BF16)
            wu_b[...] = wu_f[s].astype(BF16)
            wd_b[...] = wd_f[s].astype(BF16)

            @pl.when(nxt_ref[i] >= 0)
            def _():
                for cp in weight_copies(nxt_ref[i], 1 - s):
                    cp.start(priority=WEIGHT_DMA_PRIORITY)

        slots = GATHER_AHEAD + 1
        wait_gathers(i % slots)
        x_lo, x_hi = _unpack_halves(xbuf[i % slots])
        x_lo = x_lo.astype(BF16)
        x_hi = x_hi.astype(BF16)
        hg = (jnp.dot(x_lo, wg_b[0:c, :], preferred_element_type=F32)
              + jnp.dot(x_hi, wg_b[c:, :], preferred_element_type=F32))
        hu = (jnp.dot(x_lo, wu_b[0:c, :], preferred_element_type=F32)
              + jnp.dot(x_hi, wu_b[c:, :], preferred_element_type=F32))
        hb = (jax.nn.silu(hg) * hu).astype(BF16)
        ys_ref[...] = _pack_halves(jnp.dot(hb, wd_b[...], preferred_element_type=F32))
        start_gathers(jnp.minimum(i + GATHER_AHEAD, n_act - 1), (i + GATHER_AHEAD) % slots)

    @pl.when(i == n_act - 1)
    def _():
        for ahead in range(1, GATHER_AHEAD + 1):
            wait_gathers((i + ahead) % (GATHER_AHEAD + 1))


def _experts(xp, dest, last_blk, blk_e, first, seg_par, nxt_e, n_act, nblk, wg, wu, wd, layer):
    n_tok, c = xp.shape
    _, _, d, de = wg.shape
    return pl.pallas_call(
        functools.partial(_experts_kernel, layer=layer, n_tok=n_tok),
        grid_spec=pltpu.PrefetchScalarGridSpec(
            num_scalar_prefetch=7,
            grid=(nblk,),
            in_specs=[
                pl.BlockSpec(memory_space=pl.ANY),
                pl.BlockSpec(memory_space=pl.ANY),
                pl.BlockSpec(memory_space=pl.ANY),
                pl.BlockSpec(memory_space=pl.ANY),
            ],
            out_specs=pl.BlockSpec((MOE_BLK, c), lambda i, *_: (i, 0)),
            scratch_shapes=[
                pltpu.SMEM((nblk * MOE_BLK,), jnp.int32),
                pltpu.VMEM((GATHER_AHEAD + 1, MOE_BLK, c), U32),
                pltpu.SemaphoreType.DMA((GATHER_AHEAD + 1,)),
                pltpu.VMEM((2, d, de), F32),
                pltpu.VMEM((2, d, de), F32),
                pltpu.VMEM((2, de, d), F32),
                pltpu.VMEM((d, de), BF16),
                pltpu.VMEM((d, de), BF16),
                pltpu.VMEM((de, d), BF16),
                pltpu.SemaphoreType.DMA((2,)),
            ],
        ),
        out_shape=jax.ShapeDtypeStruct((nblk * MOE_BLK, c), U32),
        compiler_params=pltpu.CompilerParams(
            dimension_semantics=("arbitrary",), vmem_limit_bytes=VMEM_LIMIT),
        name="experts",
    )(dest, last_blk, blk_e, first, seg_par, nxt_e, n_act, xp, wg, wu, wd)


def _combine_kernel(dest_ref, h_ref, route_ref, ys_hbm, *rest, tm, nsteps, emit_next):
    if emit_next:
        gn_ref, o_ref, xn_ref, gbuf, sem = rest
    else:
        o_ref, gbuf, sem = rest
    i = pl.program_id(0)

    def start_gathers(tile, sl):
        for rr in range(tm):
            for k in range(TOP_K):
                src = dest_ref[k * tm * nsteps + tile * tm + rr]
                pltpu.make_async_copy(ys_hbm.at[pl.ds(src, 1)], gbuf.at[sl, pl.ds(k * tm + rr, 1)],
                                      sem.at[sl]).start(priority=k % 2)

    def wait_gathers(sl):
        for k in range(TOP_K):
            pltpu.make_async_copy(ys_hbm.at[pl.ds(0, tm)], gbuf.at[sl, pl.ds(k * tm, tm)],
                                  sem.at[sl]).wait()

    @pl.when(i < nsteps)
    def _():
        start_gathers(i, i % 2)

    @pl.when(i >= 1)
    def _():
        sl = (i - 1) % 2
        wait_gathers(sl)
        r = route_ref[...]
        lane = lax.broadcasted_iota(jnp.int32, r.shape, 1)
        g0 = jnp.sum(jnp.where(lane == 2, r, 0.0), axis=-1, keepdims=True)
        g1 = jnp.sum(jnp.where(lane == 3, r, 0.0), axis=-1, keepdims=True)
        a_lo, a_hi = _unpack_halves(gbuf[sl, 0:tm])
        b_lo, b_hi = _unpack_halves(gbuf[sl, tm:2 * tm])
        c = a_lo.shape[1]
        h_lo = h_ref[:, 0:c] + (a_lo * g0 + b_lo * g1)
        h_hi = h_ref[:, c:] + (a_hi * g0 + b_hi * g1)
        o_ref[:, 0:c] = h_lo
        o_ref[:, c:] = h_hi
        if emit_next:
            ms = (jnp.sum(h_lo * h_lo, axis=-1, keepdims=True)
                  + jnp.sum(h_hi * h_hi, axis=-1, keepdims=True)) / (2 * c)
            inv = lax.rsqrt(ms + EPS)
            xn_ref[:, 0:c] = (h_lo * inv * gn_ref[:, 0:c]).astype(BF16)
            xn_ref[:, c:] = (h_hi * inv * gn_ref[:, c:]).astype(BF16)


def _combine(dest, h1, route, ys, next_gain=None, tm=256):
    t, d = h1.shape
    c = ys.shape[1]
    nsteps = t // tm
    emit_next = next_gain is not None
    prev = lambda i, dst: (jnp.maximum(i - 1, 0), 0)
    in_specs = [
        pl.BlockSpec((tm, d), prev),
        pl.BlockSpec((tm, LANES), prev),
        pl.BlockSpec(memory_space=pl.ANY),
    ]
    out_specs = [pl.BlockSpec((tm, d), prev)]
    out_shape = [jax.ShapeDtypeStruct((t, d), F32)]
    args = [dest, h1, route, ys]
    if emit_next:
        in_specs.append(pl.BlockSpec((1, d), lambda i, dst: (0, 0)))
        out_specs.append(pl.BlockSpec((tm, d), prev))
        out_shape.append(jax.ShapeDtypeStruct((t, d), BF16))
        args.append(next_gain.reshape(1, d))
    return pl.pallas_call(
        functools.partial(_combine_kernel, tm=tm, nsteps=nsteps, emit_next=emit_next),
        grid_spec=pltpu.PrefetchScalarGridSpec(
            num_scalar_prefetch=1,
            grid=(nsteps + 1,),
            in_specs=in_specs,
            out_specs=out_specs,
            scratch_shapes=[pltpu.VMEM((2, TOP_K * tm, c), U32), pltpu.SemaphoreType.DMA((2,))],
        ),
        out_shape=out_shape,
        compiler_params=pltpu.CompilerParams(
            dimension_semantics=("arbitrary",), vmem_limit_bytes=VMEM_LIMIT),
        name="combine",
    )(*args)


def kernel(x, norm1, w_in, sgu_norm, sgu_w, sgu_b, conv_w, conv_b, conv_ln_g, conv_ln_b,
           q_norm, k_norm, out_norm, w_out, norm2, w_router_group, b_router_group,
           w_router_expert, b_router_expert, w_expert_gate, w_expert_up, w_expert_down):
    b, s, d = x.shape
    depth = norm1.shape[0]
    t = b * s
    h = x.reshape(t, d)
    a_end, b_end = SGU_WIDTH, SGU_WIDTH + CONV_CH
    x_in = h
    for l in range(depth):
        proj = _in_proj(x_in, norm1[l], w_in, l)
        y_ab = _mix_ab(proj, sgu_norm[l], sgu_w[l], sgu_b[l], conv_w[l], conv_b[l],
                       conv_ln_g[l], conv_ln_b[l], out_norm[l, :a_end], out_norm[l, a_end:b_end])
        y_c = _attention(proj, q_norm[l], k_norm[l])
        rw = jnp.concatenate([w_router_group[l], w_router_expert[l]], axis=1)
        rb = jnp.concatenate([b_router_group[l], b_router_expert[l]], axis=0)
        h1, xp, route, route_t, cnt = _out_proj(y_ab, y_c, out_norm[l, b_end:], h,
                                                w_out[l].astype(BF16), norm2[l], rw, rb)
        (dest, pad_start, pad_n, blk_e, first, seg_par, nxt_e, n_act,
         nblk) = _route_plan(route_t, cnt, t)
        last_blk = jnp.maximum(pad_start + pad_n - MOE_BLK, 0)
        ys = _experts(xp, dest, last_blk, blk_e, first, seg_par, nxt_e, n_act, nblk,
                      w_expert_gate, w_expert_up, w_expert_down, l)
        if l + 1 < depth:
            h, x_in = _combine(dest, h1, route, ys, next_gain=norm1[l + 1])
        else:
            (h,) = _combine(dest, h1, route, ys)
    return h.reshape(b, s, d)
```

```python
import functools

import jax
import jax.numpy as jnp
from jax import lax
from jax.experimental import pallas as pl
from jax.experimental.pallas import tpu as pltpu

F32 = jnp.float32
BF16 = jnp.bfloat16
U32 = jnp.uint32

D_MODEL = 2048
SGU_WIDTH = 512
SGU_HEADS = 4
SGU_CHUNK = 128
CONV_CH = 512
CONV_K = 31
ATTN_WIDTH = 1024
ATTN_HEADS = 8
HEAD_DIM = 128
DILATIONS = (1, 4, 16)
QBLK = 128
D_IN = 2 * SGU_WIDTH + 2 * CONV_CH + 3 * ATTN_WIDTH
N_GROUPS = 4
EXPERTS_PER_GROUP = 8
N_EXPERTS = 32
TOP_K = 2
D_EXPERT = 512
EPS = 1e-6

LANES = 128
VMEM_LIMIT = 56 * 1024 * 1024
NEG = -1e30

ATTN_TILE = QBLK * max(DILATIONS)
CONV_HALO = 32
MOE_BLK = 256
ROUTE_COLS = N_GROUPS + N_EXPERTS
ROUTE_LANES = 6


def _rms(x, g):
    return x * lax.rsqrt(jnp.mean(x * x, axis=-1, keepdims=True) + EPS) * g


def _in_proj_norm_kernel(x_ref, g_ref, w_ref, o_ref, xn_ref):
    @pl.when(pl.program_id(1) == 0)
    def _():
        xn_ref[...] = _rms(x_ref[...], g_ref[...]).astype(BF16)

    o_ref[...] = jnp.dot(xn_ref[...], w_ref[...].astype(BF16),
                         preferred_element_type=F32).astype(o_ref.dtype)


def _in_proj_kernel(xn_ref, w_ref, o_ref):
    o_ref[...] = jnp.dot(xn_ref[...], w_ref[...].astype(BF16),
                         preferred_element_type=F32).astype(o_ref.dtype)


def _in_proj(x, g, w_in, layer, tn=1024):
    s, d = x.shape
    n = w_in.shape[2]
    prenormed = x.dtype == BF16
    tm = 2048 if prenormed else 1024
    x_spec = pl.BlockSpec((tm, d), lambda i, j: (i, 0))
    w_spec = pl.BlockSpec((None, d, tn), lambda i, j: (layer, 0, j))
    common = dict(
        grid=(s // tm, n // tn),
        out_specs=pl.BlockSpec((tm, tn), lambda i, j: (i, j)),
        out_shape=jax.ShapeDtypeStruct((s, n), BF16),
        compiler_params=pltpu.CompilerParams(
            dimension_semantics=("parallel", "arbitrary"), vmem_limit_bytes=VMEM_LIMIT),
        name="in_proj",
    )
    if prenormed:
        return pl.pallas_call(_in_proj_kernel, in_specs=[x_spec, w_spec], **common)(x, w_in)
    return pl.pallas_call(
        _in_proj_norm_kernel,
        in_specs=[x_spec, pl.BlockSpec((1, d), lambda i, j: (0, 0)), w_spec],
        scratch_shapes=[pltpu.VMEM((tm, d), BF16)],
        **common,
    )(x, g.reshape(1, d), w_in)


def _mix_ab_kernel(u_ref, v_ref, a_ref, gt_ref, ah_ref, gh_ref,
                   sgn_ref, sw_ref, sb_ref, cw_ref, cb_ref, lng_ref, lnb_ref,
                   ona_ref, onb_ref, o_ref, zext_ref, conv_ref, stage_ref, *, tq):
    u = jax.nn.gelu(u_ref[...].astype(F32))
    v = _rms(jax.nn.gelu(v_ref[...].astype(F32)), sgn_ref[...]).astype(BF16)
    row = lax.broadcasted_iota(jnp.int32, (SGU_CHUNK, SGU_CHUNK), 0)
    col = lax.broadcasted_iota(jnp.int32, (SGU_CHUNK, SGU_CHUNK), 1)
    causal = col <= row
    wm = [jnp.where(causal, sw_ref[hh], 0.0).astype(BF16) for hh in range(SGU_HEADS)]
    for c in range(tq // SGU_CHUNK):
        rows = slice(c * SGU_CHUNK, (c + 1) * SGU_CHUNK)
        zs = []
        for hh in range(SGU_HEADS):
            cols = slice(hh * HEAD_DIM, (hh + 1) * HEAD_DIM)
            zs.append(jnp.dot(wm[hh], v[rows, cols], preferred_element_type=F32) + sb_ref[hh])
        ya = u[rows, :] * jnp.concatenate(zs, axis=1)
        o_ref[rows, 0:SGU_WIDTH] = _rms(ya, ona_ref[...]).astype(o_ref.dtype)

    first = pl.program_id(0) == 0
    zh = ah_ref[...].astype(F32) * jax.nn.sigmoid(gh_ref[...].astype(F32))
    zext_ref[0:CONV_HALO, :] = jnp.where(first, 0.0, zh)
    zext_ref[CONV_HALO:, :] = a_ref[...].astype(F32) * jax.nn.sigmoid(gt_ref[...].astype(F32))
    off0 = CONV_HALO - (CONV_K - 1)
    sub = 8
    rc = 128
    for j in range(CONV_CH // LANES):
        cols = slice(j * LANES, (j + 1) * LANES)
        for c in range(tq // rc):
            acc = jnp.zeros((rc, LANES), F32) + cb_ref[:, cols]
            for b in range(sub):
                taps = [(a, sub * a + b - off0) for a in range(CONV_HALO // sub + 1)
                        if 0 <= sub * a + b - off0 < CONV_K]
                rows = rc + sub * taps[-1][0]
                stage_ref[b, 0:rows, :] = zext_ref[pl.ds(c * rc + b, rows), cols]
                for a, k in taps:
                    acc = acc + stage_ref[b, sub * a:sub * a + rc, :] * cw_ref[k:k + 1, cols]
            conv_ref[c * rc:(c + 1) * rc, cols] = acc
    rc = 32
    for c in range(tq // rc):
        acc = conv_ref[c * rc:(c + 1) * rc, :]
        mu = jnp.mean(acc, axis=-1, keepdims=True)
        xc = acc - mu
        y = xc * lax.rsqrt(jnp.mean(xc * xc, axis=-1, keepdims=True) + EPS)
        y = jax.nn.silu(y * lng_ref[...] + lnb_ref[...])
        o_ref[c * rc:(c + 1) * rc, SGU_WIDTH:] = _rms(y, onb_ref[...]).astype(o_ref.dtype)


def _mix_ab(proj, sgu_norm, sgu_w, sgu_b, conv_w, conv_b, ln_g, ln_b, on_a, on_b, tq=512):
    s = proj.shape[0]
    w = SGU_WIDTH
    hb = tq // CONV_HALO
    sb = jnp.broadcast_to(sgu_b[:, :, None], (SGU_HEADS, SGU_CHUNK, HEAD_DIM))
    cw = jnp.pad(conv_w, ((0, 32 - CONV_K), (0, 0)))
    vec = lambda a: a.reshape(1, -1)
    const2 = lambda i: (0, 0)
    const3 = lambda i: (0, 0, 0)
    return pl.pallas_call(
        functools.partial(_mix_ab_kernel, tq=tq),
        grid=(s // tq,),
        in_specs=[
            pl.BlockSpec((tq, w), lambda i: (i, 0)),
            pl.BlockSpec((tq, w), lambda i: (i, 1)),
            pl.BlockSpec((tq, w), lambda i: (i, 2)),
            pl.BlockSpec((tq, w), lambda i: (i, 3)),
            pl.BlockSpec((CONV_HALO, w), lambda i: (jnp.maximum(i * hb - 1, 0), 2)),
            pl.BlockSpec((CONV_HALO, w), lambda i: (jnp.maximum(i * hb - 1, 0), 3)),
            pl.BlockSpec((1, w), const2),
            pl.BlockSpec((SGU_HEADS, SGU_CHUNK, SGU_CHUNK), const3),
            pl.BlockSpec((SGU_HEADS, SGU_CHUNK, HEAD_DIM), const3),
            pl.BlockSpec((32, w), const2),
            pl.BlockSpec((1, w), const2),
            pl.BlockSpec((1, w), const2),
            pl.BlockSpec((1, w), const2),
            pl.BlockSpec((1, w), const2),
            pl.BlockSpec((1, w), const2),
        ],
        out_specs=pl.BlockSpec((tq, 2 * w), lambda i: (i, 0)),
        out_shape=jax.ShapeDtypeStruct((s, 2 * w), BF16),
        scratch_shapes=[pltpu.VMEM((tq + CONV_HALO, w), F32), pltpu.VMEM((tq, w), F32),
                        pltpu.VMEM((8, SGU_CHUNK + CONV_HALO, LANES), F32)],
        compiler_params=pltpu.CompilerParams(
            dimension_semantics=("parallel",), vmem_limit_bytes=VMEM_LIMIT),
        name="mix_ab",
    )(proj, proj, proj, proj, proj, proj, vec(sgu_norm), sgu_w, sb, cw, vec(conv_b),
      vec(ln_g), vec(ln_b), vec(on_a), vec(on_b))


ATTN_PH = 4
ATTN_SL = ATTN_TILE // ATTN_PH


def _attn_kernel(q_ref, k_ref, v_ref, qg_ref, kg_ref, o_ref,
                 qn_ref, kn_ref, vv_ref, q4_ref, k4_ref, v4_ref, ob_ref, lb_ref, st_ref, yn_ref,
                 bias_ref):
    t = pl.program_id(1)
    tl, ph, sl = ATTN_TILE, ATTN_PH, ATTN_SL

    @pl.when(t == 0)
    def _():
        kn_ref[0:tl, :] = jnp.zeros((tl, HEAD_DIM), F32)
        vv_ref[0:tl, :] = jnp.zeros((tl, HEAD_DIM), F32)
        k4_ref[:, 0:sl, :] = jnp.zeros((ph, sl, HEAD_DIM), F32)
        v4_ref[:, 0:sl, :] = jnp.zeros((ph, sl, HEAD_DIM), F32)

    @pl.when(t > 0)
    def _():
        kn_ref[0:tl, :] = kn_ref[tl:, :]
        vv_ref[0:tl, :] = vv_ref[tl:, :]
        k4_ref[:, 0:sl, :] = k4_ref[:, sl:, :]
        v4_ref[:, 0:sl, :] = v4_ref[:, sl:, :]

    kn_ref[tl:, :] = _rms(k_ref[...].astype(F32), kg_ref[...])
    vv_ref[tl:, :] = v_ref[...].astype(F32)
    qn_ref[...] = _rms(q_ref[...].astype(F32), qg_ref[...]) * (HEAD_DIM ** -0.5)
    for r in range(ph):
        q4_ref[r] = qn_ref[pl.ds(r, sl, stride=ph), :]
        k4_ref[r, sl:, :] = kn_ref[pl.ds(tl + r, sl, stride=ph), :]
        v4_ref[r, sl:, :] = vv_ref[pl.ds(tl + r, sl, stride=ph), :]

    qi = lax.broadcasted_iota(jnp.int32, (QBLK, 2 * QBLK), 0)
    kj = lax.broadcasted_iota(jnp.int32, (QBLK, 2 * QBLK), 1)
    band = (kj >= qi) & (kj <= qi + QBLK)
    bias_ref[0] = jnp.where(band, 0.0, NEG)
    bias_ref[1] = jnp.where(band & (kj >= jnp.where(t > 0, 0, QBLK)), 0.0, NEG)

    def attend(qb, kb, vb, first):
        sc = lax.dot_general(qb.astype(BF16), kb.astype(BF16), (((1,), (1,)), ((), ())),
                             preferred_element_type=F32)
        sc = sc + bias_ref[1 if first else 0]
        m = jnp.max(sc, axis=-1, keepdims=True)
        p = jnp.exp(sc - m)
        l = jnp.sum(p, axis=-1, keepdims=True)
        ob = jnp.dot(p.astype(BF16), vb.astype(BF16), preferred_element_type=F32) / l
        return ob, jnp.broadcast_to(m + jnp.log(l), (QBLK, HEAD_DIM))

    for b in range(tl // QBLK):
        ob, lse = attend(qn_ref[b * QBLK:(b + 1) * QBLK, :],
                         kn_ref[tl + (b - 1) * QBLK:tl + (b + 1) * QBLK, :],
                         vv_ref[tl + (b - 1) * QBLK:tl + (b + 1) * QBLK, :], b == 0)
        s = b % 2
        st_ref[s, 0] = ob
        st_ref[s, 1] = lse
        n = QBLK // ph
        for r in range(ph):
            ob_ref[0, r, b * n:(b + 1) * n, :] = st_ref[s, 0, pl.ds(r, n, stride=ph), :]
            lb_ref[0, r, b * n:(b + 1) * n, :] = st_ref[s, 1, pl.ds(r, n, stride=ph), :]

    for r in range(ph):
        for b in range(sl // QBLK):
            ob, lse = attend(q4_ref[r, b * QBLK:(b + 1) * QBLK, :],
                             k4_ref[r, sl + (b - 1) * QBLK:sl + (b + 1) * QBLK, :],
                             v4_ref[r, sl + (b - 1) * QBLK:sl + (b + 1) * QBLK, :], b == 0)
            ob_ref[1, r, b * QBLK:(b + 1) * QBLK, :] = ob
            lb_ref[1, r, b * QBLK:(b + 1) * QBLK, :] = lse
        for a in range(ph):
            ob, lse = attend(q4_ref[r, pl.ds(a, QBLK, stride=ph), :],
                             k4_ref[r, pl.ds(a, 2 * QBLK, stride=ph), :],
                             v4_ref[r, pl.ds(a, 2 * QBLK, stride=ph), :], True)
            ob_ref[2, r, pl.ds(a, QBLK, stride=ph), :] = ob
            lb_ref[2, r, pl.ds(a, QBLK, stride=ph), :] = lse

    for r in range(ph):
        for c in range(sl // QBLK):
            rows = slice(c * QBLK, (c + 1) * QBLK)
            l0, l1, l2 = lb_ref[0, r, rows, :], lb_ref[1, r, rows, :], lb_ref[2, r, rows, :]
            mx = jnp.maximum(jnp.maximum(l0, l1), l2)
            w0, w1, w2 = jnp.exp(l0 - mx), jnp.exp(l1 - mx), jnp.exp(l2 - mx)
            y = (w0 * ob_ref[0, r, rows, :] + w1 * ob_ref[1, r, rows, :]
                 + w2 * ob_ref[2, r, rows, :]) / (w0 + w1 + w2)
            yn_ref[pl.ds(c * QBLK * ph + r, QBLK, stride=ph), :] = y
    o_ref[...] = yn_ref[...].astype(o_ref.dtype)


def _attention(proj, q_g, k_g):
    s = proj.shape[0]
    tl = ATTN_TILE
    qc = (2 * SGU_WIDTH + 2 * CONV_CH) // HEAD_DIM
    kc = qc + ATTN_HEADS
    vc = kc + ATTN_HEADS
    return pl.pallas_call(
        _attn_kernel,
        grid=(ATTN_HEADS, s // tl),
        in_specs=[
            pl.BlockSpec((tl, HEAD_DIM), lambda h, t: (t, qc + h)),
            pl.BlockSpec((tl, HEAD_DIM), lambda h, t: (t, kc + h)),
            pl.BlockSpec((tl, HEAD_DIM), lambda h, t: (t, vc + h)),
            pl.BlockSpec((1, HEAD_DIM), lambda h, t: (0, 0)),
            pl.BlockSpec((1, HEAD_DIM), lambda h, t: (0, 0)),
        ],
        out_specs=pl.BlockSpec((tl, HEAD_DIM), lambda h, t: (t, h)),
        out_shape=jax.ShapeDtypeStruct((s, ATTN_WIDTH), BF16),
        scratch_shapes=[
            pltpu.VMEM((tl, HEAD_DIM), F32),
            pltpu.VMEM((2 * tl, HEAD_DIM), F32),
            pltpu.VMEM((2 * tl, HEAD_DIM), F32),
            pltpu.VMEM((ATTN_PH, ATTN_SL, HEAD_DIM), F32),
            pltpu.VMEM((ATTN_PH, 2 * ATTN_SL, HEAD_DIM), F32),
            pltpu.VMEM((ATTN_PH, 2 * ATTN_SL, HEAD_DIM), F32),
            pltpu.VMEM((len(DILATIONS), ATTN_PH, ATTN_SL, HEAD_DIM), F32),
            pltpu.VMEM((len(DILATIONS), ATTN_PH, ATTN_SL, HEAD_DIM), F32),
            pltpu.VMEM((2, 2, QBLK, HEAD_DIM), F32),
            pltpu.VMEM((tl, HEAD_DIM), F32),
            pltpu.VMEM((2, QBLK, 2 * QBLK), F32),
        ],
        compiler_params=pltpu.CompilerParams(
            dimension_semantics=("parallel", "arbitrary"), vmem_limit_bytes=VMEM_LIMIT),
        name="attention",
    )(proj, proj, proj, q_g.reshape(1, HEAD_DIM), k_g.reshape(1, HEAD_DIM))


def _split_bf16(x):
    hi = x.astype(BF16)
    lo = (x - hi.astype(F32)).astype(BF16)
    return hi, lo


def _pack_halves(x):
    c = x.shape[1] // 2
    lo = lax.bitcast_convert_type(x[:, :c].astype(BF16).astype(F32), U32)
    hi = lax.bitcast_convert_type(x[:, c:].astype(BF16).astype(F32), U32)
    return (lo >> 16) | hi


def _unpack_halves(u):
    lo = lax.bitcast_convert_type(u << 16, F32)
    hi = lax.bitcast_convert_type(u & jnp.uint32(0xFFFF0000), F32)
    return lo, hi


def _out_proj_kernel(yab_ref, yc_ref, onc_ref, h_ref, w_ref, n2_ref, rwh_ref, rwl_ref, rb_ref,
                     h1_ref, xp_ref, route_ref, routet_ref, cnt_ref, tri_ref, run_ref, h1s_ref):
    tm = h_ref.shape[0]

    @pl.when(pl.program_id(0) == 0)
    def _():
        r = lax.broadcasted_iota(jnp.int32, (tm, tm), 0)
        c = lax.broadcasted_iota(jnp.int32, (tm, tm), 1)
        tri_ref[...] = jnp.where(c < r, 1.0, 0.0).astype(BF16)
        run_ref[...] = jnp.zeros(run_ref.shape, F32)
        h1s_ref[...] = jnp.zeros(h1s_ref.shape, F32)

    half = yab_ref.shape[1]
    xn = _rms(h1s_ref[...], n2_ref[...])
    xp_ref[...] = _pack_halves(xn)
    xh, xl = _split_bf16(xn)
    lg = (jnp.dot(xh, rwh_ref[...], preferred_element_type=F32)
          + jnp.dot(xl, rwh_ref[...], preferred_element_type=F32)
          + jnp.dot(xh, rwl_ref[...], preferred_element_type=F32)) + rb_ref[...]

    ycn = _rms(yc_ref[...].astype(F32), onc_ref[...]).astype(BF16)
    acc = jnp.dot(yab_ref[...], w_ref[0:half, :], preferred_element_type=F32)
    acc = acc + jnp.dot(ycn, w_ref[half:, :], preferred_element_type=F32)
    h1 = h_ref[...] + acc
    h1_ref[...] = h1

    run = _route_from_logits(lg, run_ref[...], pl.program_id(0) >= 1, route_ref, routet_ref, tri_ref)
    run_ref[...] = run
    cnt_ref[...] = run
    h1s_ref[...] = h1


def _route_from_logits(lg, run, counted, route_ref, routet_ref, tri_ref):
    lane = lax.broadcasted_iota(jnp.int32, lg.shape, 1)
    big = jnp.int32(LANES)
    gl = jnp.where(lane < N_GROUPS, lg, NEG)
    gmax = jnp.max(gl, axis=-1, keepdims=True)
    grp = jnp.min(jnp.where(gl == gmax, lane, big), axis=-1, keepdims=True)
    gate_g = 1.0 / jnp.sum(jnp.exp(gl - gmax), axis=-1, keepdims=True)
    lo = N_GROUPS + grp * EXPERTS_PER_GROUP
    el = jnp.where((lane >= lo) & (lane < lo + EXPERTS_PER_GROUP), lg, NEG)
    v1 = jnp.max(el, axis=-1, keepdims=True)
    i1 = jnp.min(jnp.where(el == v1, lane, big), axis=-1, keepdims=True)
    el2 = jnp.where(lane == i1, NEG, el)
    v2 = jnp.max(el2, axis=-1, keepdims=True)
    i2 = jnp.min(jnp.where(el2 == v2, lane, big), axis=-1, keepdims=True)
    e2 = jnp.exp(v2 - v1)
    g0 = gate_g / (1.0 + e2)
    g1 = gate_g * e2 / (1.0 + e2)
    oh0 = lane == i1 - N_GROUPS
    oh1 = lane == i2 - N_GROUPS
    cnt = jnp.where((oh0 | oh1) & counted, 1.0, 0.0)
    before = jnp.dot(tri_ref[...], cnt.astype(BF16), preferred_element_type=F32) + run
    r0 = jnp.sum(jnp.where(oh0, before, 0.0), axis=-1, keepdims=True)
    r1 = jnp.sum(jnp.where(oh1, before, 0.0), axis=-1, keepdims=True)

    route = (i1 - N_GROUPS).astype(F32)
    for k, val in enumerate(((i2 - N_GROUPS).astype(F32), g0, g1, r0, r1), start=1):
        route = jnp.where(lane == k, val, route)
    route = jnp.where(lane < ROUTE_LANES, route, 0.0)
    route_ref[...] = route
    routet_ref[...] = route.T[0:8, :]
    return run + jnp.sum(cnt, axis=0, keepdims=True)


def _out_proj(y_ab, y_c, on_c, h, w_bf16, norm2, rw, rb, tm=512):
    s, d = h.shape
    half = y_ab.shape[1]
    n = s // tm
    rw_pad = jnp.pad(rw, ((0, 0), (0, LANES - ROUTE_COLS)))
    rwh, rwl = _split_bf16(rw_pad)
    rb_pad = jnp.pad(rb, (0, LANES - ROUTE_COLS)).reshape(1, LANES)
    const = lambda i: (0, 0)
    proj = lambda i: (jnp.minimum(i, n - 1), 0)
    routed = lambda i: (jnp.maximum(i - 1, 0), 0)
    return pl.pallas_call(
        _out_proj_kernel,
        grid=(n + 1,),
        in_specs=[
            pl.BlockSpec((tm, half), proj),
            pl.BlockSpec((tm, half), proj),
            pl.BlockSpec((1, half), const),
            pl.BlockSpec((tm, d), proj),
            pl.BlockSpec((d, d), const, pipeline_mode=pl.Buffered(1)),
            pl.BlockSpec((1, d), const),
            pl.BlockSpec((d, LANES), const),
            pl.BlockSpec((d, LANES), const),
            pl.BlockSpec((1, LANES), const),
        ],
        out_specs=[
            pl.BlockSpec((tm, d), proj),
            pl.BlockSpec((tm, d // 2), routed),
            pl.BlockSpec((tm, LANES), routed),
            pl.BlockSpec((8, tm), lambda i: (0, jnp.maximum(i - 1, 0))),
            pl.BlockSpec((1, LANES), const),
        ],
        out_shape=[
            jax.ShapeDtypeStruct((s, d), F32),
            jax.ShapeDtypeStruct((s, d // 2), U32),
            jax.ShapeDtypeStruct((s, LANES), F32),
            jax.ShapeDtypeStruct((8, s), F32),
            jax.ShapeDtypeStruct((1, LANES), F32),
        ],
        scratch_shapes=[pltpu.VMEM((tm, tm), BF16), pltpu.VMEM((1, LANES), F32),
                        pltpu.VMEM((tm, d), F32)],
        compiler_params=pltpu.CompilerParams(
            dimension_semantics=("arbitrary",), vmem_limit_bytes=VMEM_LIMIT),
        name="out_proj",
    )(y_ab, y_c, on_c.reshape(1, half), h, w_bf16, norm2.reshape(1, d), rwh, rwl, rb_pad)


def _route_plan(route_t, cnt, t):
    nblk = t * TOP_K // MOE_BLK + N_EXPERTS
    ids = jnp.arange(N_EXPERTS, dtype=jnp.int32)
    e = route_t[0:TOP_K].astype(jnp.int32)
    rank = route_t[4:4 + TOP_K].astype(jnp.int32)
    counts = cnt[0, :N_EXPERTS].astype(jnp.int32)
    pcounts = (counts + MOE_BLK - 1) // MOE_BLK * MOE_BLK
    pends = jnp.cumsum(pcounts)
    pstarts = pends - pcounts
    dest = (jnp.sum(jnp.where(e[:, :, None] == ids, pstarts, 0), axis=-1) + rank).reshape(-1)
    blk = jnp.arange(nblk, dtype=jnp.int32)
    blk_e = jnp.minimum(jnp.sum((blk[:, None] * MOE_BLK >= pends[None, :]).astype(jnp.int32), axis=1),
                        N_EXPERTS - 1)
    n_act = pends[-1] // MOE_BLK
    mine = blk_e[:, None] == ids[None, :]
    first = (blk == 0) | (blk_e != jnp.roll(blk_e, 1))
    seg_par = (jnp.cumsum(first.astype(jnp.int32)) - 1) % 2
    nxt_blk = jnp.sum(jnp.where(mine, pends, 0), axis=1) // MOE_BLK
    nxt_e = jnp.where(nxt_blk < n_act,
                      jnp.sum(jnp.where(nxt_blk[:, None] == blk[None, :], blk_e[None, :], 0), axis=1),
                      -1)
    i32 = lambda a: a.astype(jnp.int32)
    return (i32(dest), i32(pstarts + counts), i32(pcounts - counts), i32(blk_e), i32(first),
            i32(seg_par), i32(nxt_e), i32(n_act).reshape(1), nblk)


def _dispatch_kernel(dest_ref, pad_start_ref, pad_n_ref, n_act_ref, xp_ref, xs_ref,
                     buf, zbuf, sem, zsem, *, tm, nsteps, nblk):
    i = pl.program_id(0)
    slot = i % 2

    def wait_slot(sl):
        for _ in range(TOP_K):
            pltpu.make_async_copy(buf.at[sl], xs_ref.at[pl.ds(0, tm)], sem.at[sl]).wait()

    def for_padding_copies(fn):
        sub = 8
        for e in range(N_EXPERTS):
            n = pad_n_ref[e]
            start = pad_start_ref[e]
            head = jnp.minimum((sub - (start & (sub - 1))) & (sub - 1), n)
            for j in range(sub - 1):
                @pl.when(j < head)
                def _(row=start + j):
                    fn(pltpu.make_async_copy(zbuf.at[pl.ds(0, 1)], xs_ref.at[pl.ds(row, 1)], zsem))
            n = n - head
            start = start + head
            p = MOE_BLK // 2
            while p >= sub:
                @pl.when((n & p) != 0)
                def _(start=start, p=p):
                    fn(pltpu.make_async_copy(zbuf.at[pl.ds(0, p)],
                                             xs_ref.at[pl.ds(pl.multiple_of(start, sub), p)], zsem))
                start = start + (n & p)
                p //= 2
        for j in range(nblk - nsteps * tm * TOP_K // MOE_BLK):
            blk = n_act_ref[0] + j

            @pl.when(blk < nblk)
            def _(blk=blk):
                fn(pltpu.make_async_copy(zbuf, xs_ref.at[pl.ds(blk * MOE_BLK, MOE_BLK)], zsem))

    @pl.when(i == 0)
    def _():
        zbuf[...] = jnp.zeros(zbuf.shape, zbuf.dtype)
        for_padding_copies(lambda cp: cp.start())

    @pl.when(i >= 2)
    def _():
        wait_slot(slot)

    buf[slot] = xp_ref[...]
    for rr in range(tm):
        for k in range(TOP_K):
            dst = dest_ref[k * tm * nsteps + i * tm + rr]
            pltpu.make_async_copy(buf.at[slot, pl.ds(rr, 1)], xs_ref.at[pl.ds(dst, 1)],
                                  sem.at[slot]).start()

    @pl.when(i == nsteps - 1)
    def _():
        wait_slot(slot)
        if nsteps > 1:
            wait_slot(1 - slot)
        for_padding_copies(lambda cp: cp.wait())


def _dispatch(dest, pad_start, pad_n, n_act, xp, nblk, tm=256):
    t, c = xp.shape
    nsteps = t // tm
    return pl.pallas_call(
        functools.partial(_dispatch_kernel, tm=tm, nsteps=nsteps, nblk=nblk),
        grid_spec=pltpu.PrefetchScalarGridSpec(
            num_scalar_prefetch=4,
            grid=(nsteps,),
            in_specs=[pl.BlockSpec((tm, c), lambda i, *_: (i, 0))],
            out_specs=pl.BlockSpec(memory_space=pl.ANY),
            scratch_shapes=[pltpu.VMEM((2, tm, c), U32), pltpu.VMEM((MOE_BLK, c), U32),
                            pltpu.SemaphoreType.DMA((2,)), pltpu.SemaphoreType.DMA],
        ),
        out_shape=jax.ShapeDtypeStruct((nblk * MOE_BLK, c), U32),
        compiler_params=pltpu.CompilerParams(
            dimension_semantics=("arbitrary",), vmem_limit_bytes=VMEM_LIMIT),
        name="dispatch",
    )(dest, pad_start, pad_n, n_act, xp)


def _experts_kernel(blk_e_ref, first_ref, par_ref, nxt_ref, n_act_ref,
                    xs_ref, wg_hbm, wu_hbm, wd_hbm, ys_ref,
                    wg_f, wu_f, wd_f, wg_b, wu_b, wd_b, wsem, *, layer):
    i = pl.program_id(0)
    n_act = n_act_ref[0]
    c = xs_ref.shape[1]

    def weight_copies(e, s):
        return (pltpu.make_async_copy(wg_hbm.at[layer, e], wg_f.at[s], wsem.at[s]),
                pltpu.make_async_copy(wu_hbm.at[layer, e], wu_f.at[s], wsem.at[s]),
                pltpu.make_async_copy(wd_hbm.at[layer, e], wd_f.at[s], wsem.at[s]))

    @pl.when(i >= n_act)
    def _():
        ys_ref[...] = jnp.zeros(ys_ref.shape, ys_ref.dtype)

    @pl.when(i < n_act)
    def _():
        s = par_ref[i]

        @pl.when(first_ref[i] == 1)
        def _():
            @pl.when(i == 0)
            def _():
                for cp in weight_copies(blk_e_ref[0], 0):
                    cp.start()

            for cp in weight_copies(0, s):
                cp.wait()
            wg_b[...] = wg_f[s].astype(BF16)
            wu_b[...] = wu_f[s].astype(BF16)
            wd_b[...] = wd_f[s].astype(BF16)

            @pl.when(nxt_ref[i] >= 0)
            def _():
                for cp in weight_copies(nxt_ref[i], 1 - s):
                    cp.start()

        x_lo, x_hi = _unpack_halves(xs_ref[...])
        x_lo = x_lo.astype(BF16)
        x_hi = x_hi.astype(BF16)
        hg = (jnp.dot(x_lo, wg_b[0:c, :], preferred_element_type=F32)
              + jnp.dot(x_hi, wg_b[c:, :], preferred_element_type=F32))
        hu = (jnp.dot(x_lo, wu_b[0:c, :], preferred_element_type=F32)
              + jnp.dot(x_hi, wu_b[c:, :], preferred_element_type=F32))
        hb = (jax.nn.silu(hg) * hu).astype(BF16)
        ys_ref[...] = _pack_halves(jnp.dot(hb, wd_b[...], preferred_element_type=F32))


def _experts(xs, blk_e, first, seg_par, nxt_e, n_act, nblk, wg, wu, wd, layer):
    n_slots, c = xs.shape
    _, _, d, de = wg.shape
    active_rows = lambda i, be, fi, pa, nx, na: (jnp.minimum(i, na[0] - 1), 0)
    return pl.pallas_call(
        functools.partial(_experts_kernel, layer=layer),
        grid_spec=pltpu.PrefetchScalarGridSpec(
            num_scalar_prefetch=5,
            grid=(nblk,),
            in_specs=[
                pl.BlockSpec((MOE_BLK, c), active_rows),
                pl.BlockSpec(memory_space=pl.ANY),
                pl.BlockSpec(memory_space=pl.ANY),
                pl.BlockSpec(memory_space=pl.ANY),
            ],
            out_specs=pl.BlockSpec((MOE_BLK, c), lambda i, *_: (i, 0)),
            scratch_shapes=[
                pltpu.VMEM((2, d, de), F32),
                pltpu.VMEM((2, d, de), F32),
                pltpu.VMEM((2, de, d), F32),
                pltpu.VMEM((d, de), BF16),
                pltpu.VMEM((d, de), BF16),
                pltpu.VMEM((de, d), BF16),
                pltpu.SemaphoreType.DMA((2,)),
            ],
        ),
        out_shape=jax.ShapeDtypeStruct((n_slots, c), U32),
        compiler_params=pltpu.CompilerParams(
            dimension_semantics=("arbitrary",), vmem_limit_bytes=VMEM_LIMIT),
        name="experts",
    )(blk_e, first, seg_par, nxt_e, n_act, xs, wg, wu, wd)


def _combine_kernel(dest_ref, h_ref, route_ref, ys_hbm, *rest, tm, nsteps, emit_next):
    if emit_next:
        gn_ref, o_ref, xn_ref, gbuf, sem = rest
    else:
        o_ref, gbuf, sem = rest
    i = pl.program_id(0)

    def start_gathers(tile, sl):
        for rr in range(tm):
            for k in range(TOP_K):
                src = dest_ref[k * tm * nsteps + tile * tm + rr]
                pltpu.make_async_copy(ys_hbm.at[pl.ds(src, 1)], gbuf.at[sl, pl.ds(k * tm + rr, 1)],
                                      sem.at[sl]).start()

    def wait_gathers(sl):
        for k in range(TOP_K):
            pltpu.make_async_copy(ys_hbm.at[pl.ds(0, tm)], gbuf.at[sl, pl.ds(k * tm, tm)],
                                  sem.at[sl]).wait()

    @pl.when(i < nsteps)
    def _():
        start_gathers(i, i % 2)

    @pl.when(i >= 1)
    def _():
        sl = (i - 1) % 2
        wait_gathers(sl)
        r = route_ref[...]
        lane = lax.broadcasted_iota(jnp.int32, r.shape, 1)
        g0 = jnp.sum(jnp.where(lane == 2, r, 0.0), axis=-1, keepdims=True)
        g1 = jnp.sum(jnp.where(lane == 3, r, 0.0), axis=-1, keepdims=True)
        a_lo, a_hi = _unpack_halves(gbuf[sl, 0:tm])
        b_lo, b_hi = _unpack_halves(gbuf[sl, tm:2 * tm])
        c = a_lo.shape[1]
        h_lo = h_ref[:, 0:c] + (a_lo * g0 + b_lo * g1)
        h_hi = h_ref[:, c:] + (a_hi * g0 + b_hi * g1)
        o_ref[:, 0:c] = h_lo
        o_ref[:, c:] = h_hi
        if emit_next:
            ms = (jnp.sum(h_lo * h_lo, axis=-1, keepdims=True)
                  + jnp.sum(h_hi * h_hi, axis=-1, keepdims=True)) / (2 * c)
            inv = lax.rsqrt(ms + EPS)
            xn_ref[:, 0:c] = (h_lo * inv * gn_ref[:, 0:c]).astype(BF16)
            xn_ref[:, c:] = (h_hi * inv * gn_ref[:, c:]).astype(BF16)


def _combine(dest, h1, route, ys, next_gain=None, tm=256):
    t, d = h1.shape
    c = ys.shape[1]
    nsteps = t // tm
    emit_next = next_gain is not None
    prev = lambda i, dst: (jnp.maximum(i - 1, 0), 0)
    in_specs = [
        pl.BlockSpec((tm, d), prev),
        pl.BlockSpec((tm, LANES), prev),
        pl.BlockSpec(memory_space=pl.ANY),
    ]
    out_specs = [pl.BlockSpec((tm, d), prev)]
    out_shape = [jax.ShapeDtypeStruct((t, d), F32)]
    args = [dest, h1, route, ys]
    if emit_next:
        in_specs.append(pl.BlockSpec((1, d), lambda i, dst: (0, 0)))
        out_specs.append(pl.BlockSpec((tm, d), prev))
        out_shape.append(jax.ShapeDtypeStruct((t, d), BF16))
        args.append(next_gain.reshape(1, d))
    return pl.pallas_call(
        functools.partial(_combine_kernel, tm=tm, nsteps=nsteps, emit_next=emit_next),
        grid_spec=pltpu.PrefetchScalarGridSpec(
            num_scalar_prefetch=1,
            grid=(nsteps + 1,),
            in_specs=in_specs,
            out_specs=out_specs,
            scratch_shapes=[pltpu.VMEM((2, TOP_K * tm, c), U32), pltpu.SemaphoreType.DMA((2,))],
        ),
        out_shape=out_shape,
        compiler_params=pltpu.CompilerParams(
            dimension_semantics=("arbitrary",), vmem_limit_bytes=VMEM_LIMIT),
        name="combine",
    )(*args)


def kernel(x, norm1, w_in, sgu_norm, sgu_w, sgu_b, conv_w, conv_b, conv_ln_g, conv_ln_b,
           q_norm, k_norm, out_norm, w_out, norm2, w_router_group, b_router_group,
           w_router_expert, b_router_expert, w_expert_gate, w_expert_up, w_expert_down):
    b, s, d = x.shape
    depth = norm1.shape[0]
    t = b * s
    h = x.reshape(t, d)
    a_end, b_end = SGU_WIDTH, SGU_WIDTH + CONV_CH
    x_in = h
    for l in range(depth):
        proj = _in_proj(x_in, norm1[l], w_in, l)
        y_ab = _mix_ab(proj, sgu_norm[l], sgu_w[l], sgu_b[l], conv_w[l], conv_b[l],
                       conv_ln_g[l], conv_ln_b[l], out_norm[l, :a_end], out_norm[l, a_end:b_end])
        y_c = _attention(proj, q_norm[l], k_norm[l])
        rw = jnp.concatenate([w_router_group[l], w_router_expert[l]], axis=1)
        rb = jnp.concatenate([b_router_group[l], b_router_expert[l]], axis=0)
        h1, xp, route, route_t, cnt = _out_proj(y_ab, y_c, out_norm[l, b_end:], h,
                                                w_out[l].astype(BF16), norm2[l], rw, rb)
        (dest, pad_start, pad_n, blk_e, first, seg_par, nxt_e, n_act,
         nblk) = _route_plan(route_t, cnt, t)
        xs = _dispatch(dest, pad_start, pad_n, n_act, xp, nblk)
        ys = _experts(xs, blk_e, first, seg_par, nxt_e, n_act, nblk,
                      w_expert_gate, w_expert_up, w_expert_down, l)
        if l + 1 < depth:
            h, x_in = _combine(dest, h1, route, ys, next_gain=norm1[l + 1])
        else:
            (h,) = _combine(dest, h1, route, ys)
    return h.reshape(b, s, d)
```

```python
import functools

import jax
import jax.numpy as jnp
from jax import lax
from jax.experimental import pallas as pl
from jax.experimental.pallas import tpu as pltpu

F32 = jnp.float32
BF16 = jnp.bfloat16
U32 = jnp.uint32

D_MODEL = 2048
SGU_WIDTH = 512
SGU_HEADS = 4
SGU_CHUNK = 128
CONV_CH = 512
CONV_K = 31
ATTN_WIDTH = 1024
ATTN_HEADS = 8
HEAD_DIM = 128
DILATIONS = (1, 4, 16)
QBLK = 128
D_IN = 2 * SGU_WIDTH + 2 * CONV_CH + 3 * ATTN_WIDTH
N_GROUPS = 4
EXPERTS_PER_GROUP = 8
N_EXPERTS = 32
TOP_K = 2
D_EXPERT = 512
EPS = 1e-6

LANES = 128
VMEM_LIMIT = 56 * 1024 * 1024
NEG = -1e30

ATTN_TILE = QBLK * max(DILATIONS)
CONV_HALO = 32
MOE_BLK = 256
ROUTE_COLS = N_GROUPS + N_EXPERTS
ROUTE_LANES = 6


def _rms(x, g):
    return x * lax.rsqrt(jnp.mean(x * x, axis=-1, keepdims=True) + EPS) * g


def _in_proj_norm_kernel(x_ref, g_ref, w_ref, o_ref, xn_ref):
    @pl.when(pl.program_id(1) == 0)
    def _():
        xn_ref[...] = _rms(x_ref[...], g_ref[...]).astype(BF16)

    o_ref[...] = jnp.dot(xn_ref[...], w_ref[...].astype(BF16),
                         preferred_element_type=F32).astype(o_ref.dtype)


def _in_proj_kernel(xn_ref, w_ref, o_ref):
    o_ref[...] = jnp.dot(xn_ref[...], w_ref[...].astype(BF16),
                         preferred_element_type=F32).astype(o_ref.dtype)


def _in_proj(x, g, w_in, layer, tn=1024):
    s, d = x.shape
    n = w_in.shape[2]
    prenormed = x.dtype == BF16
    tm = 2048 if prenormed else 1024
    x_spec = pl.BlockSpec((tm, d), lambda i, j: (i, 0))
    w_spec = pl.BlockSpec((None, d, tn), lambda i, j: (layer, 0, j))
    common = dict(
        grid=(s // tm, n // tn),
        out_specs=pl.BlockSpec((tm, tn), lambda i, j: (i, j)),
        out_shape=jax.ShapeDtypeStruct((s, n), BF16),
        compiler_params=pltpu.CompilerParams(
            dimension_semantics=("parallel", "arbitrary"), vmem_limit_bytes=VMEM_LIMIT),
        name="in_proj",
    )
    if prenormed:
        return pl.pallas_call(_in_proj_kernel, in_specs=[x_spec, w_spec], **common)(x, w_in)
    return pl.pallas_call(
        _in_proj_norm_kernel,
        in_specs=[x_spec, pl.BlockSpec((1, d), lambda i, j: (0, 0)), w_spec],
        scratch_shapes=[pltpu.VMEM((tm, d), BF16)],
        **common,
    )(x, g.reshape(1, d), w_in)


def _mix_ab_kernel(u_ref, v_ref, a_ref, gt_ref, ah_ref, gh_ref,
                   sgn_ref, sw_ref, sb_ref, cw_ref, cb_ref, lng_ref, lnb_ref,
                   ona_ref, onb_ref, o_ref, zext_ref, conv_ref, stage_ref, *, tq):
    u = jax.nn.gelu(u_ref[...].astype(F32))
    v = _rms(jax.nn.gelu(v_ref[...].astype(F32)), sgn_ref[...]).astype(BF16)
    row = lax.broadcasted_iota(jnp.int32, (SGU_CHUNK, SGU_CHUNK), 0)
    col = lax.broadcasted_iota(jnp.int32, (SGU_CHUNK, SGU_CHUNK), 1)
    causal = col <= row
    wm = [jnp.where(causal, sw_ref[hh], 0.0).astype(BF16) for hh in range(SGU_HEADS)]
    nch = tq // SGU_CHUNK
    zh = []
    for hh in range(SGU_HEADS):
        cols = slice(hh * HEAD_DIM, (hh + 1) * HEAD_DIM)
        vh = jnp.concatenate([v[c * SGU_CHUNK:(c + 1) * SGU_CHUNK, cols] for c in range(nch)],
                             axis=1)
        zh.append(jnp.dot(wm[hh], vh, preferred_element_type=F32))
    for c in range(nch):
        rows = slice(c * SGU_CHUNK, (c + 1) * SGU_CHUNK)
        zs = [zh[hh][:, c * HEAD_DIM:(c + 1) * HEAD_DIM] + sb_ref[hh] for hh in range(SGU_HEADS)]
        ya = u[rows, :] * jnp.concatenate(zs, axis=1)
        o_ref[rows, 0:SGU_WIDTH] = _rms(ya, ona_ref[...]).astype(o_ref.dtype)

    first = pl.program_id(0) == 0
    zh = ah_ref[...].astype(F32) * jax.nn.sigmoid(gh_ref[...].astype(F32))
    zext_ref[0:CONV_HALO, :] = jnp.where(first, 0.0, zh)
    zext_ref[CONV_HALO:, :] = a_ref[...].astype(F32) * jax.nn.sigmoid(gt_ref[...].astype(F32))
    off0 = CONV_HALO - (CONV_K - 1)
    sub = 8
    rc = 128
    for j in range(CONV_CH // LANES):
        cols = slice(j * LANES, (j + 1) * LANES)
        for c in range(tq // rc):
            acc = jnp.zeros((rc, LANES), F32) + cb_ref[:, cols]
            for b in range(sub):
                taps = [(a, sub * a + b - off0) for a in range(CONV_HALO // sub + 1)
                        if 0 <= sub * a + b - off0 < CONV_K]
                rows = rc + sub * taps[-1][0]
                stage_ref[b, 0:rows, :] = zext_ref[pl.ds(c * rc + b, rows), cols]
                for a, k in taps:
                    acc = acc + stage_ref[b, sub * a:sub * a + rc, :] * cw_ref[k:k + 1, cols]
            conv_ref[c * rc:(c + 1) * rc, cols] = acc
    rc = 32
    for c in range(tq // rc):
        acc = conv_ref[c * rc:(c + 1) * rc, :]
        mu = jnp.mean(acc, axis=-1, keepdims=True)
        xc = acc - mu
        y = xc * lax.rsqrt(jnp.mean(xc * xc, axis=-1, keepdims=True) + EPS)
        y = jax.nn.silu(y * lng_ref[...] + lnb_ref[...])
        o_ref[c * rc:(c + 1) * rc, SGU_WIDTH:] = _rms(y, onb_ref[...]).astype(o_ref.dtype)


def _mix_ab(proj, sgu_norm, sgu_w, sgu_b, conv_w, conv_b, ln_g, ln_b, on_a, on_b, tq=512):
    s = proj.shape[0]
    w = SGU_WIDTH
    hb = tq // CONV_HALO
    sb = jnp.broadcast_to(sgu_b[:, :, None], (SGU_HEADS, SGU_CHUNK, HEAD_DIM))
    cw = jnp.pad(conv_w, ((0, 32 - CONV_K), (0, 0)))
    vec = lambda a: a.reshape(1, -1)
    const2 = lambda i: (0, 0)
    const3 = lambda i: (0, 0, 0)
    return pl.pallas_call(
        functools.partial(_mix_ab_kernel, tq=tq),
        grid=(s // tq,),
        in_specs=[
            pl.BlockSpec((tq, w), lambda i: (i, 0)),
            pl.BlockSpec((tq, w), lambda i: (i, 1)),
            pl.BlockSpec((tq, w), lambda i: (i, 2)),
            pl.BlockSpec((tq, w), lambda i: (i, 3)),
            pl.BlockSpec((CONV_HALO, w), lambda i: (jnp.maximum(i * hb - 1, 0), 2)),
            pl.BlockSpec((CONV_HALO, w), lambda i: (jnp.maximum(i * hb - 1, 0), 3)),
            pl.BlockSpec((1, w), const2),
            pl.BlockSpec((SGU_HEADS, SGU_CHUNK, SGU_CHUNK), const3),
            pl.BlockSpec((SGU_HEADS, SGU_CHUNK, HEAD_DIM), const3),
            pl.BlockSpec((32, w), const2),
            pl.BlockSpec((1, w), const2),
            pl.BlockSpec((1, w), const2),
            pl.BlockSpec((1, w), const2),
            pl.BlockSpec((1, w), const2),
            pl.BlockSpec((1, w), const2),
        ],
        out_specs=pl.BlockSpec((tq, 2 * w), lambda i: (i, 0)),
        out_shape=jax.ShapeDtypeStruct((s, 2 * w), BF16),
        scratch_shapes=[pltpu.VMEM((tq + CONV_HALO, w), F32), pltpu.VMEM((tq, w), F32),
                        pltpu.VMEM((8, SGU_CHUNK + CONV_HALO, LANES), F32)],
        compiler_params=pltpu.CompilerParams(
            dimension_semantics=("parallel",), vmem_limit_bytes=VMEM_LIMIT),
        name="mix_ab",
    )(proj, proj, proj, proj, proj, proj, vec(sgu_norm), sgu_w, sb, cw, vec(conv_b),
      vec(ln_g), vec(ln_b), vec(on_a), vec(on_b))


ATTN_PH = 4
ATTN_SL = ATTN_TILE // ATTN_PH


def _attn_kernel(q_ref, k_ref, v_ref, qg_ref, kg_ref, o_ref,
                 qn_ref, kn_ref, vv_ref, q4_ref, k4_ref, v4_ref, ob_ref, lb_ref, st_ref, yn_ref,
                 bias_ref):
    t = pl.program_id(1)
    tl, ph, sl = ATTN_TILE, ATTN_PH, ATTN_SL

    @pl.when(t == 0)
    def _():
        kn_ref[0:tl, :] = jnp.zeros((tl, HEAD_DIM), F32)
        vv_ref[0:tl, :] = jnp.zeros((tl, HEAD_DIM), F32)
        k4_ref[:, 0:sl, :] = jnp.zeros((ph, sl, HEAD_DIM), F32)
        v4_ref[:, 0:sl, :] = jnp.zeros((ph, sl, HEAD_DIM), F32)

    @pl.when(t > 0)
    def _():
        kn_ref[0:tl, :] = kn_ref[tl:, :]
        vv_ref[0:tl, :] = vv_ref[tl:, :]
        k4_ref[:, 0:sl, :] = k4_ref[:, sl:, :]
        v4_ref[:, 0:sl, :] = v4_ref[:, sl:, :]

    kn_ref[tl:, :] = _rms(k_ref[...].astype(F32), kg_ref[...])
    vv_ref[tl:, :] = v_ref[...].astype(F32)
    qn_ref[...] = _rms(q_ref[...].astype(F32), qg_ref[...]) * (HEAD_DIM ** -0.5)
    for r in range(ph):
        q4_ref[r] = qn_ref[pl.ds(r, sl, stride=ph), :]
        k4_ref[r, sl:, :] = kn_ref[pl.ds(tl + r, sl, stride=ph), :]
        v4_ref[r, sl:, :] = vv_ref[pl.ds(tl + r, sl, stride=ph), :]

    qi = lax.broadcasted_iota(jnp.int32, (QBLK, 2 * QBLK), 0)
    kj = lax.broadcasted_iota(jnp.int32, (QBLK, 2 * QBLK), 1)
    band = (kj >= qi) & (kj <= qi + QBLK)
    bias_ref[0] = jnp.where(band, 0.0, NEG)
    bias_ref[1] = jnp.where(band & (kj >= jnp.where(t > 0, 0, QBLK)), 0.0, NEG)

    def attend(qb, kb, vb, first):
        sc = lax.dot_general(qb.astype(BF16), kb.astype(BF16), (((1,), (1,)), ((), ())),
                             preferred_element_type=F32)
        sc = sc + bias_ref[1 if first else 0]
        m = jnp.max(sc, axis=-1, keepdims=True)
        p = jnp.exp(sc - m)
        l = jnp.sum(p, axis=-1, keepdims=True)
        ob = jnp.dot(p.astype(BF16), vb.astype(BF16), preferred_element_type=F32) / l
        return ob, jnp.broadcast_to(m + jnp.log(l), (QBLK, HEAD_DIM))

    for b in range(tl // QBLK):
        ob, lse = attend(qn_ref[b * QBLK:(b + 1) * QBLK, :],
                         kn_ref[tl + (b - 1) * QBLK:tl + (b + 1) * QBLK, :],
                         vv_ref[tl + (b - 1) * QBLK:tl + (b + 1) * QBLK, :], b == 0)
        s = b % 2
        st_ref[s, 0] = ob
        st_ref[s, 1] = lse
        n = QBLK // ph
        for r in range(ph):
            ob_ref[0, r, b * n:(b + 1) * n, :] = st_ref[s, 0, pl.ds(r, n, stride=ph), :]
            lb_ref[0, r, b * n:(b + 1) * n, :] = st_ref[s, 1, pl.ds(r, n, stride=ph), :]

    for r in range(ph):
        for b in range(sl // QBLK):
            ob, lse = attend(q4_ref[r, b * QBLK:(b + 1) * QBLK, :],
                             k4_ref[r, sl + (b - 1) * QBLK:sl + (b + 1) * QBLK, :],
                             v4_ref[r, sl + (b - 1) * QBLK:sl + (b + 1) * QBLK, :], b == 0)
            ob_ref[1, r, b * QBLK:(b + 1) * QBLK, :] = ob
            lb_ref[1, r, b * QBLK:(b + 1) * QBLK, :] = lse
        for a in range(ph):
            ob, lse = attend(q4_ref[r, pl.ds(a, QBLK, stride=ph), :],
                             k4_ref[r, pl.ds(a, 2 * QBLK, stride=ph), :],
                             v4_ref[r, pl.ds(a, 2 * QBLK, stride=ph), :], True)
            ob_ref[2, r, pl.ds(a, QBLK, stride=ph), :] = ob
            lb_ref[2, r, pl.ds(a, QBLK, stride=ph), :] = lse

    for r in range(ph):
        for c in range(sl // QBLK):
            rows = slice(c * QBLK, (c + 1) * QBLK)
            l0, l1, l2 = lb_ref[0, r, rows, :], lb_ref[1, r, rows, :], lb_ref[2, r, rows, :]
            mx = jnp.maximum(jnp.maximum(l0, l1), l2)
            w0, w1, w2 = jnp.exp(l0 - mx), jnp.exp(l1 - mx), jnp.exp(l2 - mx)
            y = (w0 * ob_ref[0, r, rows, :] + w1 * ob_ref[1, r, rows, :]
                 + w2 * ob_ref[2, r, rows, :]) / (w0 + w1 + w2)
            yn_ref[pl.ds(c * QBLK * ph + r, QBLK, stride=ph), :] = y
    o_ref[...] = yn_ref[...].astype(o_ref.dtype)


def _attention(proj, q_g, k_g):
    s = proj.shape[0]
    tl = ATTN_TILE
    qc = (2 * SGU_WIDTH + 2 * CONV_CH) // HEAD_DIM
    kc = qc + ATTN_HEADS
    vc = kc + ATTN_HEADS
    return pl.pallas_call(
        _attn_kernel,
        grid=(ATTN_HEADS, s // tl),
        in_specs=[
            pl.BlockSpec((tl, HEAD_DIM), lambda h, t: (t, qc + h)),
            pl.BlockSpec((tl, HEAD_DIM), lambda h, t: (t, kc + h)),
            pl.BlockSpec((tl, HEAD_DIM), lambda h, t: (t, vc + h)),
            pl.BlockSpec((1, HEAD_DIM), lambda h, t: (0, 0)),
            pl.BlockSpec((1, HEAD_DIM), lambda h, t: (0, 0)),
        ],
        out_specs=pl.BlockSpec((tl, HEAD_DIM), lambda h, t: (t, h)),
        out_shape=jax.ShapeDtypeStruct((s, ATTN_WIDTH), BF16),
        scratch_shapes=[
            pltpu.VMEM((tl, HEAD_DIM), F32),
            pltpu.VMEM((2 * tl, HEAD_DIM), F32),
            pltpu.VMEM((2 * tl, HEAD_DIM), F32),
            pltpu.VMEM((ATTN_PH, ATTN_SL, HEAD_DIM), F32),
            pltpu.VMEM((ATTN_PH, 2 * ATTN_SL, HEAD_DIM), F32),
            pltpu.VMEM((ATTN_PH, 2 * ATTN_SL, HEAD_DIM), F32),
            pltpu.VMEM((len(DILATIONS), ATTN_PH, ATTN_SL, HEAD_DIM), F32),
            pltpu.VMEM((len(DILATIONS), ATTN_PH, ATTN_SL, HEAD_DIM), F32),
            pltpu.VMEM((2, 2, QBLK, HEAD_DIM), F32),
            pltpu.VMEM((tl, HEAD_DIM), F32),
            pltpu.VMEM((2, QBLK, 2 * QBLK), F32),
        ],
        compiler_params=pltpu.CompilerParams(
            dimension_semantics=("parallel", "arbitrary"), vmem_limit_bytes=VMEM_LIMIT),
        name="attention",
    )(proj, proj, proj, q_g.reshape(1, HEAD_DIM), k_g.reshape(1, HEAD_DIM))


def _split_bf16(x):
    hi = x.astype(BF16)
    lo = (x - hi.astype(F32)).astype(BF16)
    return hi, lo


def _pack_halves(x):
    c = x.shape[1] // 2
    lo = lax.bitcast_convert_type(x[:, :c].astype(BF16).astype(F32), U32)
    hi = lax.bitcast_convert_type(x[:, c:].astype(BF16).astype(F32), U32)
    return (lo >> 16) | hi


def _unpack_halves(u):
    lo = lax.bitcast_convert_type(u << 16, F32)
    hi = lax.bitcast_convert_type(u & jnp.uint32(0xFFFF0000), F32)
    return lo, hi


def _out_proj_kernel(yab_ref, yc_ref, onc_ref, h_ref, w_ref, n2_ref, rwh_ref, rwl_ref, rb_ref,
                     h1_ref, xp_ref, route_ref, routet_ref, cnt_ref, tri_ref, run_ref, h1s_ref):
    tm = h_ref.shape[0]

    @pl.when(pl.program_id(0) == 0)
    def _():
        r = lax.broadcasted_iota(jnp.int32, (tm, tm), 0)
        c = lax.broadcasted_iota(jnp.int32, (tm, tm), 1)
        tri_ref[...] = jnp.where(c < r, 1.0, 0.0).astype(BF16)
        run_ref[...] = jnp.zeros(run_ref.shape, F32)
        h1s_ref[...] = jnp.zeros(h1s_ref.shape, F32)

    half = yab_ref.shape[1]
    xn = _rms(h1s_ref[...], n2_ref[...])
    xp_ref[...] = _pack_halves(xn)
    xh, xl = _split_bf16(xn)
    lg = (jnp.dot(xh, rwh_ref[...], preferred_element_type=F32)
          + jnp.dot(xl, rwh_ref[...], preferred_element_type=F32)
          + jnp.dot(xh, rwl_ref[...], preferred_element_type=F32)) + rb_ref[...]

    ycn = _rms(yc_ref[...].astype(F32), onc_ref[...]).astype(BF16)
    acc = jnp.dot(yab_ref[...], w_ref[0:half, :], preferred_element_type=F32)
    acc = acc + jnp.dot(ycn, w_ref[half:, :], preferred_element_type=F32)
    h1 = h_ref[...] + acc
    h1_ref[...] = h1

    run = _route_from_logits(lg, run_ref[...], pl.program_id(0) >= 1, route_ref, routet_ref, tri_ref)
    run_ref[...] = run
    cnt_ref[...] = run
    h1s_ref[...] = h1


def _route_from_logits(lg, run, counted, route_ref, routet_ref, tri_ref):
    lane = lax.broadcasted_iota(jnp.int32, lg.shape, 1)
    big = jnp.int32(LANES)
    gl = jnp.where(lane < N_GROUPS, lg, NEG)
    gmax = jnp.max(gl, axis=-1, keepdims=True)
    grp = jnp.min(jnp.where(gl == gmax, lane, big), axis=-1, keepdims=True)
    gate_g = 1.0 / jnp.sum(jnp.exp(gl - gmax), axis=-1, keepdims=True)
    lo = N_GROUPS + grp * EXPERTS_PER_GROUP
    el = jnp.where((lane >= lo) & (lane < lo + EXPERTS_PER_GROUP), lg, NEG)
    v1 = jnp.max(el, axis=-1, keepdims=True)
    i1 = jnp.min(jnp.where(el == v1, lane, big), axis=-1, keepdims=True)
    el2 = jnp.where(lane == i1, NEG, el)
    v2 = jnp.max(el2, axis=-1, keepdims=True)
    i2 = jnp.min(jnp.where(el2 == v2, lane, big), axis=-1, keepdims=True)
    e2 = jnp.exp(v2 - v1)
    g0 = gate_g / (1.0 + e2)
    g1 = gate_g * e2 / (1.0 + e2)
    oh0 = lane == i1 - N_GROUPS
    oh1 = lane == i2 - N_GROUPS
    cnt = jnp.where((oh0 | oh1) & counted, 1.0, 0.0)
    before = jnp.dot(tri_ref[...], cnt.astype(BF16), preferred_element_type=F32) + run
    r0 = jnp.sum(jnp.where(oh0, before, 0.0), axis=-1, keepdims=True)
    r1 = jnp.sum(jnp.where(oh1, before, 0.0), axis=-1, keepdims=True)

    route = (i1 - N_GROUPS).astype(F32)
    for k, val in enumerate(((i2 - N_GROUPS).astype(F32), g0, g1, r0, r1), start=1):
        route = jnp.where(lane == k, val, route)
    route = jnp.where(lane < ROUTE_LANES, route, 0.0)
    route_ref[...] = route
    routet_ref[...] = route.T[0:8, :]
    return run + jnp.sum(cnt, axis=0, keepdims=True)


def _out_proj(y_ab, y_c, on_c, h, w_bf16, norm2, rw, rb, tm=512):
    s, d = h.shape
    half = y_ab.shape[1]
    n = s // tm
    rw_pad = jnp.pad(rw, ((0, 0), (0, LANES - ROUTE_COLS)))
    rwh, rwl = _split_bf16(rw_pad)
    rb_pad = jnp.pad(rb, (0, LANES - ROUTE_COLS)).reshape(1, LANES)
    const = lambda i: (0, 0)
    proj = lambda i: (jnp.minimum(i, n - 1), 0)
    routed = lambda i: (jnp.maximum(i - 1, 0), 0)
    return pl.pallas_call(
        _out_proj_kernel,
        grid=(n + 1,),
        in_specs=[
            pl.BlockSpec((tm, half), proj),
            pl.BlockSpec((tm, half), proj),
            pl.BlockSpec((1, half), const),
            pl.BlockSpec((tm, d), proj),
            pl.BlockSpec((d, d), const, pipeline_mode=pl.Buffered(1)),
            pl.BlockSpec((1, d), const),
            pl.BlockSpec((d, LANES), const),
            pl.BlockSpec((d, LANES), const),
            pl.BlockSpec((1, LANES), const),
        ],
        out_specs=[
            pl.BlockSpec((tm, d), proj),
            pl.BlockSpec((tm, d // 2), routed),
            pl.BlockSpec((tm, LANES), routed),
            pl.BlockSpec((8, tm), lambda i: (0, jnp.maximum(i - 1, 0))),
            pl.BlockSpec((1, LANES), const),
        ],
        out_shape=[
            jax.ShapeDtypeStruct((s, d), F32),
            jax.ShapeDtypeStruct((s, d // 2), U32),
            jax.ShapeDtypeStruct((s, LANES), F32),
            jax.ShapeDtypeStruct((8, s), F32),
            jax.ShapeDtypeStruct((1, LANES), F32),
        ],
        scratch_shapes=[pltpu.VMEM((tm, tm), BF16), pltpu.VMEM((1, LANES), F32),
                        pltpu.VMEM((tm, d), F32)],
        compiler_params=pltpu.CompilerParams(
            dimension_semantics=("arbitrary",), vmem_limit_bytes=VMEM_LIMIT),
        name="out_proj",
    )(y_ab, y_c, on_c.reshape(1, half), h, w_bf16, norm2.reshape(1, d), rwh, rwl, rb_pad)


def _route_plan(route_t, cnt, t):
    nblk = t * TOP_K // MOE_BLK + N_EXPERTS
    ids = jnp.arange(N_EXPERTS, dtype=jnp.int32)
    e = route_t[0:TOP_K].astype(jnp.int32)
    rank = route_t[4:4 + TOP_K].astype(jnp.int32)
    counts = cnt[0, :N_EXPERTS].astype(jnp.int32)
    pcounts = (counts + MOE_BLK - 1) // MOE_BLK * MOE_BLK
    pends = jnp.cumsum(pcounts)
    pstarts = pends - pcounts
    dest = (jnp.sum(jnp.where(e[:, :, None] == ids, pstarts, 0), axis=-1) + rank).reshape(-1)
    blk = jnp.arange(nblk, dtype=jnp.int32)
    blk_e = jnp.minimum(jnp.sum((blk[:, None] * MOE_BLK >= pends[None, :]).astype(jnp.int32), axis=1),
                        N_EXPERTS - 1)
    n_act = pends[-1] // MOE_BLK
    mine = blk_e[:, None] == ids[None, :]
    first = (blk == 0) | (blk_e != jnp.roll(blk_e, 1))
    seg_par = (jnp.cumsum(first.astype(jnp.int32)) - 1) % 2
    nxt_blk = jnp.sum(jnp.where(mine, pends, 0), axis=1) // MOE_BLK
    nxt_e = jnp.where(nxt_blk < n_act,
                      jnp.sum(jnp.where(nxt_blk[:, None] == blk[None, :], blk_e[None, :], 0), axis=1),
                      -1)
    i32 = lambda a: a.astype(jnp.int32)
    return (i32(dest), i32(pstarts + counts), i32(pcounts - counts), i32(blk_e), i32(first),
            i32(seg_par), i32(nxt_e), i32(n_act).reshape(1), nblk)


def _dispatch_kernel(dest_ref, pad_start_ref, pad_n_ref, n_act_ref, xp_ref, xs_ref,
                     buf, zbuf, sem, zsem, *, tm, nsteps, nblk):
    i = pl.program_id(0)
    slot = i % 2

    def wait_slot(sl):
        for _ in range(TOP_K):
            pltpu.make_async_copy(buf.at[sl], xs_ref.at[pl.ds(0, tm)], sem.at[sl]).wait()

    def for_padding_copies(fn):
        sub = 8
        for e in range(N_EXPERTS):
            n = pad_n_ref[e]
            start = pad_start_ref[e]
            head = jnp.minimum((sub - (start & (sub - 1))) & (sub - 1), n)
            for j in range(sub - 1):
                @pl.when(j < head)
                def _(row=start + j):
                    fn(pltpu.make_async_copy(zbuf.at[pl.ds(0, 1)], xs_ref.at[pl.ds(row, 1)], zsem))
            n = n - head
            start = start + head
            p = MOE_BLK // 2
            while p >= sub:
                @pl.when((n & p) != 0)
                def _(start=start, p=p):
                    fn(pltpu.make_async_copy(zbuf.at[pl.ds(0, p)],
                                             xs_ref.at[pl.ds(pl.multiple_of(start, sub), p)], zsem))
                start = start + (n & p)
                p //= 2
        for j in range(nblk - nsteps * tm * TOP_K // MOE_BLK):
            blk = n_act_ref[0] + j

            @pl.when(blk < nblk)
            def _(blk=blk):
                fn(pltpu.make_async_copy(zbuf, xs_ref.at[pl.ds(blk * MOE_BLK, MOE_BLK)], zsem))

    @pl.when(i == 0)
    def _():
        zbuf[...] = jnp.zeros(zbuf.shape, zbuf.dtype)
        for_padding_copies(lambda cp: cp.start())

    @pl.when(i >= 2)
    def _():
        wait_slot(slot)

    buf[slot] = xp_ref[...]
    for rr in range(tm):
        for k in range(TOP_K):
            dst = dest_ref[k * tm * nsteps + i * tm + rr]
            pltpu.make_async_copy(buf.at[slot, pl.ds(rr, 1)], xs_ref.at[pl.ds(dst, 1)],
                                  sem.at[slot]).start()

    @pl.when(i == nsteps - 1)
    def _():
        wait_slot(slot)
        if nsteps > 1:
            wait_slot(1 - slot)
        for_padding_copies(lambda cp: cp.wait())


def _dispatch(dest, pad_start, pad_n, n_act, xp, nblk, tm=512):
    t, c = xp.shape
    nsteps = t // tm
    return pl.pallas_call(
        functools.partial(_dispatch_kernel, tm=tm, nsteps=nsteps, nblk=nblk),
        grid_spec=pltpu.PrefetchScalarGridSpec(
            num_scalar_prefetch=4,
            grid=(nsteps,),
            in_specs=[pl.BlockSpec((tm, c), lambda i, *_: (i, 0))],
            out_specs=pl.BlockSpec(memory_space=pl.ANY),
            scratch_shapes=[pltpu.VMEM((2, tm, c), U32), pltpu.VMEM((MOE_BLK, c), U32),
                            pltpu.SemaphoreType.DMA((2,)), pltpu.SemaphoreType.DMA],
        ),
        out_shape=jax.ShapeDtypeStruct((nblk * MOE_BLK, c), U32),
        compiler_params=pltpu.CompilerParams(
            dimension_semantics=("arbitrary",), vmem_limit_bytes=VMEM_LIMIT),
        name="dispatch",
    )(dest, pad_start, pad_n, n_act, xp)


def _experts_kernel(blk_e_ref, first_ref, par_ref, nxt_ref, n_act_ref,
                    xs_ref, wg_hbm, wu_hbm, wd_hbm, ys_ref,
                    wg_f, wu_f, wd_f, wg_b, wu_b, wd_b, wsem, *, layer):
    i = pl.program_id(0)
    n_act = n_act_ref[0]
    c = xs_ref.shape[1]

    def weight_copies(e, s):
        return (pltpu.make_async_copy(wg_hbm.at[layer, e], wg_f.at[s], wsem.at[s]),
                pltpu.make_async_copy(wu_hbm.at[layer, e], wu_f.at[s], wsem.at[s]),
                pltpu.make_async_copy(wd_hbm.at[layer, e], wd_f.at[s], wsem.at[s]))

    @pl.when(i >= n_act)
    def _():
        ys_ref[...] = jnp.zeros(ys_ref.shape, ys_ref.dtype)

    @pl.when(i < n_act)
    def _():
        s = par_ref[i]

        @pl.when(first_ref[i] == 1)
        def _():
            @pl.when(i == 0)
            def _():
                for cp in weight_copies(blk_e_ref[0], 0):
                    cp.start()

            for cp in weight_copies(0, s):
                cp.wait()
            wg_b[...] = wg_f[s].astype(BF16)
            wu_b[...] = wu_f[s].astype(BF16)
            wd_b[...] = wd_f[s].astype(BF16)

            @pl.when(nxt_ref[i] >= 0)
            def _():
                for cp in weight_copies(nxt_ref[i], 1 - s):
                    cp.start()

        x_lo, x_hi = _unpack_halves(xs_ref[...])
        x_lo = x_lo.astype(BF16)
        x_hi = x_hi.astype(BF16)
        hg = (jnp.dot(x_lo, wg_b[0:c, :], preferred_element_type=F32)
              + jnp.dot(x_hi, wg_b[c:, :], preferred_element_type=F32))
        hu = (jnp.dot(x_lo, wu_b[0:c, :], preferred_element_type=F32)
              + jnp.dot(x_hi, wu_b[c:, :], preferred_element_type=F32))
        hb = (jax.nn.silu(hg) * hu).astype(BF16)
        ys_ref[...] = _pack_halves(jnp.dot(hb, wd_b[...], preferred_element_type=F32))


def _experts(xs, blk_e, first, seg_par, nxt_e, n_act, nblk, wg, wu, wd, layer):
    n_slots, c = xs.shape
    _, _, d, de = wg.shape
    active_rows = lambda i, be, fi, pa, nx, na: (jnp.minimum(i, na[0] - 1), 0)
    return pl.pallas_call(
        functools.partial(_experts_kernel, layer=layer),
        grid_spec=pltpu.PrefetchScalarGridSpec(
            num_scalar_prefetch=5,
            grid=(nblk,),
            in_specs=[
                pl.BlockSpec((MOE_BLK, c), active_rows),
                pl.BlockSpec(memory_space=pl.ANY),
                pl.BlockSpec(memory_space=pl.ANY),
                pl.BlockSpec(memory_space=pl.ANY),
            ],
            out_specs=pl.BlockSpec((MOE_BLK, c), lambda i, *_: (i, 0)),
            scratch_shapes=[
                pltpu.VMEM((2, d, de), F32),
                pltpu.VMEM((2, d, de), F32),
                pltpu.VMEM((2, de, d), F32),
                pltpu.VMEM((d, de), BF16),
                pltpu.VMEM((d, de), BF16),
                pltpu.VMEM((de, d), BF16),
                pltpu.SemaphoreType.DMA((2,)),
            ],
        ),
        out_shape=jax.ShapeDtypeStruct((n_slots, c), U32),
        compiler_params=pltpu.CompilerParams(
            dimension_semantics=("arbitrary",), vmem_limit_bytes=VMEM_LIMIT),
        name="experts",
    )(blk_e, first, seg_par, nxt_e, n_act, xs, wg, wu, wd)


def _combine_kernel(dest_ref, h_ref, route_ref, ys_hbm, *rest, tm, nsteps, emit_next):
    if emit_next:
        gn_ref, o_ref, xn_ref, gbuf, sem = rest
    else:
        o_ref, gbuf, sem = rest
    i = pl.program_id(0)

    def start_gathers(tile, sl):
        for rr in range(tm):
            for k in range(TOP_K):
                src = dest_ref[k * tm * nsteps + tile * tm + rr]
                pltpu.make_async_copy(ys_hbm.at[pl.ds(src, 1)], gbuf.at[sl, pl.ds(k * tm + rr, 1)],
                                      sem.at[sl]).start()

    def wait_gathers(sl):
        for k in range(TOP_K):
            pltpu.make_async_copy(ys_hbm.at[pl.ds(0, tm)], gbuf.at[sl, pl.ds(k * tm, tm)],
                                  sem.at[sl]).wait()

    @pl.when(i < nsteps)
    def _():
        start_gathers(i, i % 2)

    @pl.when(i >= 1)
    def _():
        sl = (i - 1) % 2
        wait_gathers(sl)
        r = route_ref[...]
        lane = lax.broadcasted_iota(jnp.int32, r.shape, 1)
        g0 = jnp.sum(jnp.where(lane == 2, r, 0.0), axis=-1, keepdims=True)
        g1 = jnp.sum(jnp.where(lane == 3, r, 0.0), axis=-1, keepdims=True)
        a_lo, a_hi = _unpack_halves(gbuf[sl, 0:tm])
        b_lo, b_hi = _unpack_halves(gbuf[sl, tm:2 * tm])
        c = a_lo.shape[1]
        h_lo = h_ref[:, 0:c] + (a_lo * g0 + b_lo * g1)
        h_hi = h_ref[:, c:] + (a_hi * g0 + b_hi * g1)
        o_ref[:, 0:c] = h_lo
        o_ref[:, c:] = h_hi
        if emit_next:
            ms = (jnp.sum(h_lo * h_lo, axis=-1, keepdims=True)
                  + jnp.sum(h_hi * h_hi, axis=-1, keepdims=True)) / (2 * c)
            inv = lax.rsqrt(ms + EPS)
            xn_ref[:, 0:c] = (h_lo * inv * gn_ref[:, 0:c]).astype(BF16)
            xn_ref[:, c:] = (h_hi * inv * gn_ref[:, c:]).astype(BF16)


def _combine(dest, h1, route, ys, next_gain=None, tm=512):
    t, d = h1.shape
    c = ys.shape[1]
    nsteps = t // tm
    emit_next = next_gain is not None
    prev = lambda i, dst: (jnp.maximum(i - 1, 0), 0)
    in_specs = [
        pl.BlockSpec((tm, d), prev),
        pl.BlockSpec((tm, LANES), prev),
        pl.BlockSpec(memory_space=pl.ANY),
    ]
    out_specs = [pl.BlockSpec((tm, d), prev)]
    out_shape = [jax.ShapeDtypeStruct((t, d), F32)]
    args = [dest, h1, route, ys]
    if emit_next:
        in_specs.append(pl.BlockSpec((1, d), lambda i, dst: (0, 0)))
        out_specs.append(pl.BlockSpec((tm, d), prev))
        out_shape.append(jax.ShapeDtypeStruct((t, d), BF16))
        args.append(next_gain.reshape(1, d))
    return pl.pallas_call(
        functools.partial(_combine_kernel, tm=tm, nsteps=nsteps, emit_next=emit_next),
        grid_spec=pltpu.PrefetchScalarGridSpec(
            num_scalar_prefetch=1,
            grid=(nsteps + 1,),
            in_specs=in_specs,
            out_specs=out_specs,
            scratch_shapes=[pltpu.VMEM((2, TOP_K * tm, c), U32), pltpu.SemaphoreType.DMA((2,))],
        ),
        out_shape=out_shape,
        compiler_params=pltpu.CompilerParams(
            dimension_semantics=("arbitrary",), vmem_limit_bytes=VMEM_LIMIT),
        name="combine",
    )(*args)


def kernel(x, norm1, w_in, sgu_norm, sgu_w, sgu_b, conv_w, conv_b, conv_ln_g, conv_ln_b,
           q_norm, k_norm, out_norm, w_out, norm2, w_router_group, b_router_group,
           w_router_expert, b_router_expert, w_expert_gate, w_expert_up, w_expert_down):
    b, s, d = x.shape
    depth = norm1.shape[0]
    t = b * s
    h = x.reshape(t, d)
    a_end, b_end = SGU_WIDTH, SGU_WIDTH + CONV_CH
    x_in = h
    for l in range(depth):
        proj = _in_proj(x_in, norm1[l], w_in, l)
        y_ab = _mix_ab(proj, sgu_norm[l], sgu_w[l], sgu_b[l], conv_w[l], conv_b[l],
                       conv_ln_g[l], conv_ln_b[l], out_norm[l, :a_end], out_norm[l, a_end:b_end])
        y_c = _attention(proj, q_norm[l], k_norm[l])
        rw = jnp.concatenate([w_router_group[l], w_router_expert[l]], axis=1)
        rb = jnp.concatenate([b_router_group[l], b_router_expert[l]], axis=0)
        h1, xp, route, route_t, cnt = _out_proj(y_ab, y_c, out_norm[l, b_end:], h,
                                                w_out[l].astype(BF16), norm2[l], rw, rb)
        (dest, pad_start, pad_n, blk_e, first, seg_par, nxt_e, n_act,
         nblk) = _route_plan(route_t, cnt, t)
        xs = _dispatch(dest, pad_start, pad_n, n_act, xp, nblk)
        ys = _experts(xs, blk_e, first, seg_par, nxt_e, n_act, nblk,
                      w_expert_gate, w_expert_up, w_expert_down, l)
        if l + 1 < depth:
            h, x_in = _combine(dest, h1, route, ys, next_gain=norm1[l + 1])
        else:
            (h,) = _combine(dest, h1, route, ys)
    return h.reshape(b, s, d)
```

```python
import functools

import jax
import jax.numpy as jnp
from jax import lax
from jax.experimental import pallas as pl
from jax.experimental.pallas import tpu as pltpu

F32 = jnp.float32
BF16 = jnp.bfloat16
U32 = jnp.uint32

D_MODEL = 2048
SGU_WIDTH = 512
SGU_HEADS = 4
SGU_CHUNK = 128
CONV_CH = 512
CONV_K = 31
ATTN_WIDTH = 1024
ATTN_HEADS = 8
HEAD_DIM = 128
DILATIONS = (1, 4, 16)
QBLK = 128
D_IN = 2 * SGU_WIDTH + 2 * CONV_CH + 3 * ATTN_WIDTH
N_GROUPS = 4
EXPERTS_PER_GROUP = 8
N_EXPERTS = 32
TOP_K = 2
D_EXPERT = 512
EPS = 1e-6

LANES = 128
VMEM_LIMIT = 56 * 1024 * 1024
NEG = -1e30

ATTN_TILE = QBLK * max(DILATIONS)
CONV_HALO = 32
MOE_BLK = 256
ROUTE_COLS = N_GROUPS + N_EXPERTS
ROUTE_LANES = 6


def _rms(x, g):
    return x * lax.rsqrt(jnp.mean(x * x, axis=-1, keepdims=True) + EPS) * g


def _in_proj_norm_kernel(x_ref, g_ref, w_ref, o_ref, xn_ref):
    @pl.when(pl.program_id(1) == 0)
    def _():
        xn_ref[...] = _rms(x_ref[...], g_ref[...]).astype(BF16)

    o_ref[...] = jnp.dot(xn_ref[...], w_ref[...].astype(BF16),
                         preferred_element_type=F32).astype(o_ref.dtype)


def _in_proj_kernel(xn_ref, w_ref, o_ref):
    o_ref[...] = jnp.dot(xn_ref[...], w_ref[...].astype(BF16),
                         preferred_element_type=F32).astype(o_ref.dtype)


def _in_proj(x, g, w_in, layer, tn=1024):
    s, d = x.shape
    n = w_in.shape[2]
    prenormed = x.dtype == BF16
    tm = 2048 if prenormed else 1024
    x_spec = pl.BlockSpec((tm, d), lambda i, j: (i, 0))
    w_spec = pl.BlockSpec((None, d, tn), lambda i, j: (layer, 0, j))
    common = dict(
        grid=(s // tm, n // tn),
        out_specs=pl.BlockSpec((tm, tn), lambda i, j: (i, j)),
        out_shape=jax.ShapeDtypeStruct((s, n), BF16),
        compiler_params=pltpu.CompilerParams(
            dimension_semantics=("parallel", "arbitrary"), vmem_limit_bytes=VMEM_LIMIT),
        name="in_proj",
    )
    if prenormed:
        return pl.pallas_call(_in_proj_kernel, in_specs=[x_spec, w_spec], **common)(x, w_in)
    return pl.pallas_call(
        _in_proj_norm_kernel,
        in_specs=[x_spec, pl.BlockSpec((1, d), lambda i, j: (0, 0)), w_spec],
        scratch_shapes=[pltpu.VMEM((tm, d), BF16)],
        **common,
    )(x, g.reshape(1, d), w_in)


def _mix_ab_kernel(u_ref, v_ref, a_ref, gt_ref, ah_ref, gh_ref,
                   sgn_ref, sw_ref, sb_ref, cw_ref, cb_ref, lng_ref, lnb_ref,
                   ona_ref, onb_ref, o_ref, zext_ref, conv_ref, stage_ref, *, tq):
    u = jax.nn.gelu(u_ref[...].astype(F32))
    v = _rms(jax.nn.gelu(v_ref[...].astype(F32)), sgn_ref[...]).astype(BF16)
    row = lax.broadcasted_iota(jnp.int32, (SGU_CHUNK, SGU_CHUNK), 0)
    col = lax.broadcasted_iota(jnp.int32, (SGU_CHUNK, SGU_CHUNK), 1)
    causal = col <= row
    wm = [jnp.where(causal, sw_ref[hh], 0.0).astype(BF16) for hh in range(SGU_HEADS)]
    for c in range(tq // SGU_CHUNK):
        rows = slice(c * SGU_CHUNK, (c + 1) * SGU_CHUNK)
        zs = []
        for hh in range(SGU_HEADS):
            cols = slice(hh * HEAD_DIM, (hh + 1) * HEAD_DIM)
            zs.append(jnp.dot(wm[hh], v[rows, cols], preferred_element_type=F32) + sb_ref[hh])
        ya = u[rows, :] * jnp.concatenate(zs, axis=1)
        o_ref[rows, 0:SGU_WIDTH] = _rms(ya, ona_ref[...]).astype(o_ref.dtype)

    first = pl.program_id(0) == 0
    zh = ah_ref[...].astype(F32) * jax.nn.sigmoid(gh_ref[...].astype(F32))
    zext_ref[0:CONV_HALO, :] = jnp.where(first, 0.0, zh)
    zext_ref[CONV_HALO:, :] = a_ref[...].astype(F32) * jax.nn.sigmoid(gt_ref[...].astype(F32))
    off0 = CONV_HALO - (CONV_K - 1)
    sub = 8
    rc = 128
    for j in range(CONV_CH // LANES):
        cols = slice(j * LANES, (j + 1) * LANES)
        for c in range(tq // rc):
            acc = jnp.zeros((rc, LANES), F32) + cb_ref[:, cols]
            for b in range(sub):
                taps = [(a, sub * a + b - off0) for a in range(CONV_HALO // sub + 1)
                        if 0 <= sub * a + b - off0 < CONV_K]
                rows = rc + sub * taps[-1][0]
                stage_ref[b, 0:rows, :] = zext_ref[pl.ds(c * rc + b, rows), cols]
                for a, k in taps:
                    acc = acc + stage_ref[b, sub * a:sub * a + rc, :] * cw_ref[k:k + 1, cols]
            conv_ref[c * rc:(c + 1) * rc, cols] = acc
    rc = 32
    for c in range(tq // rc):
        acc = conv_ref[c * rc:(c + 1) * rc, :]
        mu = jnp.mean(acc, axis=-1, keepdims=True)
        xc = acc - mu
        y = xc * lax.rsqrt(jnp.mean(xc * xc, axis=-1, keepdims=True) + EPS)
        y = jax.nn.silu(y * lng_ref[...] + lnb_ref[...])
        o_ref[c * rc:(c + 1) * rc, SGU_WIDTH:] = _rms(y, onb_ref[...]).astype(o_ref.dtype)


def _mix_ab(proj, sgu_norm, sgu_w, sgu_b, conv_w, conv_b, ln_g, ln_b, on_a, on_b, tq=512):
    s = proj.shape[0]
    w = SGU_WIDTH
    hb = tq // CONV_HALO
    sb = jnp.broadcast_to(sgu_b[:, :, None], (SGU_HEADS, SGU_CHUNK, HEAD_DIM))
    cw = jnp.pad(conv_w, ((0, 32 - CONV_K), (0, 0)))
    vec = lambda a: a.reshape(1, -1)
    const2 = lambda i: (0, 0)
    const3 = lambda i: (0, 0, 0)
    return pl.pallas_call(
        functools.partial(_mix_ab_kernel, tq=tq),
        grid=(s // tq,),
        in_specs=[
            pl.BlockSpec((tq, w), lambda i: (i, 0)),
            pl.BlockSpec((tq, w), lambda i: (i, 1)),
            pl.BlockSpec((tq, w), lambda i: (i, 2)),
            pl.BlockSpec((tq, w), lambda i: (i, 3)),
            pl.BlockSpec((CONV_HALO, w), lambda i: (jnp.maximum(i * hb - 1, 0), 2)),
            pl.BlockSpec((CONV_HALO, w), lambda i: (jnp.maximum(i * hb - 1, 0), 3)),
            pl.BlockSpec((1, w), const2),
            pl.BlockSpec((SGU_HEADS, SGU_CHUNK, SGU_CHUNK), const3),
            pl.BlockSpec((SGU_HEADS, SGU_CHUNK, HEAD_DIM), const3),
            pl.BlockSpec((32, w), const2),
            pl.BlockSpec((1, w), const2),
            pl.BlockSpec((1, w), const2),
            pl.BlockSpec((1, w), const2),
            pl.BlockSpec((1, w), const2),
            pl.BlockSpec((1, w), const2),
        ],
        out_specs=pl.BlockSpec((tq, 2 * w), lambda i: (i, 0)),
        out_shape=jax.ShapeDtypeStruct((s, 2 * w), BF16),
        scratch_shapes=[pltpu.VMEM((tq + CONV_HALO, w), F32), pltpu.VMEM((tq, w), F32),
                        pltpu.VMEM((8, SGU_CHUNK + CONV_HALO, LANES), F32)],
        compiler_params=pltpu.CompilerParams(
            dimension_semantics=("parallel",), vmem_limit_bytes=VMEM_LIMIT),
        name="mix_ab",
    )(proj, proj, proj, proj, proj, proj, vec(sgu_norm), sgu_w, sb, cw, vec(conv_b),
      vec(ln_g), vec(ln_b), vec(on_a), vec(on_b))


ATTN_PH = 4
ATTN_SL = ATTN_TILE // ATTN_PH


def _attn_kernel(q_ref, k_ref, v_ref, qg_ref, kg_ref, o_ref,
                 qn_ref, kn_ref, vv_ref, q4_ref, k4_ref, v4_ref, ob_ref, lb_ref, st_ref, yn_ref,
                 bias_ref):
    t = pl.program_id(1)
    tl, ph, sl = ATTN_TILE, ATTN_PH, ATTN_SL

    @pl.when(t == 0)
    def _():
        kn_ref[0:tl, :] = jnp.zeros((tl, HEAD_DIM), F32)
        vv_ref[0:tl, :] = jnp.zeros((tl, HEAD_DIM), F32)
        k4_ref[:, 0:sl, :] = jnp.zeros((ph, sl, HEAD_DIM), F32)
        v4_ref[:, 0:sl, :] = jnp.zeros((ph, sl, HEAD_DIM), F32)

    @pl.when(t > 0)
    def _():
        kn_ref[0:tl, :] = kn_ref[tl:, :]
        vv_ref[0:tl, :] = vv_ref[tl:, :]
        k4_ref[:, 0:sl, :] = k4_ref[:, sl:, :]
        v4_ref[:, 0:sl, :] = v4_ref[:, sl:, :]

    kn_ref[tl:, :] = _rms(k_ref[...].astype(F32), kg_ref[...])
    vv_ref[tl:, :] = v_ref[...].astype(F32)
    qn_ref[...] = _rms(q_ref[...].astype(F32), qg_ref[...]) * (HEAD_DIM ** -0.5)
    for r in range(ph):
        q4_ref[r] = qn_ref[pl.ds(r, sl, stride=ph), :]
        k4_ref[r, sl:, :] = kn_ref[pl.ds(tl + r, sl, stride=ph), :]
        v4_ref[r, sl:, :] = vv_ref[pl.ds(tl + r, sl, stride=ph), :]

    qi = lax.broadcasted_iota(jnp.int32, (QBLK, 2 * QBLK), 0)
    kj = lax.broadcasted_iota(jnp.int32, (QBLK, 2 * QBLK), 1)
    band = (kj >= qi) & (kj <= qi + QBLK)
    bias_ref[0] = jnp.where(band, 0.0, NEG)
    bias_ref[1] = jnp.where(band & (kj >= jnp.where(t > 0, 0, QBLK)), 0.0, NEG)

    def attend(qb, kb, vb, first):
        sc = lax.dot_general(qb.astype(BF16), kb.astype(BF16), (((1,), (1,)), ((), ())),
                             preferred_element_type=F32)
        sc = sc + bias_ref[1 if first else 0]
        m = jnp.max(sc, axis=-1, keepdims=True)
        p = jnp.exp(sc - m)
        l = jnp.sum(p, axis=-1, keepdims=True)
        ob = jnp.dot(p.astype(BF16), vb.astype(BF16), preferred_element_type=F32) / l
        return ob, jnp.broadcast_to(m + jnp.log(l), (QBLK, HEAD_DIM))

    for b in range(tl // QBLK):
        ob, lse = attend(qn_ref[b * QBLK:(b + 1) * QBLK, :],
                         kn_ref[tl + (b - 1) * QBLK:tl + (b + 1) * QBLK, :],
                         vv_ref[tl + (b - 1) * QBLK:tl + (b + 1) * QBLK, :], b == 0)
        s = b % 2
        st_ref[s, 0] = ob
        st_ref[s, 1] = lse
        n = QBLK // ph
        for r in range(ph):
            ob_ref[0, r, b * n:(b + 1) * n, :] = st_ref[s, 0, pl.ds(r, n, stride=ph), :]
            lb_ref[0, r, b * n:(b + 1) * n, :] = st_ref[s, 1, pl.ds(r, n, stride=ph), :]

    for r in range(ph):
        for b in range(sl // QBLK):
            ob, lse = attend(q4_ref[r, b * QBLK:(b + 1) * QBLK, :],
                             k4_ref[r, sl + (b - 1) * QBLK:sl + (b + 1) * QBLK, :],
                             v4_ref[r, sl + (b - 1) * QBLK:sl + (b + 1) * QBLK, :], b == 0)
            ob_ref[1, r, b * QBLK:(b + 1) * QBLK, :] = ob
            lb_ref[1, r, b * QBLK:(b + 1) * QBLK, :] = lse
        for a in range(ph):
            ob, lse = attend(q4_ref[r, pl.ds(a, QBLK, stride=ph), :],
                             k4_ref[r, pl.ds(a, 2 * QBLK, stride=ph), :],
                             v4_ref[r, pl.ds(a, 2 * QBLK, stride=ph), :], True)
            ob_ref[2, r, pl.ds(a, QBLK, stride=ph), :] = ob
            lb_ref[2, r, pl.ds(a, QBLK, stride=ph), :] = lse

    for r in range(ph):
        for c in range(sl // QBLK):
            rows = slice(c * QBLK, (c + 1) * QBLK)
            l0, l1, l2 = lb_ref[0, r, rows, :], lb_ref[1, r, rows, :], lb_ref[2, r, rows, :]
            mx = jnp.maximum(jnp.maximum(l0, l1), l2)
            w0, w1, w2 = jnp.exp(l0 - mx), jnp.exp(l1 - mx), jnp.exp(l2 - mx)
            y = (w0 * ob_ref[0, r, rows, :] + w1 * ob_ref[1, r, rows, :]
                 + w2 * ob_ref[2, r, rows, :]) / (w0 + w1 + w2)
            yn_ref[pl.ds(c * QBLK * ph + r, QBLK, stride=ph), :] = y
    o_ref[...] = yn_ref[...].astype(o_ref.dtype)


def _attention(proj, q_g, k_g):
    s = proj.shape[0]
    tl = ATTN_TILE
    qc = (2 * SGU_WIDTH + 2 * CONV_CH) // HEAD_DIM
    kc = qc + ATTN_HEADS
    vc = kc + ATTN_HEADS
    return pl.pallas_call(
        _attn_kernel,
        grid=(ATTN_HEADS, s // tl),
        in_specs=[
            pl.BlockSpec((tl, HEAD_DIM), lambda h, t: (t, qc + h)),
            pl.BlockSpec((tl, HEAD_DIM), lambda h, t: (t, kc + h)),
            pl.BlockSpec((tl, HEAD_DIM), lambda h, t: (t, vc + h)),
            pl.BlockSpec((1, HEAD_DIM), lambda h, t: (0, 0)),
            pl.BlockSpec((1, HEAD_DIM), lambda h, t: (0, 0)),
        ],
        out_specs=pl.BlockSpec((tl, HEAD_DIM), lambda h, t: (t, h)),
        out_shape=jax.ShapeDtypeStruct((s, ATTN_WIDTH), BF16),
        scratch_shapes=[
            pltpu.VMEM((tl, HEAD_DIM), F32),
            pltpu.VMEM((2 * tl, HEAD_DIM), F32),
            pltpu.VMEM((2 * tl, HEAD_DIM), F32),
            pltpu.VMEM((ATTN_PH, ATTN_SL, HEAD_DIM), F32),
            pltpu.VMEM((ATTN_PH, 2 * ATTN_SL, HEAD_DIM), F32),
            pltpu.VMEM((ATTN_PH, 2 * ATTN_SL, HEAD_DIM), F32),
            pltpu.VMEM((len(DILATIONS), ATTN_PH, ATTN_SL, HEAD_DIM), F32),
            pltpu.VMEM((len(DILATIONS), ATTN_PH, ATTN_SL, HEAD_DIM), F32),
            pltpu.VMEM((2, 2, QBLK, HEAD_DIM), F32),
            pltpu.VMEM((tl, HEAD_DIM), F32),
            pltpu.VMEM((2, QBLK, 2 * QBLK), F32),
        ],
        compiler_params=pltpu.CompilerParams(
            dimension_semantics=("parallel", "arbitrary"), vmem_limit_bytes=VMEM_LIMIT),
        name="attention",
    )(proj, proj, proj, q_g.reshape(1, HEAD_DIM), k_g.reshape(1, HEAD_DIM))


def _split_bf16(x):
    hi = x.astype(BF16)
    lo = (x - hi.astype(F32)).astype(BF16)
    return hi, lo


def _pack_halves(x):
    c = x.shape[1] // 2
    lo = lax.bitcast_convert_type(x[:, :c].astype(BF16).astype(F32), U32)
    hi = lax.bitcast_convert_type(x[:, c:].astype(BF16).astype(F32), U32)
    return (lo >> 16) | hi


def _unpack_halves(u):
    lo = lax.bitcast_convert_type(u << 16, F32)
    hi = lax.bitcast_convert_type(u & jnp.uint32(0xFFFF0000), F32)
    return lo, hi


def _out_proj_kernel(yab_ref, yc_ref, onc_ref, h_ref, w_ref, n2_ref, rwh_ref, rwl_ref, rb_ref,
                     h1_ref, xp_ref, route_ref, routet_ref, cnt_ref, tri_ref, run_ref, h1s_ref):
    tm = h_ref.shape[0]

    @pl.when(pl.program_id(0) == 0)
    def _():
        r = lax.broadcasted_iota(jnp.int32, (tm, tm), 0)
        c = lax.broadcasted_iota(jnp.int32, (tm, tm), 1)
        tri_ref[...] = jnp.where(c < r, 1.0, 0.0).astype(BF16)
        run_ref[...] = jnp.zeros(run_ref.shape, F32)
        h1s_ref[...] = jnp.zeros(h1s_ref.shape, F32)

    half = yab_ref.shape[1]
    xn = _rms(h1s_ref[...], n2_ref[...])
    xp_ref[...] = _pack_halves(xn)
    xh, xl = _split_bf16(xn)
    lg = (jnp.dot(xh, rwh_ref[...], preferred_element_type=F32)
          + jnp.dot(xl, rwh_ref[...], preferred_element_type=F32)
          + jnp.dot(xh, rwl_ref[...], preferred_element_type=F32)) + rb_ref[...]

    ycn = _rms(yc_ref[...].astype(F32), onc_ref[...]).astype(BF16)
    acc = jnp.dot(yab_ref[...], w_ref[0:half, :], preferred_element_type=F32)
    acc = acc + jnp.dot(ycn, w_ref[half:, :], preferred_element_type=F32)
    h1 = h_ref[...] + acc
    h1_ref[...] = h1

    run = _route_from_logits(lg, run_ref[...], pl.program_id(0) >= 1, route_ref, routet_ref, tri_ref)
    run_ref[...] = run
    cnt_ref[...] = run
    h1s_ref[...] = h1


def _route_from_logits(lg, run, counted, route_ref, routet_ref, tri_ref):
    lane = lax.broadcasted_iota(jnp.int32, lg.shape, 1)
    big = jnp.int32(LANES)
    gl = jnp.where(lane < N_GROUPS, lg, NEG)
    gmax = jnp.max(gl, axis=-1, keepdims=True)
    grp = jnp.min(jnp.where(gl == gmax, lane, big), axis=-1, keepdims=True)
    gate_g = 1.0 / jnp.sum(jnp.exp(gl - gmax), axis=-1, keepdims=True)
    lo = N_GROUPS + grp * EXPERTS_PER_GROUP
    el = jnp.where((lane >= lo) & (lane < lo + EXPERTS_PER_GROUP), lg, NEG)
    v1 = jnp.max(el, axis=-1, keepdims=True)
    i1 = jnp.min(jnp.where(el == v1, lane, big), axis=-1, keepdims=True)
    el2 = jnp.where(lane == i1, NEG, el)
    v2 = jnp.max(el2, axis=-1, keepdims=True)
    i2 = jnp.min(jnp.where(el2 == v2, lane, big), axis=-1, keepdims=True)
    e2 = jnp.exp(v2 - v1)
    g0 = gate_g / (1.0 + e2)
    g1 = gate_g * e2 / (1.0 + e2)
    oh0 = lane == i1 - N_GROUPS
    oh1 = lane == i2 - N_GROUPS
    cnt = jnp.where((oh0 | oh1) & counted, 1.0, 0.0)
    before = jnp.dot(tri_ref[...], cnt.astype(BF16), preferred_element_type=F32) + run
    r0 = jnp.sum(jnp.where(oh0, before, 0.0), axis=-1, keepdims=True)
    r1 = jnp.sum(jnp.where(oh1, before, 0.0), axis=-1, keepdims=True)

    route = (i1 - N_GROUPS).astype(F32)
    for k, val in enumerate(((i2 - N_GROUPS).astype(F32), g0, g1, r0, r1), start=1):
        route = jnp.where(lane == k, val, route)
    route = jnp.where(lane < ROUTE_LANES, route, 0.0)
    route_ref[...] = route
    routet_ref[...] = route.T[0:8, :]
    return run + jnp.sum(cnt, axis=0, keepdims=True)


def _out_proj(y_ab, y_c, on_c, h, w_bf16, norm2, rw, rb, tm=512):
    s, d = h.shape
    half = y_ab.shape[1]
    n = s // tm
    rw_pad = jnp.pad(rw, ((0, 0), (0, LANES - ROUTE_COLS)))
    rwh, rwl = _split_bf16(rw_pad)
    rb_pad = jnp.pad(rb, (0, LANES - ROUTE_COLS)).reshape(1, LANES)
    const = lambda i: (0, 0)
    proj = lambda i: (jnp.minimum(i, n - 1), 0)
    routed = lambda i: (jnp.maximum(i - 1, 0), 0)
    return pl.pallas_call(
        _out_proj_kernel,
        grid=(n + 1,),
        in_specs=[
            pl.BlockSpec((tm, half), proj),
            pl.BlockSpec((tm, half), proj),
            pl.BlockSpec((1, half), const),
            pl.BlockSpec((tm, d), proj),
            pl.BlockSpec((d, d), const, pipeline_mode=pl.Buffered(1)),
            pl.BlockSpec((1, d), const),
            pl.BlockSpec((d, LANES), const),
            pl.BlockSpec((d, LANES), const),
            pl.BlockSpec((1, LANES), const),
        ],
        out_specs=[
            pl.BlockSpec((tm, d), proj),
            pl.BlockSpec((tm, d // 2), routed),
            pl.BlockSpec((tm, LANES), routed),
            pl.BlockSpec((8, tm), lambda i: (0, jnp.maximum(i - 1, 0))),
            pl.BlockSpec((1, LANES), const),
        ],
        out_shape=[
            jax.ShapeDtypeStruct((s, d), F32),
            jax.ShapeDtypeStruct((s, d // 2), U32),
            jax.ShapeDtypeStruct((s, LANES), F32),
            jax.ShapeDtypeStruct((8, s), F32),
            jax.ShapeDtypeStruct((1, LANES), F32),
        ],
        scratch_shapes=[pltpu.VMEM((tm, tm), BF16), pltpu.VMEM((1, LANES), F32),
                        pltpu.VMEM((tm, d), F32)],
        compiler_params=pltpu.CompilerParams(
            dimension_semantics=("arbitrary",), vmem_limit_bytes=VMEM_LIMIT),
        name="out_proj",
    )(y_ab, y_c, on_c.reshape(1, half), h, w_bf16, norm2.reshape(1, d), rwh, rwl, rb_pad)


def _route_plan(route_t, cnt, t):
    nblk = t * TOP_K // MOE_BLK + N_EXPERTS
    ids = jnp.arange(N_EXPERTS, dtype=jnp.int32)
    e = route_t[0:TOP_K].astype(jnp.int32)
    rank = route_t[4:4 + TOP_K].astype(jnp.int32)
    counts = cnt[0, :N_EXPERTS].astype(jnp.int32)
    pcounts = (counts + MOE_BLK - 1) // MOE_BLK * MOE_BLK
    pends = jnp.cumsum(pcounts)
    pstarts = pends - pcounts
    dest = (jnp.sum(jnp.where(e[:, :, None] == ids, pstarts, 0), axis=-1) + rank).reshape(-1)
    blk = jnp.arange(nblk, dtype=jnp.int32)
    blk_e = jnp.minimum(jnp.sum((blk[:, None] * MOE_BLK >= pends[None, :]).astype(jnp.int32), axis=1),
                        N_EXPERTS - 1)
    n_act = pends[-1] // MOE_BLK
    mine = blk_e[:, None] == ids[None, :]
    first = (blk == 0) | (blk_e != jnp.roll(blk_e, 1))
    seg_par = (jnp.cumsum(first.astype(jnp.int32)) - 1) % 2
    nxt_blk = jnp.sum(jnp.where(mine, pends, 0), axis=1) // MOE_BLK
    nxt_e = jnp.where(nxt_blk < n_act,
                      jnp.sum(jnp.where(nxt_blk[:, None] == blk[None, :], blk_e[None, :], 0), axis=1),
                      -1)
    i32 = lambda a: a.astype(jnp.int32)
    return (i32(dest), i32(pstarts + counts), i32(pcounts - counts), i32(blk_e), i32(first),
            i32(seg_par), i32(nxt_e), i32(n_act).reshape(1), nblk)


def _dispatch_kernel(dest_ref, pad_start_ref, pad_n_ref, n_act_ref, xp_ref, xs_ref,
                     buf, zbuf, sem, zsem, *, tm, nsteps, nblk):
    i = pl.program_id(0)
    slot = i % 2

    def wait_slot(sl):
        for _ in range(TOP_K):
            pltpu.make_async_copy(buf.at[sl], xs_ref.at[pl.ds(0, tm)], sem.at[sl]).wait()

    def for_padding_copies(fn):
        sub = 8
        for e in range(N_EXPERTS):
            n = pad_n_ref[e]
            start = pad_start_ref[e]
            head = jnp.minimum((sub - (start & (sub - 1))) & (sub - 1), n)
            for j in range(sub - 1):
                @pl.when(j < head)
                def _(row=start + j):
                    fn(pltpu.make_async_copy(zbuf.at[pl.ds(0, 1)], xs_ref.at[pl.ds(row, 1)], zsem))
            n = n - head
            start = start + head
            p = MOE_BLK // 2
            while p >= sub:
                @pl.when((n & p) != 0)
                def _(start=start, p=p):
                    fn(pltpu.make_async_copy(zbuf.at[pl.ds(0, p)],
                                             xs_ref.at[pl.ds(pl.multiple_of(start, sub), p)], zsem))
                start = start + (n & p)
                p //= 2
        for j in range(nblk - nsteps * tm * TOP_K // MOE_BLK):
            blk = n_act_ref[0] + j

            @pl.when(blk < nblk)
            def _(blk=blk):
                fn(pltpu.make_async_copy(zbuf, xs_ref.at[pl.ds(blk * MOE_BLK, MOE_BLK)], zsem))

    @pl.when(i == 0)
    def _():
        zbuf[...] = jnp.zeros(zbuf.shape, zbuf.dtype)
        for_padding_copies(lambda cp: cp.start())

    @pl.when(i >= 2)
    def _():
        wait_slot(slot)

    buf[slot] = xp_ref[...]
    for rr in range(tm):
        for k in range(TOP_K):
            dst = dest_ref[k * tm * nsteps + i * tm + rr]
            pltpu.make_async_copy(buf.at[slot, pl.ds(rr, 1)], xs_ref.at[pl.ds(dst, 1)],
                                  sem.at[slot]).start()

    @pl.when(i == nsteps - 1)
    def _():
        wait_slot(slot)
        if nsteps > 1:
            wait_slot(1 - slot)
        for_padding_copies(lambda cp: cp.wait())


def _dispatch(dest, pad_start, pad_n, n_act, xp, nblk, tm=256):
    t, c = xp.shape
    nsteps = t // tm
    return pl.pallas_call(
        functools.partial(_dispatch_kernel, tm=tm, nsteps=nsteps, nblk=nblk),
        grid_spec=pltpu.PrefetchScalarGridSpec(
            num_scalar_prefetch=4,
            grid=(nsteps,),
            in_specs=[pl.BlockSpec((tm, c), lambda i, *_: (i, 0))],
            out_specs=pl.BlockSpec(memory_space=pl.ANY),
            scratch_shapes=[pltpu.VMEM((2, tm, c), U32), pltpu.VMEM((MOE_BLK, c), U32),
                            pltpu.SemaphoreType.DMA((2,)), pltpu.SemaphoreType.DMA],
        ),
        out_shape=jax.ShapeDtypeStruct((nblk * MOE_BLK, c), U32),
        compiler_params=pltpu.CompilerParams(
            dimension_semantics=("arbitrary",), vmem_limit_bytes=VMEM_LIMIT),
        name="dispatch",
    )(dest, pad_start, pad_n, n_act, xp)


def _experts_kernel(blk_e_ref, first_ref, last_ref, par_ref, nxt_ref, n_act_ref,
                    xs_ref, wg_hbm, wu_hbm, wd_hbm, ys_ref,
                    wg_f, wu_f, wd_f, wg_b, wu_b, wd_b, wsem, *, layer):
    i = pl.program_id(0)
    n_act = n_act_ref[0]
    c = xs_ref.shape[1]

    def weight_copies(e, s):
        return (pltpu.make_async_copy(wg_hbm.at[layer, e], wg_f.at[s], wsem.at[s]),
                pltpu.make_async_copy(wu_hbm.at[layer, e], wu_f.at[s], wsem.at[s]),
                pltpu.make_async_copy(wd_hbm.at[layer, e], wd_f.at[s], wsem.at[s]))

    @pl.when(i >= n_act)
    def _():
        ys_ref[...] = jnp.zeros(ys_ref.shape, ys_ref.dtype)

    def cast_weights(s):
        wg_b[s] = wg_f[s].astype(BF16)
        wu_b[s] = wu_f[s].astype(BF16)
        wd_b[s] = wd_f[s].astype(BF16)

    def compute_block(s):
        x_lo, x_hi = _unpack_halves(xs_ref[...])
        x_lo = x_lo.astype(BF16)
        x_hi = x_hi.astype(BF16)
        hg = (jnp.dot(x_lo, wg_b[s, 0:c, :], preferred_element_type=F32)
              + jnp.dot(x_hi, wg_b[s, c:, :], preferred_element_type=F32))
        hu = (jnp.dot(x_lo, wu_b[s, 0:c, :], preferred_element_type=F32)
              + jnp.dot(x_hi, wu_b[s, c:, :], preferred_element_type=F32))
        hb = (jax.nn.silu(hg) * hu).astype(BF16)
        ys_ref[...] = _pack_halves(jnp.dot(hb, wd_b[s], preferred_element_type=F32))

    @pl.when(i < n_act)
    def _():
        s = par_ref[i]
        more = nxt_ref[i] >= 0

        @pl.when(i == 0)
        def _():
            for cp in weight_copies(blk_e_ref[0], 0):
                cp.start()
            for cp in weight_copies(0, 0):
                cp.wait()
            cast_weights(0)

        @pl.when((first_ref[i] == 1) & more)
        def _():
            for cp in weight_copies(nxt_ref[i], 1 - s):
                cp.start()

        cast_next = (last_ref[i] == 1) & more

        @pl.when(cast_next)
        def _():
            for cp in weight_copies(0, 1 - s):
                cp.wait()
            compute_block(s)
            cast_weights(1 - s)

        @pl.when(jnp.logical_not(cast_next))
        def _():
            compute_block(s)


def _experts(xs, blk_e, first, seg_par, nxt_e, n_act, nblk, wg, wu, wd, layer):
    n_slots, c = xs.shape
    _, _, d, de = wg.shape
    last = jnp.roll(first, -1).at[nblk - 1].set(1)
    active_rows = lambda i, be, fi, la, pa, nx, na: (jnp.minimum(i, na[0] - 1), 0)
    return pl.pallas_call(
        functools.partial(_experts_kernel, layer=layer),
        grid_spec=pltpu.PrefetchScalarGridSpec(
            num_scalar_prefetch=6,
            grid=(nblk,),
            in_specs=[
                pl.BlockSpec((MOE_BLK, c), active_rows),
                pl.BlockSpec(memory_space=pl.ANY),
                pl.BlockSpec(memory_space=pl.ANY),
                pl.BlockSpec(memory_space=pl.ANY),
            ],
            out_specs=pl.BlockSpec((MOE_BLK, c), lambda i, *_: (i, 0)),
            scratch_shapes=[
                pltpu.VMEM((2, d, de), F32),
                pltpu.VMEM((2, d, de), F32),
                pltpu.VMEM((2, de, d), F32),
                pltpu.VMEM((2, d, de), BF16),
                pltpu.VMEM((2, d, de), BF16),
                pltpu.VMEM((2, de, d), BF16),
                pltpu.SemaphoreType.DMA((2,)),
            ],
        ),
        out_shape=jax.ShapeDtypeStruct((n_slots, c), U32),
        compiler_params=pltpu.CompilerParams(
            dimension_semantics=("arbitrary",), vmem_limit_bytes=VMEM_LIMIT),
        name="experts",
    )(blk_e, first, last, seg_par, nxt_e, n_act, xs, wg, wu, wd)


def _combine_kernel(dest_ref, h_ref, route_ref, ys_hbm, *rest, tm, nsteps, emit_next):
    if emit_next:
        gn_ref, o_ref, xn_ref, gbuf, sem = rest
    else:
        o_ref, gbuf, sem = rest
    i = pl.program_id(0)

    def start_gathers(tile, sl):
        for rr in range(tm):
            for k in range(TOP_K):
                src = dest_ref[k * tm * nsteps + tile * tm + rr]
                pltpu.make_async_copy(ys_hbm.at[pl.ds(src, 1)], gbuf.at[sl, pl.ds(k * tm + rr, 1)],
                                      sem.at[sl]).start()

    def wait_gathers(sl):
        for k in range(TOP_K):
            pltpu.make_async_copy(ys_hbm.at[pl.ds(0, tm)], gbuf.at[sl, pl.ds(k * tm, tm)],
                                  sem.at[sl]).wait()

    @pl.when(i < nsteps)
    def _():
        start_gathers(i, i % 2)

    @pl.when(i >= 1)
    def _():
        sl = (i - 1) % 2
        wait_gathers(sl)
        r = route_ref[...]
        lane = lax.broadcasted_iota(jnp.int32, r.shape, 1)
        g0 = jnp.sum(jnp.where(lane == 2, r, 0.0), axis=-1, keepdims=True)
        g1 = jnp.sum(jnp.where(lane == 3, r, 0.0), axis=-1, keepdims=True)
        a_lo, a_hi = _unpack_halves(gbuf[sl, 0:tm])
        b_lo, b_hi = _unpack_halves(gbuf[sl, tm:2 * tm])
        c = a_lo.shape[1]
        h_lo = h_ref[:, 0:c] + (a_lo * g0 + b_lo * g1)
        h_hi = h_ref[:, c:] + (a_hi * g0 + b_hi * g1)
        o_ref[:, 0:c] = h_lo
        o_ref[:, c:] = h_hi
        if emit_next:
            ms = (jnp.sum(h_lo * h_lo, axis=-1, keepdims=True)
                  + jnp.sum(h_hi * h_hi, axis=-1, keepdims=True)) / (2 * c)
            inv = lax.rsqrt(ms + EPS)
            xn_ref[:, 0:c] = (h_lo * inv * gn_ref[:, 0:c]).astype(BF16)
            xn_ref[:, c:] = (h_hi * inv * gn_ref[:, c:]).astype(BF16)


def _combine(dest, h1, route, ys, next_gain=None, tm=256):
    t, d = h1.shape
    c = ys.shape[1]
    nsteps = t // tm
    emit_next = next_gain is not None
    prev = lambda i, dst: (jnp.maximum(i - 1, 0), 0)
    in_specs = [
        pl.BlockSpec((tm, d), prev),
        pl.BlockSpec((tm, LANES), prev),
        pl.BlockSpec(memory_space=pl.ANY),
    ]
    out_specs = [pl.BlockSpec((tm, d), prev)]
    out_shape = [jax.ShapeDtypeStruct((t, d), F32)]
    args = [dest, h1, route, ys]
    if emit_next:
        in_specs.append(pl.BlockSpec((1, d), lambda i, dst: (0, 0)))
        out_specs.append(pl.BlockSpec((tm, d), prev))
        out_shape.append(jax.ShapeDtypeStruct((t, d), BF16))
        args.append(next_gain.reshape(1, d))
    return pl.pallas_call(
        functools.partial(_combine_kernel, tm=tm, nsteps=nsteps, emit_next=emit_next),
        grid_spec=pltpu.PrefetchScalarGridSpec(
            num_scalar_prefetch=1,
            grid=(nsteps + 1,),
            in_specs=in_specs,
            out_specs=out_specs,
            scratch_shapes=[pltpu.VMEM((2, TOP_K * tm, c), U32), pltpu.SemaphoreType.DMA((2,))],
        ),
        out_shape=out_shape,
        compiler_params=pltpu.CompilerParams(
            dimension_semantics=("arbitrary",), vmem_limit_bytes=VMEM_LIMIT),
        name="combine",
    )(*args)


def kernel(x, norm1, w_in, sgu_norm, sgu_w, sgu_b, conv_w, conv_b, conv_ln_g, conv_ln_b,
           q_norm, k_norm, out_norm, w_out, norm2, w_router_group, b_router_group,
           w_router_expert, b_router_expert, w_expert_gate, w_expert_up, w_expert_down):
    b, s, d = x.shape
    depth = norm1.shape[0]
    t = b * s
    h = x.reshape(t, d)
    a_end, b_end = SGU_WIDTH, SGU_WIDTH + CONV_CH
    x_in = h
    for l in range(depth):
        proj = _in_proj(x_in, norm1[l], w_in, l)
        y_ab = _mix_ab(proj, sgu_norm[l], sgu_w[l], sgu_b[l], conv_w[l], conv_b[l],
                       conv_ln_g[l], conv_ln_b[l], out_norm[l, :a_end], out_norm[l, a_end:b_end])
        y_c = _attention(proj, q_norm[l], k_norm[l])
        rw = jnp.concatenate([w_router_group[l], w_router_expert[l]], axis=1)
        rb = jnp.concatenate([b_router_group[l], b_router_expert[l]], axis=0)
        h1, xp, route, route_t, cnt = _out_proj(y_ab, y_c, out_norm[l, b_end:], h,
                                                w_out[l].astype(BF16), norm2[l], rw, rb)
        (dest, pad_start, pad_n, blk_e, first, seg_par, nxt_e, n_act,
         nblk) = _route_plan(route_t, cnt, t)
        xs = _dispatch(dest, pad_start, pad_n, n_act, xp, nblk)
        ys = _experts(xs, blk_e, first, seg_par, nxt_e, n_act, nblk,
                      w_expert_gate, w_expert_up, w_expert_down, l)
        if l + 1 < depth:
            h, x_in = _combine(dest, h1, route, ys, next_gain=norm1[l + 1])
        else:
            (h,) = _combine(dest, h1, route, ys)
    return h.reshape(b, s, d)
```
